```python
import jax, jax.numpy as jnp
from jax import lax
import numpy as np

D_MODEL = 1024
BATCH = 4
SEQ = 8192
DEPTH = 2

CHUNK = 64
N_MEM = 256
A_GROUPS = 4
A_GROUP_DIM = 128
A_WIDTH = A_GROUPS * A_GROUP_DIM
A_BLOCK = 128
FOX_HEADS = 8
FOX_HEAD_DIM = 64
FOX_WIDTH = FOX_HEADS * FOX_HEAD_DIM
Q_BLOCK = 128
POOL_WINDOWS = (2, 4, 8, 16)
POOL_GROUP_DIM = 128
POOL_WIDTH = len(POOL_WINDOWS) * POOL_GROUP_DIM
CONV_WIDTH = 512
CONV_K = 3
MIX_WIDTH = 1024
EVEN_IN = 2 * A_WIDTH + 3 * FOX_WIDTH + FOX_HEADS
ODD_IN = POOL_WIDTH + 3 * CONV_WIDTH
XA_HEADS = 4
XA_HEAD_DIM = 128
XA_WIDTH = XA_HEADS * XA_HEAD_DIM
D_FF = -(-(8 * D_MODEL) // (3 * 256)) * 256
EPS = 1e-6
N_EVEN = (DEPTH + 1) // 2
N_ODD = DEPTH // 2

kernel_name = "hybrid_streaming_gmlp_fox_pool_conv"


def rms_norm(x, g):
    xf = x.astype(jnp.float32)
    y = xf * lax.rsqrt(jnp.mean(xf * xf, axis=-1, keepdims=True) + EPS)
    return (y * g.astype(jnp.float32)).astype(x.dtype)


def chunk_gmlp(u, v, g_v, w_s, b_s):
    B, S, _ = v.shape
    v = rms_norm(v.reshape(B, S, A_GROUPS, A_GROUP_DIM), g_v.reshape(A_GROUPS, A_GROUP_DIM))
    vb = v.reshape(B, S // A_BLOCK, A_BLOCK, A_GROUPS, A_GROUP_DIM)
    cpos = np.arange(A_BLOCK) // CHUNK
    mask = jnp.asarray(cpos[:, None] >= cpos[None, :]).astype(w_s.dtype)
    w = w_s * mask[None]
    s = jnp.einsum('gts,bnsgc->bntgc', w, vb) + b_s.T[None, None, :, :, None]
    return u * s.reshape(B, S, A_WIDTH)


def forgetting_attention(q, k, v, log_f):
    B, S, H, hd = q.shape
    nb = S // Q_BLOCK
    scale = 1.0 / np.sqrt(hd)
    c = jnp.cumsum(log_f.astype(jnp.float32), axis=1).transpose(0, 2, 1)
    qb = q.reshape(B, nb, Q_BLOCK, H, hd).transpose(1, 0, 2, 3, 4)
    cb = c.reshape(B, H, nb, Q_BLOCK).transpose(2, 0, 1, 3)
    kpos = jnp.arange(S)

    def block(args):
        qi, ci, i = args
        logits = jnp.einsum('bqhd,bkhd->bhqk', qi, k).astype(jnp.float32) * scale
        logits = logits + ci[..., :, None] - c[:, :, None, :]
        qpos = i * Q_BLOCK + jnp.arange(Q_BLOCK)
        allowed = kpos[None, :] <= qpos[:, None]
        logits = jnp.where(allowed[None, None], logits, -jnp.inf)
        p = jax.nn.softmax(logits, axis=-1)
        return jnp.einsum('bhqk,bkhd->bqhd', p.astype(v.dtype), v)

    out = lax.map(block, (qb, cb, jnp.arange(nb)))
    return out.transpose(1, 0, 2, 3, 4).reshape(B, S, H * hd)


def multiscale_pool(z, w_pool, s_pool):
    B, S, _ = z.shape
    zf = z.astype(jnp.float32)
    cs = jnp.cumsum(zf, axis=1)
    pos = jnp.arange(S)
    outs = []
    for g, w in enumerate(POOL_WINDOWS):
        sl = slice(g * POOL_GROUP_DIM, (g + 1) * POOL_GROUP_DIM)
        csg = cs[..., sl]
        lag = jnp.pad(csg, ((0, 0), (w, 0), (0, 0)))[:, :S]
        cnt = jnp.minimum(pos + 1, w).astype(jnp.float32)[None, :, None]
        outs.append((csg - lag) / cnt - zf[..., sl])
    p = jnp.concatenate(outs, axis=-1).astype(z.dtype)
    p = p.reshape(B, S, len(POOL_WINDOWS), POOL_GROUP_DIM)
    y = jnp.einsum('bsgc,gcd->bsgd', p, w_pool).reshape(B, S, POOL_WIDTH)
    return y * s_pool


def short_gated_conv(h, gate_b, gate_c, conv_w):
    S = h.shape[1]
    xg = gate_c * h
    xp = jnp.pad(xg, ((0, 0), (CONV_K - 1, 0), (0, 0)))
    conv = sum(conv_w[j] * xp[:, j:j + S] for j in range(CONV_K))
    return gate_b * conv


def even_mixer(h, w_in, b_f, g_v, w_s, b_s, g_qn, g_kn, w_out):
    B, S, _ = h.shape
    z = h @ w_in
    uv = jax.nn.gelu(z[..., :2 * A_WIDTH])
    u, v = uv[..., :A_WIDTH], uv[..., A_WIDTH:]
    o = 2 * A_WIDTH
    q = z[..., o:o + FOX_WIDTH].reshape(B, S, FOX_HEADS, FOX_HEAD_DIM)
    k = z[..., o + FOX_WIDTH:o + 2 * FOX_WIDTH].reshape(B, S, FOX_HEADS, FOX_HEAD_DIM)
    vv = z[..., o + 2 * FOX_WIDTH:o + 3 * FOX_WIDTH].reshape(B, S, FOX_HEADS, FOX_HEAD_DIM)
    f_logit = z[..., o + 3 * FOX_WIDTH:].astype(jnp.float32) + b_f.astype(jnp.float32)
    log_f = jax.nn.log_sigmoid(f_logit)
    q = rms_norm(q, g_qn)
    k = rms_norm(k, g_kn)
    y_a = chunk_gmlp(u, v, g_v, w_s, b_s)
    y_b = forgetting_attention(q, k, vv, log_f).astype(h.dtype)
    return jnp.concatenate([y_a, y_b], axis=-1) @ w_out


def odd_mixer(h, w_in, w_pool, s_pool, conv_w, w_out):
    z = h @ w_in
    zc = z[..., :POOL_WIDTH]
    o = POOL_WIDTH
    hd = z[..., o:o + CONV_WIDTH]
    gb = z[..., o + CONV_WIDTH:o + 2 * CONV_WIDTH]
    gc = z[..., o + 2 * CONV_WIDTH:]
    y_c = multiscale_pool(zc, w_pool, s_pool)
    y_d = short_gated_conv(hd, gb, gc, conv_w)
    return jnp.concatenate([y_c, y_d], axis=-1) @ w_out


def memory_cross_attention(h, m, w_q, w_kv, w_o, g_qn, g_kn):
    B, S, _ = h.shape
    M = m.shape[1]
    q = rms_norm((h @ w_q).reshape(B, S, XA_HEADS, XA_HEAD_DIM), g_qn)
    kv = m @ w_kv
    k = rms_norm(kv[..., :XA_WIDTH].reshape(B, M, XA_HEADS, XA_HEAD_DIM), g_kn)
    v = kv[..., XA_WIDTH:].reshape(B, M, XA_HEADS, XA_HEAD_DIM)
    logits = jnp.einsum('bshd,bmhd->bhsm', q, k).astype(jnp.float32) / np.sqrt(XA_HEAD_DIM)
    p = jax.nn.softmax(logits, axis=-1).astype(v.dtype)
    o = jnp.einsum('bhsm,bmhd->bshd', p, v).reshape(B, S, XA_WIDTH)
    return o @ w_o


def swiglu(h, w_gate, w_up, w_down):
    return (jax.nn.silu(h @ w_gate) * (h @ w_up)) @ w_down


def setup_inputs(seed: int = 0) -> dict:
    key = jax.random.key(seed)
    ks = jax.random.split(key, 32)
    f32 = jnp.float32

    def nrm(k, shape, scale):
        return jax.random.normal(k, shape, f32) * scale

    def gain(k, shape):
        return 1.0 + 0.05 * jax.random.normal(k, shape, f32)

    L, NE, NO, D = DEPTH, N_EVEN, N_ODD, D_MODEL
    return {
        "x": nrm(ks[0], (BATCH, SEQ, D), 1.0),
        "mem": nrm(ks[1], (BATCH, N_MEM, D), 1.0),
        "g_mix": gain(ks[2], (L, D)),
        "g_xa": gain(ks[3], (L, D)),
        "g_mem": gain(ks[4], (L, D)),
        "xa_wq": nrm(ks[5], (L, D, XA_WIDTH), D ** -0.5),
        "xa_wkv": nrm(ks[6], (L, D, 2 * XA_WIDTH), D ** -0.5),
        "xa_wo": nrm(ks[7], (L, XA_WIDTH, D), XA_WIDTH ** -0.5),
        "xa_gq": gain(ks[8], (L, XA_HEAD_DIM)),
        "xa_gk": gain(ks[9], (L, XA_HEAD_DIM)),
        "g_ffn": gain(ks[10], (L, D)),
        "w_gate": nrm(ks[11], (L, D, D_FF), D ** -0.5),
        "w_up": nrm(ks[12], (L, D, D_FF), D ** -0.5),
        "w_down": nrm(ks[13], (L, D_FF, D), D_FF ** -0.5),
        "e_w_in": nrm(ks[14], (NE, D, EVEN_IN), D ** -0.5),
        "e_b_f": jnp.linspace(1.0, 6.0, FOX_HEADS, dtype=f32)[None] + 0.1 * jax.random.normal(ks[15], (NE, FOX_HEADS), f32),
        "e_g_v": gain(ks[16], (NE, A_WIDTH)),
        "e_w_s": nrm(ks[17], (NE, A_GROUPS, A_BLOCK, A_BLOCK), A_BLOCK ** -0.5),
        "e_b_s": gain(ks[18], (NE, A_GROUPS, A_BLOCK)),
        "e_g_qn": gain(ks[19], (NE, FOX_HEAD_DIM)),
        "e_g_kn": gain(ks[20], (NE, FOX_HEAD_DIM)),
        "e_w_out": nrm(ks[21], (NE, MIX_WIDTH, D), MIX_WIDTH ** -0.5),
        "o_w_in": nrm(ks[22], (NO, D, ODD_IN), D ** -0.5),
        "o_w_pool": nrm(ks[23], (NO, len(POOL_WINDOWS), POOL_GROUP_DIM, POOL_GROUP_DIM), POOL_GROUP_DIM ** -0.5),
        "o_s_pool": gain(ks[24], (NO, POOL_WIDTH)),
        "o_conv_w": nrm(ks[25], (NO, CONV_K, CONV_WIDTH), CONV_K ** -0.5),
        "o_w_out": nrm(ks[26], (NO, MIX_WIDTH, D), MIX_WIDTH ** -0.5),
    }


def reference(x, mem, g_mix, g_xa, g_mem, xa_wq, xa_wkv, xa_wo, xa_gq, xa_gk,
              g_ffn, w_gate, w_up, w_down,
              e_w_in, e_b_f, e_g_v, e_w_s, e_b_s, e_g_qn, e_g_kn, e_w_out,
              o_w_in, o_w_pool, o_s_pool, o_conv_w, o_w_out):
    for layer in range(DEPTH):
        i = layer // 2
        h = rms_norm(x, g_mix[layer])
        if layer % 2 == 0:
            y = even_mixer(h, e_w_in[i], e_b_f[i], e_g_v[i], e_w_s[i], e_b_s[i],
                           e_g_qn[i], e_g_kn[i], e_w_out[i])
        else:
            y = odd_mixer(h, o_w_in[i], o_w_pool[i], o_s_pool[i], o_conv_w[i], o_w_out[i])
        x = x + y
        m = rms_norm(mem, g_mem[layer])
        x = x + memory_cross_attention(rms_norm(x, g_xa[layer]), m, xa_wq[layer], xa_wkv[layer],
                                       xa_wo[layer], xa_gq[layer], xa_gk[layer])
        x = x + swiglu(rms_norm(x, g_ffn[layer]), w_gate[layer], w_up[layer], w_down[layer])
    return x
```

```python
import functools
import math

import jax
import jax.numpy as jnp
from jax import lax
from jax.experimental import pallas as pl
from jax.experimental.pallas import tpu as pltpu

F32 = jnp.float32
BF16 = jnp.bfloat16
EPS = 1e-6
LOG2E = 1.4426950408889634
NEG_BIG = -1e30

LANES = 128
V7X_VMEM_BYTES = 64 * 1024 * 1024

ROW_TILE = 512
FOX_BLOCK = 256
POOL_HALO = 16
CONV_HALO = 8
FF_CHUNK = 512


def _vmem_limit(nbytes):
    return int(min(nbytes, V7X_VMEM_BYTES - 4 * 1024 * 1024))


def _rms(x, g):
    ms = jnp.mean(x * x, axis=-1, keepdims=True)
    return (x * lax.rsqrt(ms + EPS)) * g


def _dot(a, b):
    return jnp.dot(a, b, preferred_element_type=F32)


def _dot_nt(a, b):
    return lax.dot_general(a, b, (((1,), (1,)), ((), ())), preferred_element_type=F32)


def _split3(x):
    hi = x.astype(BF16)
    r1 = x - hi.astype(F32)
    mid = r1.astype(BF16)
    lo = (r1 - mid.astype(F32)).astype(BF16)
    return hi, mid, lo


def _const_spec(shape, single=False):
    nd = len(shape)
    kw = {}
    if single:
        kw["pipeline_mode"] = pl.Buffered(1)
    return pl.BlockSpec(shape, lambda *_: (0,) * nd, **kw)


def _even_in_kernel(x_ref, g_ref, w_ref, wf_ref, bf_ref, gv_ref, ws_ref, bst_ref, gq_ref, gk_ref,
                    tri_ref, ya_ref, qp_ref, kp_ref, v_ref, cb_ref, run_ref):
    t = pl.program_id(1)
    tm = x_ref.shape[1]
    a_w = ya_ref.shape[2]
    f_w = v_ref.shape[2]
    n_grp = a_w // LANES
    blk_per_tile = tm // FOX_BLOCK

    @pl.when(t == 0)
    def _():
        run_ref[...] = jnp.zeros_like(run_ref)

    h = _rms(x_ref[0], g_ref[...]).astype(BF16)
    z = _dot(h, w_ref[...])
    fl = _dot(h, wf_ref[...]) + bf_ref[...]
    logf = -(jnp.maximum(-fl, 0.0) + jnp.log1p(jnp.exp(-jnp.abs(fl)))) * LOG2E

    uv = jax.nn.gelu(z[:, :2 * a_w])
    row = lax.broadcasted_iota(jnp.int32, (LANES, LANES), 0) // 64
    col = lax.broadcasted_iota(jnp.int32, (LANES, LANES), 1) // 64
    chunk_mask = row >= col
    for g in range(n_grp):
        sl = slice(g * LANES, (g + 1) * LANES)
        vg = uv[:, a_w + g * LANES:a_w + (g + 1) * LANES]
        vn = _rms(vg, gv_ref[:, sl]).astype(BF16)
        wm = jnp.where(chunk_mask, ws_ref[g], 0.0).astype(BF16)
        bias = bst_ref[:, g:g + 1]
        for n in range(tm // LANES):
            rs = slice(n * LANES, (n + 1) * LANES)
            s = _dot(wm, vn[rs]) + bias
            ya_ref[0, rs, sl] = (uv[rs, sl] * s).astype(BF16)

    a_parts, b_parts = [], []
    for r in range(blk_per_tile):
        lf = logf[r * FOX_BLOCK:(r + 1) * FOX_BLOCK]
        hi, mid, lo = _split3(lf)
        tri = tri_ref[...]
        lc = _dot(tri, hi) + _dot(tri, mid) + _dot(tri, lo)
        first = lf[0:1]
        tot = lc[FOX_BLOCK - 1:FOX_BLOCK]
        a_parts.append(lc - first)
        b_parts.append(tot - lc)
        run = run_ref[...]
        rs8 = slice(r * 8, (r + 1) * 8)
        cb_ref[0, rs8, 0:LANES] = jnp.broadcast_to(run + first, (8, LANES))
        cb_ref[0, rs8, LANES:2 * LANES] = jnp.broadcast_to(run + tot, (8, LANES))
        run_ref[...] = run + tot
    a_all = jnp.concatenate(a_parts, axis=0)
    b_all = jnp.concatenate(b_parts, axis=0)

    lane = lax.broadcasted_iota(jnp.int32, (tm, LANES), 1)
    low = lane < 64
    hd = 64
    q_off = 2 * a_w
    k_off = 2 * a_w + f_w
    v_off = 2 * a_w + 2 * f_w

    def head_norm(blk, gain):
        sq = blk * blk
        s_lo = jnp.sum(jnp.where(low, sq, 0.0), axis=-1, keepdims=True)
        s_hi = jnp.sum(jnp.where(low, 0.0, sq), axis=-1, keepdims=True)
        r = jnp.where(low, lax.rsqrt(s_lo / hd + EPS), lax.rsqrt(s_hi / hd + EPS))
        return (blk * r) * gain

    def extras(col_vals, base, vals_first):
        bc = jnp.broadcast_to(col_vals, (tm, LANES))
        hi, mid, lo = _split3(bc)
        o = 0 if vals_first else 3
        e = jnp.where(lane == base + o, hi.astype(F32),
                      jnp.where(lane == base + o + 1, mid.astype(F32),
                                jnp.where(lane == base + o + 2, lo.astype(F32), 0.0)))
        ones_lo = base + (3 if vals_first else 0)
        return jnp.where((lane >= ones_lo) & (lane < ones_lo + 3), 1.0, e)

    for j in range(f_w // LANES):
        sl = slice(j * LANES, (j + 1) * LANES)
        qn = head_norm(z[:, q_off + j * LANES:q_off + (j + 1) * LANES], gq_ref[...]) * (LOG2E / math.sqrt(hd))
        kn = head_norm(z[:, k_off + j * LANES:k_off + (j + 1) * LANES], gk_ref[...])
        for hh in range(2):
            hidx = 2 * j + hh
            qhalf = low if hh == 0 else jnp.logical_not(low)
            base = 64 if hh == 0 else 0
            e_q = extras(a_all[:, hidx:hidx + 1], base, True)
            e_k = extras(b_all[:, hidx:hidx + 1], base, False)
            osl = slice(hidx * LANES, (hidx + 1) * LANES)
            qp_ref[0, :, osl] = jnp.where(qhalf, qn, e_q).astype(BF16)
            kp_ref[0, :, osl] = jnp.where(qhalf, kn, e_k).astype(BF16)
    v_ref[0] = z[:, v_off:v_off + f_w].astype(BF16)


def _even_in(x, g_mix, w_main, w_f, b_f, g_v, w_s, b_s_t, g_q2, g_k2, tri):
    B, S, D = x.shape
    tm = ROW_TILE
    nblk = S // FOX_BLOCK
    a_w = g_v.shape[1]
    f_w = (w_main.shape[1] - 2 * a_w) // 3
    n_heads = f_w // 64
    grid = (B, S // tm)
    row3 = lambda w: pl.BlockSpec((1, tm, w), lambda b, t: (b, t, 0))
    blk_rows = 8 * (tm // FOX_BLOCK)
    return pl.pallas_call(
        _even_in_kernel,
        grid=grid,
        in_specs=[row3(D), _const_spec(g_mix.shape), _const_spec(w_main.shape), _const_spec(w_f.shape),
                  _const_spec(b_f.shape), _const_spec(g_v.shape), _const_spec(w_s.shape),
                  _const_spec(b_s_t.shape), _const_spec(g_q2.shape), _const_spec(g_k2.shape),
                  _const_spec(tri.shape)],
        out_specs=[row3(a_w), row3(n_heads * LANES), row3(n_heads * LANES), row3(f_w),
                   pl.BlockSpec((1, blk_rows, 2 * LANES), lambda b, t: (b, t, 0))],
        out_shape=[jax.ShapeDtypeStruct((B, S, a_w), BF16),
                   jax.ShapeDtypeStruct((B, S, n_heads * LANES), BF16),
                   jax.ShapeDtypeStruct((B, S, n_heads * LANES), BF16),
                   jax.ShapeDtypeStruct((B, S, f_w), BF16),
                   jax.ShapeDtypeStruct((B, nblk * 8, 2 * LANES), F32)],
        scratch_shapes=[pltpu.VMEM((1, LANES), F32)],
        compiler_params=pltpu.CompilerParams(
            dimension_semantics=("arbitrary", "arbitrary"),
            vmem_limit_bytes=_vmem_limit(48 * 1024 * 1024)),
        name="even_in",
    )(x, g_mix, w_main, w_f, b_f, g_v, w_s, b_s_t, g_q2, g_k2, tri)


def _fox_kernel(cb_ref, qp_ref, kp_ref, v_ref, o_ref, m_scr, l_scr, acc_scr, *, nblk, n_heads):
    b = pl.program_id(0)
    hp = pl.program_id(1)
    tb = FOX_BLOCK
    rowi = lax.broadcasted_iota(jnp.int32, (tb, tb), 0)
    coli = lax.broadcasted_iota(jnp.int32, (tb, tb), 1)
    causal = rowi >= coli
    lane = lax.broadcasted_iota(jnp.int32, (tb, LANES), 1)

    def q_body(i, carry):
        q0 = pl.multiple_of(i * tb, tb)
        for hh in range(2):
            m_scr[hh] = jnp.full((tb, 1), NEG_BIG, F32)
            l_scr[hh] = jnp.zeros((tb, 1), F32)
            acc_scr[hh] = jnp.zeros((tb, LANES), F32)

        def kv_step(j, masked):
            k0 = pl.multiple_of(j * tb, tb)
            vblk = v_ref[0, pl.ds(k0, tb), :]
            for hh in range(2):
                head = hp * 2 + hh
                qb = qp_ref[0, pl.ds(q0, tb), hh * LANES:(hh + 1) * LANES]
                kb = kp_ref[0, pl.ds(k0, tb), hh * LANES:(hh + 1) * LANES]
                s = _dot_nt(qb, kb)
                d = (cb_ref[(b * nblk + i) * 2 * n_heads + head]
                     - cb_ref[(b * nblk + j) * 2 * n_heads + n_heads + head])
                if masked:
                    s = jnp.where(causal, s, NEG_BIG)
                m_old = m_scr[hh]
                m_new = jnp.maximum(m_old, jnp.max(s, axis=-1, keepdims=True) + d)
                alpha = jnp.exp2(m_old - m_new)
                p = jnp.exp2(s - (m_new - d))
                l_scr[hh] = alpha * l_scr[hh] + jnp.sum(p, axis=-1, keepdims=True)
                acc_scr[hh] = alpha * acc_scr[hh] + _dot(p.astype(BF16), vblk)
                m_scr[hh] = m_new

        def full_step(j, c):
            kv_step(j, False)
            return c

        lax.fori_loop(0, i, full_step, 0)
        kv_step(i, True)
        o0 = acc_scr[0] * (1.0 / l_scr[0])
        o1 = acc_scr[1] * (1.0 / l_scr[1])
        o_ref[0, pl.ds(q0, tb), :] = jnp.where(lane < 64, o0, o1).astype(BF16)
        return carry

    lax.fori_loop(0, nblk, q_body, 0)


def _fox(cb_flat, qp, kp, v):
    B, S, HW = qp.shape
    n_heads = HW // LANES
    nblk = S // FOX_BLOCK
    grid = (B, n_heads // 2)
    return pl.pallas_call(
        functools.partial(_fox_kernel, nblk=nblk, n_heads=n_heads),
        grid=grid,
        in_specs=[pl.BlockSpec(memory_space=pltpu.SMEM),
                  pl.BlockSpec((1, S, 2 * LANES), lambda b, h: (b, 0, h)),
                  pl.BlockSpec((1, S, 2 * LANES), lambda b, h: (b, 0, h)),
                  pl.BlockSpec((1, S, LANES), lambda b, h: (b, 0, h))],
        out_specs=pl.BlockSpec((1, S, LANES), lambda b, h: (b, 0, h)),
        out_shape=jax.ShapeDtypeStruct((B, S, n_heads * 64), BF16),
        scratch_shapes=[pltpu.VMEM((2, FOX_BLOCK, 1), F32), pltpu.VMEM((2, FOX_BLOCK, 1), F32),
                        pltpu.VMEM((2, FOX_BLOCK, LANES), F32)],
        compiler_params=pltpu.CompilerParams(
            dimension_semantics=("arbitrary", "arbitrary"),
            vmem_limit_bytes=_vmem_limit(48 * 1024 * 1024)),
        name="fox_attn",
    )(cb_flat, qp, kp, v)


def _odd_in_kernel(x_ref, g_ref, w_ref, wp_ref, sp_ref, cw_ref, yc_ref, yd_ref, zbuf, xbuf, *, windows):
    t = pl.program_id(1)
    tm = x_ref.shape[1]
    pw = yc_ref.shape[2]
    cwid = yd_ref.shape[2]

    @pl.when(t == 0)
    def _():
        zbuf[0:POOL_HALO, :] = jnp.zeros((POOL_HALO, pw), F32)
        xbuf[0:CONV_HALO, :] = jnp.zeros((CONV_HALO, cwid), F32)

    h = _rms(x_ref[0], g_ref[...]).astype(BF16)
    z = _dot(h, w_ref[...])
    zc = z[:, :pw]
    hdn = z[:, pw:pw + cwid]
    gb = z[:, pw + cwid:pw + 2 * cwid]
    gc = z[:, pw + 2 * cwid:pw + 3 * cwid]

    zbuf[POOL_HALO:POOL_HALO + tm, :] = zc
    pos = t * tm + lax.broadcasted_iota(jnp.int32, (tm, 1), 0)
    for g, w in enumerate(windows):
        sl = slice(g * LANES, (g + 1) * LANES)
        acc = zbuf[POOL_HALO:POOL_HALO + tm, sl]
        for j in range(1, w):
            acc = acc + zbuf[POOL_HALO - j:POOL_HALO - j + tm, sl]
        inv_cnt = 1.0 / jnp.minimum(pos + 1, w).astype(F32)
        p = acc * inv_cnt - zc[:, sl]
        yc_ref[0, :, sl] = (_dot(p.astype(BF16), wp_ref[g]) * sp_ref[:, sl]).astype(BF16)
    zbuf[0:POOL_HALO, :] = zbuf[tm:tm + POOL_HALO, :]

    xg = gc * hdn
    xbuf[CONV_HALO:CONV_HALO + tm, :] = xg
    k = cw_ref.shape[0]
    conv = cw_ref[k - 1:k, :] * xg
    for j in range(1, k):
        conv = conv + cw_ref[k - 1 - j:k - j, :] * xbuf[CONV_HALO - j:CONV_HALO - j + tm, :]
    yd_ref[0] = (gb * conv).astype(BF16)
    xbuf[0:CONV_HALO, :] = xbuf[tm:tm + CONV_HALO, :]


def _odd_in(x, g_mix, w_in, w_pool, s_pool, conv_w, windows):
    B, S, D = x.shape
    tm = ROW_TILE
    pw = s_pool.shape[1]
    cwid = conv_w.shape[1]
    assert max(windows) <= POOL_HALO and conv_w.shape[0] - 1 <= CONV_HALO
    row3 = lambda w: pl.BlockSpec((1, tm, w), lambda b, t: (b, t, 0))
    return pl.pallas_call(
        functools.partial(_odd_in_kernel, windows=windows),
        grid=(B, S // tm),
        in_specs=[row3(D), _const_spec(g_mix.shape), _const_spec(w_in.shape), _const_spec(w_pool.shape),
                  _const_spec(s_pool.shape), _const_spec(conv_w.shape)],
        out_specs=[row3(pw), row3(cwid)],
        out_shape=[jax.ShapeDtypeStruct((B, S, pw), BF16), jax.ShapeDtypeStruct((B, S, cwid), BF16)],
        scratch_shapes=[pltpu.VMEM((POOL_HALO + tm, pw), F32), pltpu.VMEM((CONV_HALO + tm, cwid), F32)],
        compiler_params=pltpu.CompilerParams(
            dimension_semantics=("arbitrary", "arbitrary"),
            vmem_limit_bytes=_vmem_limit(48 * 1024 * 1024)),
        name="odd_in",
    )(x, g_mix, w_in, w_pool, s_pool, conv_w)


def _mem_kv_kernel(m_ref, g_ref, w_ref, gk_ref, k_ref, v_ref):
    xa = k_ref.shape[2]
    hm = _rms(m_ref[0], g_ref[...]).astype(BF16)
    kv = _dot(hm, w_ref[...])
    for h in range(xa // LANES):
        sl = slice(h * LANES, (h + 1) * LANES)
        k_ref[0, :, sl] = _rms(kv[:, sl], gk_ref[...]).astype(BF16)
    v_ref[0] = kv[:, xa:].astype(BF16)


def _mem_kv(mem, g_mem, w_kv, g_k):
    B, M, D = mem.shape
    xa = w_kv.shape[1] // 2
    blk = lambda w: pl.BlockSpec((1, M, w), lambda b: (b, 0, 0))
    return pl.pallas_call(
        _mem_kv_kernel,
        grid=(B,),
        in_specs=[blk(D), _const_spec(g_mem.shape), _const_spec(w_kv.shape), _const_spec(g_k.shape)],
        out_specs=[blk(xa), blk(xa)],
        out_shape=[jax.ShapeDtypeStruct((B, M, xa), BF16), jax.ShapeDtypeStruct((B, M, xa), BF16)],
        compiler_params=pltpu.CompilerParams(dimension_semantics=("arbitrary",)),
        name="mem_kv",
    )(mem, g_mem, w_kv, g_k)


def _post_kernel(x_ref, ya_ref, yb_ref, woa_ref, wob_ref, gxa_ref, wq_ref, gq_ref, k_ref, v_ref, wo_ref,
                 gff_ref, wg_ref, wu_ref, wd_ref, o_ref, hs_scr):
    xa = wq_ref.shape[1]
    dff = wg_ref.shape[1]
    x1 = x_ref[0] + _dot(ya_ref[0], woa_ref[...]) + _dot(yb_ref[0], wob_ref[...])

    hx = _rms(x1, gxa_ref[...]).astype(BF16)
    q = _dot(hx, wq_ref[...])
    inv_sqrt = 1.0 / math.sqrt(LANES)
    outs = []
    for h in range(xa // LANES):
        sl = slice(h * LANES, (h + 1) * LANES)
        qn = _rms(q[:, sl], gq_ref[...]).astype(BF16)
        s = _dot_nt(qn, k_ref[0, :, sl]) * inv_sqrt
        m = jnp.max(s, axis=-1, keepdims=True)
        p = jnp.exp(s - m)
        l = jnp.sum(p, axis=-1, keepdims=True)
        outs.append((_dot(p.astype(BF16), v_ref[0, :, sl]) * (1.0 / l)).astype(BF16))
    x2 = x1 + _dot(jnp.concatenate(outs, axis=-1), wo_ref[...])

    hf = _rms(x2, gff_ref[...]).astype(BF16)
    c0 = 0
    while c0 < dff:
        c1 = min(c0 + FF_CHUNK, dff)
        a = _dot(hf, wg_ref[:, c0:c1])
        u = _dot(hf, wu_ref[:, c0:c1])
        hs_scr[:, c0:c1] = (a * jax.nn.sigmoid(a) * u).astype(BF16)
        c0 = c1
    o_ref[0] = x2 + _dot(hs_scr[...], wd_ref[...])


def _post(x, ya, yb, wo_a, wo_b, g_xa, w_q, g_q, k_mem, v_mem, w_o, g_ffn, w_gate, w_up, w_down):
    B, S, D = x.shape
    tm = ROW_TILE
    M = k_mem.shape[1]
    xa = w_q.shape[1]
    dff = w_gate.shape[1]
    row3 = lambda w: pl.BlockSpec((1, tm, w), lambda b, t: (b, t, 0))
    memb = pl.BlockSpec((1, M, xa), lambda b, t: (b, 0, 0))
    cs = lambda a: _const_spec(a.shape, single=True)
    return pl.pallas_call(
        _post_kernel,
        grid=(B, S // tm),
        in_specs=[row3(D), row3(ya.shape[2]), row3(yb.shape[2]), cs(wo_a), cs(wo_b), cs(g_xa), cs(w_q),
                  cs(g_q), memb, memb, cs(w_o), cs(g_ffn), cs(w_gate), cs(w_up), cs(w_down)],
        out_specs=row3(D),
        out_shape=jax.ShapeDtypeStruct((B, S, D), F32),
        scratch_shapes=[pltpu.VMEM((tm, dff), BF16)],
        compiler_params=pltpu.CompilerParams(
            dimension_semantics=("arbitrary", "arbitrary"),
            vmem_limit_bytes=_vmem_limit(58 * 1024 * 1024)),
        name="post",
    )(x, ya, yb, wo_a, wo_b, g_xa, w_q, g_q, k_mem, v_mem, w_o, g_ffn, w_gate, w_up, w_down)


def kernel(x, mem, g_mix, g_xa, g_mem, xa_wq, xa_wkv, xa_wo, xa_gq, xa_gk, g_ffn, w_gate, w_up, w_down,
           e_w_in, e_b_f, e_g_v, e_w_s, e_b_s, e_g_qn, e_g_kn, e_w_out,
           o_w_in, o_w_pool, o_s_pool, o_conv_w, o_w_out):
    depth = g_mix.shape[0]
    B, S, D = x.shape
    row = lambda a: a.reshape(1, -1)
    tri = (lax.broadcasted_iota(jnp.int32, (FOX_BLOCK, FOX_BLOCK), 0)
           >= lax.broadcasted_iota(jnp.int32, (FOX_BLOCK, FOX_BLOCK), 1)).astype(BF16)
    pool_windows = (2, 4, 8, 16)[:o_w_pool.shape[1]]

    for layer in range(depth):
        i = layer // 2
        if layer % 2 == 0:
            n_heads = e_b_f.shape[1]
            a_w = e_g_v.shape[1]
            n_main = e_w_in.shape[2] - n_heads
            w_main = e_w_in[i, :, :n_main].astype(BF16)
            w_f = jnp.pad(e_w_in[i, :, n_main:], ((0, 0), (0, LANES - n_heads))).astype(BF16)
            b_f = jnp.pad(e_b_f[i], (0, LANES - n_heads)).reshape(1, LANES)
            g_q2 = jnp.tile(e_g_qn[i], 2).reshape(1, LANES)
            g_k2 = jnp.tile(e_g_kn[i], 2).reshape(1, LANES)
            ya, qp, kp, vv, cb = _even_in(x, row(g_mix[layer]), w_main, w_f, b_f, row(e_g_v[i]), e_w_s[i],
                                          e_b_s[i].T, g_q2, g_k2, tri)
            cb = cb[:, ::8, :].reshape(-1, 2 * LANES)
            cb_flat = jnp.concatenate([cb[:, :n_heads], cb[:, LANES:LANES + n_heads]], axis=1).reshape(-1)
            yb = _fox(cb_flat, qp, kp, vv)
            w_out = e_w_out[i].astype(BF16)
        else:
            ya, yb = _odd_in(x, row(g_mix[layer]), o_w_in[i].astype(BF16), o_w_pool[i].astype(BF16),
                             row(o_s_pool[i]), o_conv_w[i], pool_windows)
            a_w = ya.shape[2]
            w_out = o_w_out[i].astype(BF16)
        k_mem, v_mem = _mem_kv(mem, row(g_mem[layer]), xa_wkv[layer].astype(BF16), row(xa_gk[layer]))
        x = _post(x, ya, yb, w_out[:a_w], w_out[a_w:], row(g_xa[layer]), xa_wq[layer].astype(BF16),
                  row(xa_gq[layer]), k_mem, v_mem, xa_wo[layer].astype(BF16), row(g_ffn[layer]),
                  w_gate[layer].astype(BF16), w_up[layer].astype(BF16), w_down[layer].astype(BF16))
    return x
```

```python
import functools
import math

import jax
import jax.numpy as jnp
from jax import lax
from jax.experimental import pallas as pl
from jax.experimental.pallas import tpu as pltpu

F32 = jnp.float32
BF16 = jnp.bfloat16
EPS = 1e-6
LOG2E = 1.4426950408889634
NEG_BIG = -1e30

LANES = 128
V7X_VMEM_BYTES = 64 * 1024 * 1024

ROW_TILE = 512
FOX_TQ = 1024
FOX_TK = ROW_TILE
POOL_HALO = 16
CONV_HALO = 8
FF_CHUNK = 512


def _vmem_limit(nbytes):
    return int(min(nbytes, V7X_VMEM_BYTES - 4 * 1024 * 1024))


def _rms(x, g):
    ms = jnp.mean(x * x, axis=-1, keepdims=True)
    return (x * lax.rsqrt(ms + EPS)) * g


def _dot(a, b):
    return jnp.dot(a, b, preferred_element_type=F32)


def _dot_nt(a, b):
    return lax.dot_general(a, b, (((1,), (1,)), ((), ())), preferred_element_type=F32)


def _split3(x):
    hi = x.astype(BF16)
    r1 = x - hi.astype(F32)
    mid = r1.astype(BF16)
    lo = (r1 - mid.astype(F32)).astype(BF16)
    return hi, mid, lo


def _const_spec(shape, single=False):
    nd = len(shape)
    kw = {}
    if single:
        kw["pipeline_mode"] = pl.Buffered(1)
    return pl.BlockSpec(shape, lambda *_: (0,) * nd, **kw)


def _even_in_kernel(x_ref, g_ref, w_ref, wf_ref, bf_ref, gv_ref, ws_ref, bst_ref, gq_ref, gk_ref,
                    tri_ref, ya_ref, qp_ref, kp_ref, vp_ref, cs_ref, ce_ref, run_ref, aoff_ref, cq0_ref,
                    *, tiles_per_q):
    t = pl.program_id(1)
    tm = x_ref.shape[1]
    a_w = ya_ref.shape[2]
    f_w = vp_ref.shape[2] // 2
    n_grp = a_w // LANES

    @pl.when(t == 0)
    def _():
        run_ref[...] = jnp.zeros_like(run_ref)

    h = _rms(x_ref[0], g_ref[...]).astype(BF16)
    z = _dot(h, w_ref[...])
    fl = _dot(h, wf_ref[...]) + bf_ref[...]
    logf = -(jnp.maximum(-fl, 0.0) + jnp.log1p(jnp.exp(-jnp.abs(fl)))) * LOG2E

    uv = jax.nn.gelu(z[:, :2 * a_w])
    row = lax.broadcasted_iota(jnp.int32, (LANES, LANES), 0) // 64
    col = lax.broadcasted_iota(jnp.int32, (LANES, LANES), 1) // 64
    chunk_mask = row >= col
    for g in range(n_grp):
        sl = slice(g * LANES, (g + 1) * LANES)
        vg = uv[:, a_w + g * LANES:a_w + (g + 1) * LANES]
        vn = _rms(vg, gv_ref[:, sl]).astype(BF16)
        wm = jnp.where(chunk_mask, ws_ref[g], 0.0).astype(BF16)
        bias = bst_ref[:, g:g + 1]
        for n in range(tm // LANES):
            rs = slice(n * LANES, (n + 1) * LANES)
            s = _dot(wm, vn[rs]) + bias
            ya_ref[0, rs, sl] = (uv[rs, sl] * s).astype(BF16)

    hi, mid, lo = _split3(logf)
    tri = tri_ref[...]
    lc = _dot(tri, hi) + _dot(tri, mid) + _dot(tri, lo)
    first = logf[0:1]
    tot = lc[tm - 1:tm]
    run = run_ref[...]
    q_start = (t % tiles_per_q) == 0
    a_off = jnp.where(q_start, -first, aoff_ref[...])
    c_q0 = jnp.where(q_start, run + first, cq0_ref[...])
    a_all = a_off + lc
    b_all = tot - lc
    cs_ref[0] = jnp.broadcast_to(c_q0, (8, LANES))
    ce_ref[0] = jnp.broadcast_to(run + tot, (8, LANES))
    aoff_ref[...] = a_off + tot
    cq0_ref[...] = c_q0
    run_ref[...] = run + tot

    lane = lax.broadcasted_iota(jnp.int32, (tm, LANES), 1)
    low = lane < 64
    hd = 64
    q_off = 2 * a_w
    k_off = 2 * a_w + f_w
    v_off = 2 * a_w + 2 * f_w

    def head_norm(blk, gain):
        sq = blk * blk
        s_lo = jnp.sum(jnp.where(low, sq, 0.0), axis=-1, keepdims=True)
        s_hi = jnp.sum(jnp.where(low, 0.0, sq), axis=-1, keepdims=True)
        r = jnp.where(low, lax.rsqrt(s_lo / hd + EPS), lax.rsqrt(s_hi / hd + EPS))
        return (blk * r) * gain

    def extras(col_vals, base, vals_first):
        bc = jnp.broadcast_to(col_vals, (tm, LANES))
        hi, mid, lo = _split3(bc)
        o = 0 if vals_first else 3
        e = jnp.where(lane == base + o, hi.astype(F32),
                      jnp.where(lane == base + o + 1, mid.astype(F32),
                                jnp.where(lane == base + o + 2, lo.astype(F32), 0.0)))
        ones_lo = base + (3 if vals_first else 0)
        return jnp.where((lane >= ones_lo) & (lane < ones_lo + 3), 1.0, e)

    for j in range(f_w // LANES):
        sl = slice(j * LANES, (j + 1) * LANES)
        qn = head_norm(z[:, q_off + j * LANES:q_off + (j + 1) * LANES], gq_ref[...]) * (LOG2E / math.sqrt(hd))
        kn = head_norm(z[:, k_off + j * LANES:k_off + (j + 1) * LANES], gk_ref[...])
        vv = z[:, v_off + j * LANES:v_off + (j + 1) * LANES]
        for hh in range(2):
            hidx = 2 * j + hh
            qhalf = low if hh == 0 else jnp.logical_not(low)
            base = 64 if hh == 0 else 0
            e_q = extras(a_all[:, hidx:hidx + 1], base, True)
            e_k = extras(b_all[:, hidx:hidx + 1], base, False)
            osl = slice(hidx * LANES, (hidx + 1) * LANES)
            qp_ref[0, :, osl] = jnp.where(qhalf, qn, e_q).astype(BF16)
            kp_ref[0, :, osl] = jnp.where(qhalf, kn, e_k).astype(BF16)
            vp_ref[0, :, osl] = jnp.where(qhalf, vv, 1.0).astype(BF16)


def _even_in(x, g_mix, w_main, w_f, b_f, g_v, w_s, b_s_t, g_q2, g_k2, tri):
    B, S, D = x.shape
    tm = ROW_TILE
    assert FOX_TK == tm and FOX_TQ % tm == 0
    tiles_per_q = FOX_TQ // tm
    a_w = g_v.shape[1]
    f_w = (w_main.shape[1] - 2 * a_w) // 3
    n_heads = f_w // 64
    grid = (B, S // tm)
    row3 = lambda w: pl.BlockSpec((1, tm, w), lambda b, t: (b, t, 0))
    return pl.pallas_call(
        functools.partial(_even_in_kernel, tiles_per_q=tiles_per_q),
        grid=grid,
        in_specs=[row3(D), _const_spec(g_mix.shape), _const_spec(w_main.shape), _const_spec(w_f.shape),
                  _const_spec(b_f.shape), _const_spec(g_v.shape), _const_spec(w_s.shape),
                  _const_spec(b_s_t.shape), _const_spec(g_q2.shape), _const_spec(g_k2.shape),
                  _const_spec(tri.shape)],
        out_specs=[row3(a_w), row3(n_heads * LANES), row3(n_heads * LANES), row3(n_heads * LANES),
                   pl.BlockSpec((1, 8, LANES), lambda b, t: (b, t // tiles_per_q, 0)),
                   pl.BlockSpec((1, 8, LANES), lambda b, t: (b, t, 0))],
        out_shape=[jax.ShapeDtypeStruct((B, S, a_w), BF16),
                   jax.ShapeDtypeStruct((B, S, n_heads * LANES), BF16),
                   jax.ShapeDtypeStruct((B, S, n_heads * LANES), BF16),
                   jax.ShapeDtypeStruct((B, S, n_heads * LANES), BF16),
                   jax.ShapeDtypeStruct((B, (S // FOX_TQ) * 8, LANES), F32),
                   jax.ShapeDtypeStruct((B, (S // FOX_TK) * 8, LANES), F32)],
        scratch_shapes=[pltpu.VMEM((1, LANES), F32), pltpu.VMEM((1, LANES), F32), pltpu.VMEM((1, LANES), F32)],
        compiler_params=pltpu.CompilerParams(
            dimension_semantics=("arbitrary", "arbitrary"),
            vmem_limit_bytes=_vmem_limit(48 * 1024 * 1024)),
        name="even_in",
    )(x, g_mix, w_main, w_f, b_f, g_v, w_s, b_s_t, g_q2, g_k2, tri)


def _fox_kernel(cs_ref, ce_ref, qp_ref, kp_ref, vp_ref, o_ref, s_buf, mx_buf, m_scr, acc_scr,
                *, nq, nk, n_heads):
    b = pl.program_id(0)
    hp = pl.program_id(1)
    tq, tk = FOX_TQ, FOX_TK
    sub = tq // tk
    nchunk = tk // LANES
    lane = lax.broadcasted_iota(jnp.int32, (tq, LANES), 1)
    rowi = lax.broadcasted_iota(jnp.int32, (tq, tk), 0)
    coli = lax.broadcasted_iota(jnp.int32, (tq, tk), 1)

    def stage1(i, j, row_shift):
        q0 = pl.multiple_of(i * tq, tq)
        k0 = pl.multiple_of(j * tk, tk)
        for hh in range(2):
            sl = slice(hh * LANES, (hh + 1) * LANES)
            s = _dot_nt(qp_ref[0, pl.ds(q0, tq), sl], kp_ref[0, pl.ds(k0, tk), sl])
            if row_shift is not None:
                s = jnp.where(rowi + row_shift >= coli, s, NEG_BIG)
            s_buf[hh] = s
            mx = s[:, 0:LANES]
            for c in range(1, nchunk):
                mx = jnp.maximum(mx, s[:, c * LANES:(c + 1) * LANES])
            mx_buf[hh] = jnp.broadcast_to(jnp.max(mx, axis=-1, keepdims=True), (tq, LANES))

    def stage2(i, j):
        k0 = pl.multiple_of(j * tk, tk)
        for hh in range(2):
            head = hp * 2 + hh
            sl = slice(hh * LANES, (hh + 1) * LANES)
            d = cs_ref[(b * nq + i) * n_heads + head] - ce_ref[(b * nk + j) * n_heads + head]
            m_old = m_scr[hh]
            m_new = jnp.maximum(m_old, mx_buf[hh] + d)
            alpha = jnp.exp2(m_old - m_new)
            shift = m_new - d
            p = jnp.concatenate(
                [jnp.exp2(s_buf[hh, :, c * LANES:(c + 1) * LANES] - shift).astype(BF16) for c in range(nchunk)],
                axis=-1)
            acc_scr[hh] = alpha * acc_scr[hh] + _dot(p, vp_ref[0, pl.ds(k0, tk), sl])
            m_scr[hh] = m_new

    def q_body(i, carry):
        q0 = pl.multiple_of(i * tq, tq)
        n_full = i * sub
        for hh in range(2):
            m_scr[hh] = jnp.full((tq, LANES), NEG_BIG, F32)
            acc_scr[hh] = jnp.zeros((tq, LANES), F32)
        stage1(i, 0, q0)

        def body(j, c):
            stage2(i, j)
            stage1(i, j + 1, None)
            return c

        lax.fori_loop(0, n_full - 1, body, 0)

        @pl.when(n_full > 0)
        def _():
            stage2(i, n_full - 1)
            stage1(i, n_full, 0)

        for t in range(1, sub):
            stage2(i, n_full + t - 1)
            stage1(i, n_full + t, -t * tk)
        stage2(i, n_full + sub - 1)

        outs = []
        for hh in range(2):
            a = acc_scr[hh]
            outs.append(a * (1.0 / pltpu.roll(a, 64, axis=1)))
        o_ref[0, pl.ds(q0, tq), :] = jnp.where(lane < 64, outs[0], outs[1]).astype(BF16)
        return carry

    lax.fori_loop(0, nq, q_body, 0)


def _fox(cs_flat, ce_flat, qp, kp, vp):
    B, S, HW = qp.shape
    n_heads = HW // LANES
    nq, nk = S // FOX_TQ, S // FOX_TK
    grid = (B, n_heads // 2)
    slab = pl.BlockSpec((1, S, 2 * LANES), lambda b, h: (b, 0, h))
    return pl.pallas_call(
        functools.partial(_fox_kernel, nq=nq, nk=nk, n_heads=n_heads),
        grid=grid,
        in_specs=[pl.BlockSpec(memory_space=pltpu.SMEM), pl.BlockSpec(memory_space=pltpu.SMEM),
                  slab, slab, slab],
        out_specs=pl.BlockSpec((1, S, LANES), lambda b, h: (b, 0, h)),
        out_shape=jax.ShapeDtypeStruct((B, S, n_heads * 64), BF16),
        scratch_shapes=[pltpu.VMEM((2, FOX_TQ, FOX_TK), F32), pltpu.VMEM((2, FOX_TQ, LANES), F32),
                        pltpu.VMEM((2, FOX_TQ, LANES), F32), pltpu.VMEM((2, FOX_TQ, LANES), F32)],
        compiler_params=pltpu.CompilerParams(
            dimension_semantics=("arbitrary", "arbitrary"),
            vmem_limit_bytes=_vmem_limit(56 * 1024 * 1024)),
        name="fox_attn",
    )(cs_flat, ce_flat, qp, kp, vp)


def _odd_in_kernel(x_ref, g_ref, w_ref, wp_ref, sp_ref, cw_ref, yc_ref, yd_ref, zbuf, xbuf, *, windows):
    t = pl.program_id(1)
    tm = x_ref.shape[1]
    pw = yc_ref.shape[2]
    cwid = yd_ref.shape[2]

    @pl.when(t == 0)
    def _():
        zbuf[0:POOL_HALO, :] = jnp.zeros((POOL_HALO, pw), F32)
        xbuf[0:CONV_HALO, :] = jnp.zeros((CONV_HALO, cwid), F32)

    h = _rms(x_ref[0], g_ref[...]).astype(BF16)
    z = _dot(h, w_ref[...])
    zc = z[:, :pw]
    hdn = z[:, pw:pw + cwid]
    gb = z[:, pw + cwid:pw + 2 * cwid]
    gc = z[:, pw + 2 * cwid:pw + 3 * cwid]

    zbuf[POOL_HALO:POOL_HALO + tm, :] = zc
    pos = t * tm + lax.broadcasted_iota(jnp.int32, (tm, 1), 0)
    for g, w in enumerate(windows):
        sl = slice(g * LANES, (g + 1) * LANES)
        acc = zbuf[POOL_HALO:POOL_HALO + tm, sl]
        for j in range(1, w):
            acc = acc + zbuf[POOL_HALO - j:POOL_HALO - j + tm, sl]
        inv_cnt = 1.0 / jnp.minimum(pos + 1, w).astype(F32)
        p = acc * inv_cnt - zc[:, sl]
        yc_ref[0, :, sl] = (_dot(p.astype(BF16), wp_ref[g]) * sp_ref[:, sl]).astype(BF16)
    zbuf[0:POOL_HALO, :] = zbuf[tm:tm + POOL_HALO, :]

    xg = gc * hdn
    xbuf[CONV_HALO:CONV_HALO + tm, :] = xg
    k = cw_ref.shape[0]
    conv = cw_ref[k - 1:k, :] * xg
    for j in range(1, k):
        conv = conv + cw_ref[k - 1 - j:k - j, :] * xbuf[CONV_HALO - j:CONV_HALO - j + tm, :]
    yd_ref[0] = (gb * conv).astype(BF16)
    xbuf[0:CONV_HALO, :] = xbuf[tm:tm + CONV_HALO, :]


def _odd_in(x, g_mix, w_in, w_pool, s_pool, conv_w, windows):
    B, S, D = x.shape
    tm = ROW_TILE
    pw = s_pool.shape[1]
    cwid = conv_w.shape[1]
    assert max(windows) <= POOL_HALO and conv_w.shape[0] - 1 <= CONV_HALO
    row3 = lambda w: pl.BlockSpec((1, tm, w), lambda b, t: (b, t, 0))
    return pl.pallas_call(
        functools.partial(_odd_in_kernel, windows=windows),
        grid=(B, S // tm),
        in_specs=[row3(D), _const_spec(g_mix.shape), _const_spec(w_in.shape), _const_spec(w_pool.shape),
                  _const_spec(s_pool.shape), _const_spec(conv_w.shape)],
        out_specs=[row3(pw), row3(cwid)],
        out_shape=[jax.ShapeDtypeStruct((B, S, pw), BF16), jax.ShapeDtypeStruct((B, S, cwid), BF16)],
        scratch_shapes=[pltpu.VMEM((POOL_HALO + tm, pw), F32), pltpu.VMEM((CONV_HALO + tm, cwid), F32)],
        compiler_params=pltpu.CompilerParams(
            dimension_semantics=("arbitrary", "arbitrary"),
            vmem_limit_bytes=_vmem_limit(48 * 1024 * 1024)),
        name="odd_in",
    )(x, g_mix, w_in, w_pool, s_pool, conv_w)


def _mem_kv_kernel(m_ref, g_ref, w_ref, gk_ref, k_ref, v_ref):
    xa = k_ref.shape[2]
    hm = _rms(m_ref[0], g_ref[...]).astype(BF16)
    kv = _dot(hm, w_ref[...])
    for h in range(xa // LANES):
        sl = slice(h * LANES, (h + 1) * LANES)
        k_ref[0, :, sl] = _rms(kv[:, sl], gk_ref[...]).astype(BF16)
    v_ref[0] = kv[:, xa:].astype(BF16)


def _mem_kv(mem, g_mem, w_kv, g_k):
    B, M, D = mem.shape
    xa = w_kv.shape[1] // 2
    blk = lambda w: pl.BlockSpec((1, M, w), lambda b: (b, 0, 0))
    return pl.pallas_call(
        _mem_kv_kernel,
        grid=(B,),
        in_specs=[blk(D), _const_spec(g_mem.shape), _const_spec(w_kv.shape), _const_spec(g_k.shape)],
        out_specs=[blk(xa), blk(xa)],
        out_shape=[jax.ShapeDtypeStruct((B, M, xa), BF16), jax.ShapeDtypeStruct((B, M, xa), BF16)],
        compiler_params=pltpu.CompilerParams(dimension_semantics=("arbitrary",)),
        name="mem_kv",
    )(mem, g_mem, w_kv, g_k)


def _post_kernel(x_ref, ya_ref, yb_ref, woa_ref, wob_ref, gxa_ref, wq_ref, gq_ref, k_ref, v_ref, wo_ref,
                 gff_ref, wg_ref, wu_ref, wd_ref, o_ref, hs_scr):
    xa = wq_ref.shape[1]
    dff = wg_ref.shape[1]
    x1 = x_ref[0] + _dot(ya_ref[0], woa_ref[...]) + _dot(yb_ref[0], wob_ref[...])

    hx = _rms(x1, gxa_ref[...]).astype(BF16)
    q = _dot(hx, wq_ref[...])
    inv_sqrt = 1.0 / math.sqrt(LANES)
    outs = []
    for h in range(xa // LANES):
        sl = slice(h * LANES, (h + 1) * LANES)
        qn = _rms(q[:, sl], gq_ref[...]).astype(BF16)
        s = _dot_nt(qn, k_ref[0, :, sl]) * inv_sqrt
        m = jnp.max(s, axis=-1, keepdims=True)
        p = jnp.exp(s - m)
        l = jnp.sum(p, axis=-1, keepdims=True)
        outs.append((_dot(p.astype(BF16), v_ref[0, :, sl]) * (1.0 / l)).astype(BF16))
    x2 = x1 + _dot(jnp.concatenate(outs, axis=-1), wo_ref[...])

    hf = _rms(x2, gff_ref[...]).astype(BF16)
    c0 = 0
    while c0 < dff:
        c1 = min(c0 + FF_CHUNK, dff)
        a = _dot(hf, wg_ref[:, c0:c1])
        u = _dot(hf, wu_ref[:, c0:c1])
        hs_scr[:, c0:c1] = (a * jax.nn.sigmoid(a) * u).astype(BF16)
        c0 = c1
    o_ref[0] = x2 + _dot(hs_scr[...], wd_ref[...])


def _post(x, ya, yb, wo_a, wo_b, g_xa, w_q, g_q, k_mem, v_mem, w_o, g_ffn, w_gate, w_up, w_down):
    B, S, D = x.shape
    tm = ROW_TILE
    M = k_mem.shape[1]
    xa = w_q.shape[1]
    dff = w_gate.shape[1]
    row3 = lambda w: pl.BlockSpec((1, tm, w), lambda b, t: (b, t, 0))
    memb = pl.BlockSpec((1, M, xa), lambda b, t: (b, 0, 0))
    cs = lambda a: _const_spec(a.shape, single=True)
    return pl.pallas_call(
        _post_kernel,
        grid=(B, S // tm),
        in_specs=[row3(D), row3(ya.shape[2]), row3(yb.shape[2]), cs(wo_a), cs(wo_b), cs(g_xa), cs(w_q),
                  cs(g_q), memb, memb, cs(w_o), cs(g_ffn), cs(w_gate), cs(w_up), cs(w_down)],
        out_specs=row3(D),
        out_shape=jax.ShapeDtypeStruct((B, S, D), F32),
        scratch_shapes=[pltpu.VMEM((tm, dff), BF16)],
        compiler_params=pltpu.CompilerParams(
            dimension_semantics=("arbitrary", "arbitrary"),
            vmem_limit_bytes=_vmem_limit(58 * 1024 * 1024)),
        name="post",
    )(x, ya, yb, wo_a, wo_b, g_xa, w_q, g_q, k_mem, v_mem, w_o, g_ffn, w_gate, w_up, w_down)


def kernel(x, mem, g_mix, g_xa, g_mem, xa_wq, xa_wkv, xa_wo, xa_gq, xa_gk, g_ffn, w_gate, w_up, w_down,
           e_w_in, e_b_f, e_g_v, e_w_s, e_b_s, e_g_qn, e_g_kn, e_w_out,
           o_w_in, o_w_pool, o_s_pool, o_conv_w, o_w_out):
    depth = g_mix.shape[0]
    B, S, D = x.shape
    row = lambda a: a.reshape(1, -1)
    tri = (lax.broadcasted_iota(jnp.int32, (ROW_TILE, ROW_TILE), 0)
           >= lax.broadcasted_iota(jnp.int32, (ROW_TILE, ROW_TILE), 1)).astype(BF16)
    pool_windows = (2, 4, 8, 16)[:o_w_pool.shape[1]]

    for layer in range(depth):
        i = layer // 2
        if layer % 2 == 0:
            n_heads = e_b_f.shape[1]
            a_w = e_g_v.shape[1]
            n_main = e_w_in.shape[2] - n_heads
            w_main = e_w_in[i, :, :n_main].astype(BF16)
            w_f = jnp.pad(e_w_in[i, :, n_main:], ((0, 0), (0, LANES - n_heads))).astype(BF16)
            b_f = jnp.pad(e_b_f[i], (0, LANES - n_heads)).reshape(1, LANES)
            g_q2 = jnp.tile(e_g_qn[i], 2).reshape(1, LANES)
            g_k2 = jnp.tile(e_g_kn[i], 2).reshape(1, LANES)
            ya, qp, kp, vp, cs, ce = _even_in(x, row(g_mix[layer]), w_main, w_f, b_f, row(e_g_v[i]), e_w_s[i],
                                              e_b_s[i].T, g_q2, g_k2, tri)
            yb = _fox(cs[:, ::8, :n_heads].reshape(-1), ce[:, ::8, :n_heads].reshape(-1), qp, kp, vp)
            w_out = e_w_out[i].astype(BF16)
        else:
            ya, yb = _odd_in(x, row(g_mix[layer]), o_w_in[i].astype(BF16), o_w_pool[i].astype(BF16),
                             row(o_s_pool[i]), o_conv_w[i], pool_windows)
            a_w = ya.shape[2]
            w_out = o_w_out[i].astype(BF16)
        k_mem, v_mem = _mem_kv(mem, row(g_mem[layer]), xa_wkv[layer].astype(BF16), row(xa_gk[layer]))
        x = _post(x, ya, yb, w_out[:a_w], w_out[a_w:], row(g_xa[layer]), xa_wq[layer].astype(BF16),
                  row(xa_gq[layer]), k_mem, v_mem, xa_wo[layer].astype(BF16), row(g_ffn[layer]),
                  w_gate[layer].astype(BF16), w_up[layer].astype(BF16), w_down[layer].astype(BF16))
    return x
```

```python
import functools
import math

import jax
import jax.numpy as jnp
import numpy as np
from jax import lax
from jax.experimental import pallas as pl
from jax.experimental.pallas import tpu as pltpu

F32 = jnp.float32
BF16 = jnp.bfloat16
EPS = 1e-6
LOG2E = 1.4426950408889634
NEG_BIG = -1e30

LANES = 128
V7X_VMEM_BYTES = 64 * 1024 * 1024

ROW_TILE = 512
FOX_TQ = 2048
FOX_TK = ROW_TILE
POOL_HALO = 16
CONV_HALO = 8
FF_CHUNK = 256
POST_SPLIT = 1


def _vmem_limit(nbytes):
    return int(min(nbytes, V7X_VMEM_BYTES - 4 * 1024 * 1024))


def _rms(x, g):
    ms = jnp.mean(x * x, axis=-1, keepdims=True)
    return (x * lax.rsqrt(ms + EPS)) * g


def _dot(a, b):
    return jnp.dot(a, b, preferred_element_type=F32)


def _dot_nt(a, b):
    return lax.dot_general(a, b, (((1,), (1,)), ((), ())), preferred_element_type=F32)


def _split3(x):
    hi = x.astype(BF16)
    r1 = x - hi.astype(F32)
    mid = r1.astype(BF16)
    lo = (r1 - mid.astype(F32)).astype(BF16)
    return hi, mid, lo


def _const_spec(shape, single=False):
    nd = len(shape)
    kw = {}
    if single:
        kw["pipeline_mode"] = pl.Buffered(1)
    return pl.BlockSpec(shape, lambda *_: (0,) * nd, **kw)


def _even_in_kernel(x_ref, g_ref, w_ref, wf_ref, bf_ref, gv_ref, ws_ref, bst_ref, gq_ref, gk_ref,
                    tri_ref, route_ref, ya_ref, qp_ref, kp_ref, vp_ref, cs_ref, ce_ref, run_ref, aoff_ref, cq0_ref,
                    *, tiles_per_q):
    t = pl.program_id(1)
    tm = x_ref.shape[1]
    a_w = ya_ref.shape[2]
    f_w = vp_ref.shape[2] // 2
    n_grp = a_w // LANES

    @pl.when(t == 0)
    def _():
        run_ref[...] = jnp.zeros_like(run_ref)

    h = _rms(x_ref[0], g_ref[...]).astype(BF16)
    z = _dot(h, w_ref[...])
    fl = _dot(h, wf_ref[...]) + bf_ref[...]
    logf = -(jnp.maximum(-fl, 0.0) + jnp.log1p(jnp.exp(-jnp.abs(fl)))) * LOG2E

    uv = jax.nn.gelu(z[:, :2 * a_w])
    row = lax.broadcasted_iota(jnp.int32, (LANES, LANES), 0) // 64
    col = lax.broadcasted_iota(jnp.int32, (LANES, LANES), 1) // 64
    chunk_mask = row >= col
    for g in range(n_grp):
        sl = slice(g * LANES, (g + 1) * LANES)
        vg = uv[:, a_w + g * LANES:a_w + (g + 1) * LANES]
        vn = _rms(vg, gv_ref[:, sl]).astype(BF16)
        wm = jnp.where(chunk_mask, ws_ref[g], 0.0).astype(BF16)
        bias = bst_ref[:, g:g + 1]
        for n in range(tm // LANES):
            rs = slice(n * LANES, (n + 1) * LANES)
            s = _dot(wm, vn[rs]) + bias
            ya_ref[0, rs, sl] = (uv[rs, sl] * s).astype(BF16)

    hi, mid, lo = _split3(logf)
    tri = tri_ref[...]
    lc = _dot(tri, hi) + _dot(tri, mid) + _dot(tri, lo)
    first = logf[0:1]
    tot = lc[tm - 1:tm]
    run = run_ref[...]
    q_start = (t % tiles_per_q) == 0
    a_off = jnp.where(q_start, -first, aoff_ref[...])
    c_q0 = jnp.where(q_start, run + first, cq0_ref[...])
    a_all = a_off + lc
    b_all = tot - lc
    cs_ref[0] = jnp.broadcast_to(c_q0, (8, LANES))
    ce_ref[0] = jnp.broadcast_to(run + tot, (8, LANES))
    aoff_ref[...] = a_off + tot
    cq0_ref[...] = c_q0
    run_ref[...] = run + tot

    lane = lax.broadcasted_iota(jnp.int32, (tm, LANES), 1)
    low = lane < 64
    hd = 64
    q_off = 2 * a_w
    k_off = 2 * a_w + f_w
    v_off = 2 * a_w + 2 * f_w

    def head_norm(blk, gain):
        sq = blk * blk
        s_lo = jnp.sum(jnp.where(low, sq, 0.0), axis=-1, keepdims=True)
        s_hi = jnp.sum(jnp.where(low, 0.0, sq), axis=-1, keepdims=True)
        r = jnp.where(low, lax.rsqrt(s_lo / hd + EPS), lax.rsqrt(s_hi / hd + EPS))
        return (blk * r) * gain

    n_heads = f_w // hd
    ab3 = jnp.concatenate(list(_split3(a_all)) + list(_split3(b_all)), axis=-1)
    routed = _dot(ab3, route_ref[...])
    l64 = lane & 63
    x_q = jnp.where((l64 >= 3 * n_heads) & (l64 < 6 * n_heads), 1.0, routed[:, :LANES])
    y_all = routed[:, LANES:]

    def k_extras(h):
        return jnp.where((l64 >= 3 * h) & (l64 < 3 * h + 3), 1.0,
                         jnp.where((l64 >= 3 * (n_heads + h)) & (l64 < 3 * (n_heads + h) + 3), y_all, 0.0))

    for j in range(f_w // LANES):
        sl = slice(j * LANES, (j + 1) * LANES)
        qn = head_norm(z[:, q_off + j * LANES:q_off + (j + 1) * LANES], gq_ref[...]) * (LOG2E / math.sqrt(hd))
        kn = head_norm(z[:, k_off + j * LANES:k_off + (j + 1) * LANES], gk_ref[...])
        vv = z[:, v_off + j * LANES:v_off + (j + 1) * LANES]
        for hh in range(2):
            hidx = 2 * j + hh
            qhalf = low if hh == 0 else jnp.logical_not(low)
            osl = slice(hidx * LANES, (hidx + 1) * LANES)
            qp_ref[0, :, osl] = jnp.where(qhalf, qn, x_q).astype(BF16)
            kp_ref[0, :, osl] = jnp.where(qhalf, kn, k_extras(hidx)).astype(BF16)
            vp_ref[0, :, osl] = jnp.where(qhalf, vv, 1.0).astype(BF16)


def _route_matrix(n_heads):
    assert 6 * n_heads <= 64
    r = np.zeros((6 * LANES, 2 * LANES), np.float32)
    for h in range(n_heads):
        for x in range(3):
            for half in (0, 64):
                r[x * LANES + h, half + 3 * h + x] = 1.0
                r[(3 + x) * LANES + h, LANES + half + 3 * (n_heads + h) + x] = 1.0
    return jnp.asarray(r, BF16)


def _even_in(x, g_mix, w_main, w_f, b_f, g_v, w_s, b_s_t, g_q2, g_k2, tri):
    B, S, D = x.shape
    tm = ROW_TILE
    assert FOX_TK == tm and FOX_TQ % tm == 0
    tiles_per_q = FOX_TQ // tm
    a_w = g_v.shape[1]
    f_w = (w_main.shape[1] - 2 * a_w) // 3
    n_heads = f_w // 64
    route = _route_matrix(n_heads)
    grid = (B, S // tm)
    row3 = lambda w: pl.BlockSpec((1, tm, w), lambda b, t: (b, t, 0))
    return pl.pallas_call(
        functools.partial(_even_in_kernel, tiles_per_q=tiles_per_q),
        grid=grid,
        in_specs=[row3(D), _const_spec(g_mix.shape), _const_spec(w_main.shape), _const_spec(w_f.shape),
                  _const_spec(b_f.shape), _const_spec(g_v.shape), _const_spec(w_s.shape),
                  _const_spec(b_s_t.shape), _const_spec(g_q2.shape), _const_spec(g_k2.shape),
                  _const_spec(tri.shape), _const_spec(route.shape)],
        out_specs=[row3(a_w), row3(n_heads * LANES), row3(n_heads * LANES), row3(n_heads * LANES),
                   pl.BlockSpec((1, 8, LANES), lambda b, t: (b, t // tiles_per_q, 0)),
                   pl.BlockSpec((1, 8, LANES), lambda b, t: (b, t, 0))],
        out_shape=[jax.ShapeDtypeStruct((B, S, a_w), BF16),
                   jax.ShapeDtypeStruct((B, S, n_heads * LANES), BF16),
                   jax.ShapeDtypeStruct((B, S, n_heads * LANES), BF16),
                   jax.ShapeDtypeStruct((B, S, n_heads * LANES), BF16),
                   jax.ShapeDtypeStruct((B, (S // FOX_TQ) * 8, LANES), F32),
                   jax.ShapeDtypeStruct((B, (S // FOX_TK) * 8, LANES), F32)],
        scratch_shapes=[pltpu.VMEM((1, LANES), F32), pltpu.VMEM((1, LANES), F32), pltpu.VMEM((1, LANES), F32)],
        compiler_params=pltpu.CompilerParams(
            dimension_semantics=("arbitrary", "arbitrary"),
            vmem_limit_bytes=_vmem_limit(48 * 1024 * 1024)),
        name="even_in",
    )(x, g_mix, w_main, w_f, b_f, g_v, w_s, b_s_t, g_q2, g_k2, tri, route)


def _fox_kernel(cs_ref, ce_ref, qp_ref, kp_ref, vp_ref, o_ref, s_buf, mx_buf, m_scr, acc_scr,
                *, nq, nk, n_heads):
    b = pl.program_id(0)
    hp = pl.program_id(1)
    tq, tk = FOX_TQ, FOX_TK
    sub = tq // tk
    nchunk = tk // LANES
    lane = lax.broadcasted_iota(jnp.int32, (tq, LANES), 1)

    def stage1(i, j, row_shift, row_lo=0):
        q0 = pl.multiple_of(i * tq, tq)
        k0 = pl.multiple_of(j * tk, tk)
        nr = tq - row_lo
        for hh in range(2):
            sl = slice(hh * LANES, (hh + 1) * LANES)
            s = _dot_nt(qp_ref[0, pl.ds(q0 + row_lo, nr), sl], kp_ref[0, pl.ds(k0, tk), sl])
            if row_shift is not None:
                rowi = lax.broadcasted_iota(jnp.int32, (nr, tk), 0)
                coli = lax.broadcasted_iota(jnp.int32, (nr, tk), 1)
                s = jnp.where(rowi + row_shift >= coli, s, NEG_BIG)
            s_buf[hh, row_lo:tq] = s
            mx = s[:, 0:LANES]
            for c in range(1, nchunk):
                mx = jnp.maximum(mx, s[:, c * LANES:(c + 1) * LANES])
            mx_buf[hh, row_lo:tq] = jnp.broadcast_to(jnp.max(mx, axis=-1, keepdims=True), (nr, LANES))

    def stage2(i, j, row_lo=0):
        k0 = pl.multiple_of(j * tk, tk)
        rs = slice(row_lo, tq)
        for hh in range(2):
            head = hp * 2 + hh
            sl = slice(hh * LANES, (hh + 1) * LANES)
            d = cs_ref[(b * nq + i) * n_heads + head] - ce_ref[(b * nk + j) * n_heads + head]
            m_old = m_scr[hh, rs]
            m_new = jnp.maximum(m_old, mx_buf[hh, rs] + d)
            alpha = jnp.exp2(m_old - m_new)
            shift = m_new - d
            p = jnp.concatenate(
                [jnp.exp2(s_buf[hh, rs, c * LANES:(c + 1) * LANES] - shift).astype(BF16) for c in range(nchunk)],
                axis=-1)
            acc_scr[hh, rs] = alpha * acc_scr[hh, rs] + _dot(p, vp_ref[0, pl.ds(k0, tk), sl])
            m_scr[hh, rs] = m_new

    def q_body(i, carry):
        q0 = pl.multiple_of(i * tq, tq)
        n_full = i * sub
        for hh in range(2):
            m_scr[hh] = jnp.full((tq, LANES), NEG_BIG, F32)
            acc_scr[hh] = jnp.zeros((tq, LANES), F32)
        stage1(i, 0, q0)

        def body(j, c):
            stage2(i, j)
            stage1(i, j + 1, None)
            return c

        lax.fori_loop(0, n_full - 1, body, 0)

        @pl.when(n_full > 0)
        def _():
            stage2(i, n_full - 1)
            stage1(i, n_full, 0)

        for t in range(1, sub):
            stage2(i, n_full + t - 1, (t - 1) * tk)
            stage1(i, n_full + t, 0, t * tk)
        stage2(i, n_full + sub - 1, (sub - 1) * tk)

        outs = []
        for hh in range(2):
            a = acc_scr[hh]
            outs.append(a * (1.0 / pltpu.roll(a, 64, axis=1)))
        o_ref[0, pl.ds(q0, tq), :] = jnp.where(lane < 64, outs[0], outs[1]).astype(BF16)
        return carry

    lax.fori_loop(0, nq, q_body, 0)


def _fox(cs_flat, ce_flat, qp, kp, vp):
    B, S, HW = qp.shape
    n_heads = HW // LANES
    nq, nk = S // FOX_TQ, S // FOX_TK
    grid = (B, n_heads // 2)
    slab = pl.BlockSpec((1, S, 2 * LANES), lambda b, h: (b, 0, h))
    return pl.pallas_call(
        functools.partial(_fox_kernel, nq=nq, nk=nk, n_heads=n_heads),
        grid=grid,
        in_specs=[pl.BlockSpec(memory_space=pltpu.SMEM), pl.BlockSpec(memory_space=pltpu.SMEM),
                  slab, slab, slab],
        out_specs=pl.BlockSpec((1, S, LANES), lambda b, h: (b, 0, h)),
        out_shape=jax.ShapeDtypeStruct((B, S, n_heads * 64), BF16),
        scratch_shapes=[pltpu.VMEM((2, FOX_TQ, FOX_TK), F32), pltpu.VMEM((2, FOX_TQ, LANES), F32),
                        pltpu.VMEM((2, FOX_TQ, LANES), F32), pltpu.VMEM((2, FOX_TQ, LANES), F32)],
        compiler_params=pltpu.CompilerParams(
            dimension_semantics=("arbitrary", "arbitrary"),
            vmem_limit_bytes=_vmem_limit(56 * 1024 * 1024)),
        name="fox_attn",
    )(cs_flat, ce_flat, qp, kp, vp)


def _odd_in_kernel(x_ref, g_ref, w_ref, wp_ref, sp_ref, cw_ref, yc_ref, yd_ref, zbuf, xbuf, *, windows):
    t = pl.program_id(1)
    tm = x_ref.shape[1]
    pw = yc_ref.shape[2]
    cwid = yd_ref.shape[2]

    @pl.when(t == 0)
    def _():
        zbuf[0:POOL_HALO, :] = jnp.zeros((POOL_HALO, pw), F32)
        xbuf[0:CONV_HALO, :] = jnp.zeros((CONV_HALO, cwid), F32)

    h = _rms(x_ref[0], g_ref[...]).astype(BF16)
    z = _dot(h, w_ref[...])
    zc = z[:, :pw]
    hdn = z[:, pw:pw + cwid]
    gb = z[:, pw + cwid:pw + 2 * cwid]
    gc = z[:, pw + 2 * cwid:pw + 3 * cwid]

    zbuf[POOL_HALO:POOL_HALO + tm, :] = zc
    pos = t * tm + lax.broadcasted_iota(jnp.int32, (tm, 1), 0)
    for g, w in enumerate(windows):
        sl = slice(g * LANES, (g + 1) * LANES)
        acc = zbuf[POOL_HALO:POOL_HALO + tm, sl]
        for j in range(1, w):
            acc = acc + zbuf[POOL_HALO - j:POOL_HALO - j + tm, sl]
        inv_cnt = 1.0 / jnp.minimum(pos + 1, w).astype(F32)
        p = acc * inv_cnt - zc[:, sl]
        yc_ref[0, :, sl] = (_dot(p.astype(BF16), wp_ref[g]) * sp_ref[:, sl]).astype(BF16)
    zbuf[0:POOL_HALO, :] = zbuf[tm:tm + POOL_HALO, :]

    xg = gc * hdn
    xbuf[CONV_HALO:CONV_HALO + tm, :] = xg
    k = cw_ref.shape[0]
    conv = cw_ref[k - 1:k, :] * xg
    for j in range(1, k):
        conv = conv + cw_ref[k - 1 - j:k - j, :] * xbuf[CONV_HALO - j:CONV_HALO - j + tm, :]
    yd_ref[0] = (gb * conv).astype(BF16)
    xbuf[0:CONV_HALO, :] = xbuf[tm:tm + CONV_HALO, :]


def _odd_in(x, g_mix, w_in, w_pool, s_pool, conv_w, windows):
    B, S, D = x.shape
    tm = ROW_TILE
    pw = s_pool.shape[1]
    cwid = conv_w.shape[1]
    assert max(windows) <= POOL_HALO and conv_w.shape[0] - 1 <= CONV_HALO
    row3 = lambda w: pl.BlockSpec((1, tm, w), lambda b, t: (b, t, 0))
    return pl.pallas_call(
        functools.partial(_odd_in_kernel, windows=windows),
        grid=(B, S // tm),
        in_specs=[row3(D), _const_spec(g_mix.shape), _const_spec(w_in.shape), _const_spec(w_pool.shape),
                  _const_spec(s_pool.shape), _const_spec(conv_w.shape)],
        out_specs=[row3(pw), row3(cwid)],
        out_shape=[jax.ShapeDtypeStruct((B, S, pw), BF16), jax.ShapeDtypeStruct((B, S, cwid), BF16)],
        scratch_shapes=[pltpu.VMEM((POOL_HALO + tm, pw), F32), pltpu.VMEM((CONV_HALO + tm, cwid), F32)],
        compiler_params=pltpu.CompilerParams(
            dimension_semantics=("arbitrary", "arbitrary"),
            vmem_limit_bytes=_vmem_limit(48 * 1024 * 1024)),
        name="odd_in",
    )(x, g_mix, w_in, w_pool, s_pool, conv_w)


def _mem_kv_kernel(m_ref, g_ref, w_ref, gk_ref, k_ref, v_ref):
    xa = k_ref.shape[2]
    hm = _rms(m_ref[0], g_ref[...]).astype(BF16)
    kv = _dot(hm, w_ref[...])
    for h in range(xa // LANES):
        sl = slice(h * LANES, (h + 1) * LANES)
        k_ref[0, :, sl] = _rms(kv[:, sl], gk_ref[...]).astype(BF16)
    v_ref[0] = kv[:, xa:].astype(BF16)


def _mem_kv(mem, g_mem, w_kv, g_k):
    B, M, D = mem.shape
    xa = w_kv.shape[1] // 2
    blk = lambda w: pl.BlockSpec((1, M, w), lambda b: (b, 0, 0))
    return pl.pallas_call(
        _mem_kv_kernel,
        grid=(B,),
        in_specs=[blk(D), _const_spec(g_mem.shape), _const_spec(w_kv.shape), _const_spec(g_k.shape)],
        out_specs=[blk(xa), blk(xa)],
        out_shape=[jax.ShapeDtypeStruct((B, M, xa), BF16), jax.ShapeDtypeStruct((B, M, xa), BF16)],
        compiler_params=pltpu.CompilerParams(dimension_semantics=("arbitrary",)),
        name="mem_kv",
    )(mem, g_mem, w_kv, g_k)


def _post_kernel(x_ref, ya_ref, yb_ref, woa_ref, wob_ref, gxa_ref, wq_ref, gq_ref, k_ref, v_ref, wo_ref,
                 gff_ref, wg_ref, wu_ref, wd_ref, o_ref, hs_scr):
    xa = wq_ref.shape[1]
    dff = wg_ref.shape[1]
    tm = x_ref.shape[1]
    inv_sqrt = 1.0 / math.sqrt(LANES)

    for r0 in range(0, tm, tm // POST_SPLIT):
        rs = slice(r0, r0 + tm // POST_SPLIT)
        x1 = x_ref[0, rs] + _dot(ya_ref[0, rs], woa_ref[...]) + _dot(yb_ref[0, rs], wob_ref[...])

        hx = _rms(x1, gxa_ref[...]).astype(BF16)
        q = _dot(hx, wq_ref[...])
        outs = []
        for h in range(xa // LANES):
            sl = slice(h * LANES, (h + 1) * LANES)
            qn = _rms(q[:, sl], gq_ref[...]).astype(BF16)
            s = _dot_nt(qn, k_ref[0, :, sl]) * inv_sqrt
            m = jnp.max(s, axis=-1, keepdims=True)
            p = jnp.exp(s - m)
            l = jnp.sum(p, axis=-1, keepdims=True)
            outs.append((_dot(p.astype(BF16), v_ref[0, :, sl]) * (1.0 / l)).astype(BF16))
        x2 = x1 + _dot(jnp.concatenate(outs, axis=-1), wo_ref[...])

        hf = _rms(x2, gff_ref[...]).astype(BF16)
        c0 = 0
        while c0 < dff:
            c1 = min(c0 + FF_CHUNK, dff)
            a = _dot(hf, wg_ref[:, c0:c1])
            u = _dot(hf, wu_ref[:, c0:c1])
            hs_scr[rs, c0:c1] = (a * jax.nn.sigmoid(a) * u).astype(BF16)
            c0 = c1
        o_ref[0, rs] = x2 + _dot(hs_scr[rs, :], wd_ref[...])


def _post(x, ya, yb, wo_a, wo_b, g_xa, w_q, g_q, k_mem, v_mem, w_o, g_ffn, w_gate, w_up, w_down):
    B, S, D = x.shape
    tm = ROW_TILE
    M = k_mem.shape[1]
    xa = w_q.shape[1]
    dff = w_gate.shape[1]
    row3 = lambda w: pl.BlockSpec((1, tm, w), lambda b, t: (b, t, 0))
    memb = pl.BlockSpec((1, M, xa), lambda b, t: (b, 0, 0))
    cs = lambda a: _const_spec(a.shape, single=True)
    return pl.pallas_call(
        _post_kernel,
        grid=(B, S // tm),
        in_specs=[row3(D), row3(ya.shape[2]), row3(yb.shape[2]), cs(wo_a), cs(wo_b), cs(g_xa), cs(w_q),
                  cs(g_q), memb, memb, cs(w_o), cs(g_ffn), cs(w_gate), cs(w_up), cs(w_down)],
        out_specs=row3(D),
        out_shape=jax.ShapeDtypeStruct((B, S, D), F32),
        scratch_shapes=[pltpu.VMEM((tm, dff), BF16)],
        compiler_params=pltpu.CompilerParams(
            dimension_semantics=("arbitrary", "arbitrary"),
            vmem_limit_bytes=_vmem_limit(58 * 1024 * 1024)),
        name="post",
    )(x, ya, yb, wo_a, wo_b, g_xa, w_q, g_q, k_mem, v_mem, w_o, g_ffn, w_gate, w_up, w_down)


def kernel(x, mem, g_mix, g_xa, g_mem, xa_wq, xa_wkv, xa_wo, xa_gq, xa_gk, g_ffn, w_gate, w_up, w_down,
           e_w_in, e_b_f, e_g_v, e_w_s, e_b_s, e_g_qn, e_g_kn, e_w_out,
           o_w_in, o_w_pool, o_s_pool, o_conv_w, o_w_out):
    depth = g_mix.shape[0]
    B, S, D = x.shape
    row = lambda a: a.reshape(1, -1)
    tri = (lax.broadcasted_iota(jnp.int32, (ROW_TILE, ROW_TILE), 0)
           >= lax.broadcasted_iota(jnp.int32, (ROW_TILE, ROW_TILE), 1)).astype(BF16)
    pool_windows = (2, 4, 8, 16)[:o_w_pool.shape[1]]

    for layer in range(depth):
        i = layer // 2
        if layer % 2 == 0:
            n_heads = e_b_f.shape[1]
            a_w = e_g_v.shape[1]
            n_main = e_w_in.shape[2] - n_heads
            w_main = e_w_in[i, :, :n_main].astype(BF16)
            w_f = jnp.pad(e_w_in[i, :, n_main:], ((0, 0), (0, LANES - n_heads))).astype(BF16)
            b_f = jnp.pad(e_b_f[i], (0, LANES - n_heads)).reshape(1, LANES)
            g_q2 = jnp.tile(e_g_qn[i], 2).reshape(1, LANES)
            g_k2 = jnp.tile(e_g_kn[i], 2).reshape(1, LANES)
            ya, qp, kp, vp, cs, ce = _even_in(x, row(g_mix[layer]), w_main, w_f, b_f, row(e_g_v[i]), e_w_s[i],
                                              e_b_s[i].T, g_q2, g_k2, tri)
            yb = _fox(cs[:, ::8, :n_heads].reshape(-1), ce[:, ::8, :n_heads].reshape(-1), qp, kp, vp)
            w_out = e_w_out[i].astype(BF16)
        else:
            ya, yb = _odd_in(x, row(g_mix[layer]), o_w_in[i].astype(BF16), o_w_pool[i].astype(BF16),
                             row(o_s_pool[i]), o_conv_w[i], pool_windows)
            a_w = ya.shape[2]
            w_out = o_w_out[i].astype(BF16)
        k_mem, v_mem = _mem_kv(mem, row(g_mem[layer]), xa_wkv[layer].astype(BF16), row(xa_gk[layer]))
        x = _post(x, ya, yb, w_out[:a_w], w_out[a_w:], row(g_xa[layer]), xa_wq[layer].astype(BF16),
                  row(xa_gq[layer]), k_mem, v_mem, xa_wo[layer].astype(BF16), row(g_ffn[layer]),
                  w_gate[layer].astype(BF16), w_up[layer].astype(BF16), w_down[layer].astype(BF16))
    return x
```

```python
import functools
import math

import jax
import jax.numpy as jnp
import numpy as np
from jax import lax
from jax.experimental import pallas as pl
from jax.experimental.pallas import tpu as pltpu

F32 = jnp.float32
BF16 = jnp.bfloat16
EPS = 1e-6
LOG2E = 1.4426950408889634
NEG_BIG = -1e30

LANES = 128
MXU_TILE = 256
V7X_VMEM_BYTES = 64 * 1024 * 1024

ROW_TILE = 512
FOX_TQ = 2048
FOX_TK = ROW_TILE
POOL_HALO = 16
CONV_HALO = 8
FF_CHUNK = 256
POST_SPLIT = 1


def _vmem_limit(nbytes):
    return int(min(nbytes, V7X_VMEM_BYTES - 4 * 1024 * 1024))


def _rms(x, g):
    ms = jnp.mean(x * x, axis=-1, keepdims=True)
    return (x * lax.rsqrt(ms + EPS)) * g


def _dot(a, b):
    return jnp.dot(a, b, preferred_element_type=F32)


def _dot_nt(a, b):
    return lax.dot_general(a, b, (((1,), (1,)), ((), ())), preferred_element_type=F32)


def _split3(x):
    hi = x.astype(BF16)
    r1 = x - hi.astype(F32)
    mid = r1.astype(BF16)
    lo = (r1 - mid.astype(F32)).astype(BF16)
    return hi, mid, lo


def _const_spec(shape, single=False):
    nd = len(shape)
    kw = {}
    if single:
        kw["pipeline_mode"] = pl.Buffered(1)
    return pl.BlockSpec(shape, lambda *_: (0,) * nd, **kw)


def _even_in_kernel(x_ref, g_ref, w_ref, wvt_ref, wf_ref, bf_ref, gv_ref, ws_ref, bst_ref, gq_ref, gk_ref,
                    tri_ref, route_ref, ya_ref, qp_ref, kp_ref, vt_ref, cs_ref, ce_ref, run_ref, aoff_ref, cq0_ref,
                    *, tiles_per_q):
    t = pl.program_id(1)
    tm = x_ref.shape[1]
    a_w = ya_ref.shape[2]
    f_w = qp_ref.shape[2] // 2
    n_grp = a_w // LANES

    @pl.when(t == 0)
    def _():
        run_ref[...] = jnp.zeros_like(run_ref)

    h = _rms(x_ref[0], g_ref[...]).astype(BF16)
    z = _dot(h, w_ref[...])
    zvt = _dot_nt(wvt_ref[...], h)
    fl = _dot(h, wf_ref[...]) + bf_ref[...]
    logf = -(jnp.maximum(-fl, 0.0) + jnp.log1p(jnp.exp(-jnp.abs(fl)))) * LOG2E

    uv = jax.nn.gelu(z[:, :2 * a_w])
    row = lax.broadcasted_iota(jnp.int32, (LANES, LANES), 0) // 64
    col = lax.broadcasted_iota(jnp.int32, (LANES, LANES), 1) // 64
    chunk_mask = row >= col
    for g in range(n_grp):
        sl = slice(g * LANES, (g + 1) * LANES)
        vg = uv[:, a_w + g * LANES:a_w + (g + 1) * LANES]
        vn = _rms(vg, gv_ref[:, sl]).astype(BF16)
        wm = jnp.where(chunk_mask, ws_ref[g], 0.0).astype(BF16)
        bias = bst_ref[:, g:g + 1]
        for n in range(tm // LANES):
            rs = slice(n * LANES, (n + 1) * LANES)
            s = _dot(wm, vn[rs]) + bias
            ya_ref[0, rs, sl] = (uv[rs, sl] * s).astype(BF16)

    hi, mid, lo = _split3(logf)
    tri = tri_ref[...]
    lc = _dot(tri, hi) + _dot(tri, mid) + _dot(tri, lo)
    first = logf[0:1]
    tot = lc[tm - 1:tm]
    run = run_ref[...]
    q_start = (t % tiles_per_q) == 0
    a_off = jnp.where(q_start, -first, aoff_ref[...])
    c_q0 = jnp.where(q_start, run + first, cq0_ref[...])
    a_all = a_off + lc
    b_all = tot - lc
    cs_ref[0] = jnp.broadcast_to(c_q0, (8, LANES))
    ce_ref[0] = jnp.broadcast_to(run + tot, (8, LANES))
    aoff_ref[...] = a_off + tot
    cq0_ref[...] = c_q0
    run_ref[...] = run + tot

    lane = lax.broadcasted_iota(jnp.int32, (tm, LANES), 1)
    low = lane < 64
    hd = 64
    q_off = 2 * a_w
    k_off = 2 * a_w + f_w

    def head_norm(blk, gain):
        sq = blk * blk
        s_lo = jnp.sum(jnp.where(low, sq, 0.0), axis=-1, keepdims=True)
        s_hi = jnp.sum(jnp.where(low, 0.0, sq), axis=-1, keepdims=True)
        r = jnp.where(low, lax.rsqrt(s_lo / hd + EPS), lax.rsqrt(s_hi / hd + EPS))
        return (blk * r) * gain

    n_heads = f_w // hd
    ab3 = jnp.concatenate(list(_split3(a_all)) + list(_split3(b_all)), axis=-1)
    routed = _dot(ab3, route_ref[...])
    l64 = lane & 63
    x_q = jnp.where((l64 >= 3 * n_heads) & (l64 < 6 * n_heads), 1.0, routed[:, :LANES])
    y_all = routed[:, LANES:]

    def k_extras(h):
        return jnp.where((l64 >= 3 * h) & (l64 < 3 * h + 3), 1.0,
                         jnp.where((l64 >= 3 * (n_heads + h)) & (l64 < 3 * (n_heads + h) + 3), y_all, 0.0))

    for j in range(f_w // LANES):
        sl = slice(j * LANES, (j + 1) * LANES)
        qn = head_norm(z[:, q_off + j * LANES:q_off + (j + 1) * LANES], gq_ref[...]) * (LOG2E / math.sqrt(hd))
        kn = head_norm(z[:, k_off + j * LANES:k_off + (j + 1) * LANES], gk_ref[...])
        for hh in range(2):
            hidx = 2 * j + hh
            qhalf = low if hh == 0 else jnp.logical_not(low)
            osl = slice(hidx * LANES, (hidx + 1) * LANES)
            qp_ref[0, :, osl] = jnp.where(qhalf, qn, x_q).astype(BF16)
            kp_ref[0, :, osl] = jnp.where(qhalf, kn, k_extras(hidx)).astype(BF16)

    for hidx in range(n_heads):
        vt_ref[0, hidx, 0:hd, :] = zvt[hidx * hd:(hidx + 1) * hd].astype(BF16)
        vt_ref[0, hidx, hd:2 * hd, :] = jnp.ones((hd, tm), BF16)


def _route_matrix(n_heads):
    assert 6 * n_heads <= 64
    r = np.zeros((6 * LANES, 2 * LANES), np.float32)
    for h in range(n_heads):
        for x in range(3):
            for half in (0, 64):
                r[x * LANES + h, half + 3 * h + x] = 1.0
                r[(3 + x) * LANES + h, LANES + half + 3 * (n_heads + h) + x] = 1.0
    return jnp.asarray(r, BF16)


def _even_in(x, g_mix, w_main, w_vt, w_f, b_f, g_v, w_s, b_s_t, g_q2, g_k2, tri):
    B, S, D = x.shape
    tm = ROW_TILE
    assert FOX_TK == tm and FOX_TQ % tm == 0
    tiles_per_q = FOX_TQ // tm
    a_w = g_v.shape[1]
    f_w = w_vt.shape[0]
    n_heads = f_w // 64
    route = _route_matrix(n_heads)
    grid = (B, S // tm)
    row3 = lambda w: pl.BlockSpec((1, tm, w), lambda b, t: (b, t, 0))
    return pl.pallas_call(
        functools.partial(_even_in_kernel, tiles_per_q=tiles_per_q),
        grid=grid,
        in_specs=[row3(D), _const_spec(g_mix.shape), _const_spec(w_main.shape), _const_spec(w_vt.shape),
                  _const_spec(w_f.shape), _const_spec(b_f.shape), _const_spec(g_v.shape), _const_spec(w_s.shape),
                  _const_spec(b_s_t.shape), _const_spec(g_q2.shape), _const_spec(g_k2.shape),
                  _const_spec(tri.shape), _const_spec(route.shape)],
        out_specs=[row3(a_w), row3(n_heads * LANES), row3(n_heads * LANES),
                   pl.BlockSpec((1, n_heads, LANES, tm), lambda b, t: (b, 0, 0, t)),
                   pl.BlockSpec((1, 8, LANES), lambda b, t: (b, t // tiles_per_q, 0)),
                   pl.BlockSpec((1, 8, LANES), lambda b, t: (b, t, 0))],
        out_shape=[jax.ShapeDtypeStruct((B, S, a_w), BF16),
                   jax.ShapeDtypeStruct((B, S, n_heads * LANES), BF16),
                   jax.ShapeDtypeStruct((B, S, n_heads * LANES), BF16),
                   jax.ShapeDtypeStruct((B, n_heads, LANES, S), BF16),
                   jax.ShapeDtypeStruct((B, (S // FOX_TQ) * 8, LANES), F32),
                   jax.ShapeDtypeStruct((B, (S // FOX_TK) * 8, LANES), F32)],
        scratch_shapes=[pltpu.VMEM((1, LANES), F32), pltpu.VMEM((1, LANES), F32), pltpu.VMEM((1, LANES), F32)],
        compiler_params=pltpu.CompilerParams(
            dimension_semantics=("arbitrary", "arbitrary"),
            vmem_limit_bytes=_vmem_limit(48 * 1024 * 1024)),
        name="even_in",
    )(x, g_mix, w_main, w_vt, w_f, b_f, g_v, w_s, b_s_t, g_q2, g_k2, tri, route)


def _fox_kernel(cs_ref, ce_ref, qp_ref, kp_ref, vt_ref, o_ref, s_buf, mx_buf, p_buf, al_buf, m_scr, acc_scr,
                *, nq, nk, n_heads):
    b = pl.program_id(0)
    hp = pl.program_id(1)
    tq, tk = FOX_TQ, FOX_TK
    sub = tq // tk
    T = MXU_TILE
    nct = tq // T
    nkt = tk // T
    zslab = jnp.zeros((T, LANES), BF16)

    def qk(i, j, masked, col_lo=0):
        q0 = pl.multiple_of(i * tq, tq)
        k0 = pl.multiple_of(j * tk, tk)
        for c in range(col_lo // T, nct):
            mxu = c % 2
            pltpu.matmul_push_rhs(qp_ref[0, pl.ds(q0 + c * T, T), :], staging_register=0, mxu_index=mxu,
                                  transpose=True)
            first = True
            for hh in range(2):
                for kt in range(nkt):
                    ks = kp_ref[0, pl.ds(k0 + kt * T, T), hh * LANES:(hh + 1) * LANES]
                    lhs = jnp.concatenate([ks, zslab] if hh == 0 else [zslab, ks], axis=1)
                    a_qk = ((hh * nkt + kt) % 2) * (T // 4)
                    pltpu.matmul_acc_lhs(a_qk, lhs, mxu, load_staged_rhs=0 if first else None)
                    first = False
                    st = pltpu.matmul_pop(a_qk, (T, T), F32, mxu)
                    if masked:
                        ri = lax.broadcasted_iota(jnp.int32, (T, T), 0) + kt * T
                        ci = lax.broadcasted_iota(jnp.int32, (T, T), 1) + (c * T - col_lo)
                        st = jnp.where(ci >= ri, st, NEG_BIG)
                    s_buf[hh, kt * T:(kt + 1) * T, c * T:(c + 1) * T] = st
                    cm = jnp.max(st, axis=0, keepdims=True)
                    cmax = cm if kt == 0 else jnp.maximum(cmax, cm)
                mx_buf[hh, :, c * T:(c + 1) * T] = jnp.broadcast_to(cmax, (8, T))

    def ex(i, j, col_lo=0):
        for hh in range(2):
            head = hp * 2 + hh
            d = cs_ref[(b * nq + i) * n_heads + head] - ce_ref[(b * nk + j) * n_heads + head]
            for c in range(col_lo // T, nct):
                cs_ = slice(c * T, (c + 1) * T)
                m_old = m_scr[hh, :, cs_]
                m_new = jnp.maximum(m_old, mx_buf[hh, :, cs_] + d)
                al_buf[hh, :, cs_] = jnp.exp2(m_old - m_new)
                m_scr[hh, :, cs_] = m_new
                shift = jnp.broadcast_to((m_new - d)[0:1], (T, T))
                for kt in range(nkt):
                    rs = slice(kt * T, (kt + 1) * T)
                    p_buf[hh, rs, cs_] = jnp.exp2(s_buf[hh, rs, cs_] - shift).astype(BF16)

    def pv(i, j, col_lo=0):
        k0 = pl.multiple_of(j * tk, tk)
        n = 0
        for hh in range(2):
            for c in range(col_lo // T, nct):
                mxu = c % 2
                cs_ = slice(c * T, (c + 1) * T)
                a_pv = 2 * (T // 4) + (LANES // 4) * ((n // 2) % 4)
                n += 1
                for kt in range(nkt):
                    pltpu.matmul_push_rhs(p_buf[hh, kt * T:(kt + 1) * T, cs_], staging_register=1, mxu_index=mxu)
                    pltpu.matmul_acc_lhs(a_pv, vt_ref[0, hh, :, pl.ds(k0 + kt * T, T)], mxu, load_staged_rhs=1)
                out = pltpu.matmul_pop(a_pv, (LANES, T), F32, mxu)
                acc_scr[hh, :, cs_] = (jnp.broadcast_to(al_buf[hh, 0:1, cs_], (LANES, T)) * acc_scr[hh, :, cs_]
                                       + out)

    def tail(i, n_full, has_full_blocks):
        for t in range(sub + 2):
            for stage, off in ((pv, t - 2), (ex, t - 1)):
                if off >= 0:
                    stage(i, n_full + off, off * tk)
                elif has_full_blocks:
                    stage(i, n_full + off)
            if t < sub:
                qk(i, n_full + t, True, t * tk)

    def q_body(i, carry):
        q0 = pl.multiple_of(i * tq, tq)
        n_full = i * sub
        for hh in range(2):
            m_scr[hh] = jnp.full((8, tq), NEG_BIG, F32)
            acc_scr[hh] = jnp.zeros((LANES, tq), F32)

        @pl.when(i == 0)
        def _():
            tail(i, 0, False)

        @pl.when(i > 0)
        def _():
            assert sub >= 2
            qk(i, 0, False)
            ex(i, 0)
            qk(i, 1, False)

            def body(s, c):
                pv(i, s - 2)
                ex(i, s - 1)
                qk(i, s, False)
                return c

            lax.fori_loop(2, n_full, body, 0)
            tail(i, n_full, True)

        hd = LANES // 2
        tops = []
        for hh in range(2):
            a = acc_scr[hh]
            tops.append(a[0:hd] * (1.0 / a[hd:2 * hd]))
        o_ref[0, pl.ds(q0, tq), :] = jnp.concatenate(tops, axis=0).T.astype(BF16)
        return carry

    lax.fori_loop(0, nq, q_body, 0)


def _fox(cs_flat, ce_flat, qp, kp, vt):
    B, S, HW = qp.shape
    n_heads = HW // LANES
    nq, nk = S // FOX_TQ, S // FOX_TK
    assert FOX_TQ % MXU_TILE == 0 and FOX_TK % MXU_TILE == 0 and 2 * LANES == MXU_TILE
    grid = (B, n_heads // 2)
    slab = pl.BlockSpec((1, S, 2 * LANES), lambda b, h: (b, 0, h))
    stat = pltpu.VMEM((2, 8, FOX_TQ), F32)
    return pl.pallas_call(
        functools.partial(_fox_kernel, nq=nq, nk=nk, n_heads=n_heads),
        grid=grid,
        in_specs=[pl.BlockSpec(memory_space=pltpu.SMEM), pl.BlockSpec(memory_space=pltpu.SMEM),
                  slab, slab, pl.BlockSpec((1, 2, LANES, S), lambda b, h: (b, h, 0, 0))],
        out_specs=pl.BlockSpec((1, S, LANES), lambda b, h: (b, 0, h)),
        out_shape=jax.ShapeDtypeStruct((B, S, n_heads * 64), BF16),
        scratch_shapes=[pltpu.VMEM((2, FOX_TK, FOX_TQ), F32), stat,
                        pltpu.VMEM((2, FOX_TK, FOX_TQ), BF16), stat,
                        stat, pltpu.VMEM((2, LANES, FOX_TQ), F32)],
        compiler_params=pltpu.CompilerParams(
            dimension_semantics=("arbitrary", "arbitrary"),
            vmem_limit_bytes=_vmem_limit(56 * 1024 * 1024)),
        name="fox_attn",
    )(cs_flat, ce_flat, qp, kp, vt)


def _odd_in_kernel(x_ref, g_ref, w_ref, wp_ref, sp_ref, cw_ref, yc_ref, yd_ref, zbuf, xbuf, *, windows):
    t = pl.program_id(1)
    tm = x_ref.shape[1]
    pw = yc_ref.shape[2]
    cwid = yd_ref.shape[2]

    @pl.when(t == 0)
    def _():
        zbuf[0:POOL_HALO, :] = jnp.zeros((POOL_HALO, pw), F32)
        xbuf[0:CONV_HALO, :] = jnp.zeros((CONV_HALO, cwid), F32)

    h = _rms(x_ref[0], g_ref[...]).astype(BF16)
    z = _dot(h, w_ref[...])
    zc = z[:, :pw]
    hdn = z[:, pw:pw + cwid]
    gb = z[:, pw + cwid:pw + 2 * cwid]
    gc = z[:, pw + 2 * cwid:pw + 3 * cwid]

    zbuf[POOL_HALO:POOL_HALO + tm, :] = zc
    pos = t * tm + lax.broadcasted_iota(jnp.int32, (tm, 1), 0)
    for g, w in enumerate(windows):
        sl = slice(g * LANES, (g + 1) * LANES)
        acc = zbuf[POOL_HALO:POOL_HALO + tm, sl]
        for j in range(1, w):
            acc = acc + zbuf[POOL_HALO - j:POOL_HALO - j + tm, sl]
        inv_cnt = 1.0 / jnp.minimum(pos + 1, w).astype(F32)
        p = acc * inv_cnt - zc[:, sl]
        yc_ref[0, :, sl] = (_dot(p.astype(BF16), wp_ref[g]) * sp_ref[:, sl]).astype(BF16)
    zbuf[0:POOL_HALO, :] = zbuf[tm:tm + POOL_HALO, :]

    xg = gc * hdn
    xbuf[CONV_HALO:CONV_HALO + tm, :] = xg
    k = cw_ref.shape[0]
    conv = cw_ref[k - 1:k, :] * xg
    for j in range(1, k):
        conv = conv + cw_ref[k - 1 - j:k - j, :] * xbuf[CONV_HALO - j:CONV_HALO - j + tm, :]
    yd_ref[0] = (gb * conv).astype(BF16)
    xbuf[0:CONV_HALO, :] = xbuf[tm:tm + CONV_HALO, :]


def _odd_in(x, g_mix, w_in, w_pool, s_pool, conv_w, windows):
    B, S, D = x.shape
    tm = ROW_TILE
    pw = s_pool.shape[1]
    cwid = conv_w.shape[1]
    assert max(windows) <= POOL_HALO and conv_w.shape[0] - 1 <= CONV_HALO
    row3 = lambda w: pl.BlockSpec((1, tm, w), lambda b, t: (b, t, 0))
    return pl.pallas_call(
        functools.partial(_odd_in_kernel, windows=windows),
        grid=(B, S // tm),
        in_specs=[row3(D), _const_spec(g_mix.shape), _const_spec(w_in.shape), _const_spec(w_pool.shape),
                  _const_spec(s_pool.shape), _const_spec(conv_w.shape)],
        out_specs=[row3(pw), row3(cwid)],
        out_shape=[jax.ShapeDtypeStruct((B, S, pw), BF16), jax.ShapeDtypeStruct((B, S, cwid), BF16)],
        scratch_shapes=[pltpu.VMEM((POOL_HALO + tm, pw), F32), pltpu.VMEM((CONV_HALO + tm, cwid), F32)],
        compiler_params=pltpu.CompilerParams(
            dimension_semantics=("arbitrary", "arbitrary"),
            vmem_limit_bytes=_vmem_limit(48 * 1024 * 1024)),
        name="odd_in",
    )(x, g_mix, w_in, w_pool, s_pool, conv_w)


def _mem_kv_kernel(m_ref, g_ref, w_ref, gk_ref, k_ref, v_ref):
    xa = k_ref.shape[2]
    hm = _rms(m_ref[0], g_ref[...]).astype(BF16)
    kv = _dot(hm, w_ref[...])
    for h in range(xa // LANES):
        sl = slice(h * LANES, (h + 1) * LANES)
        k_ref[0, :, sl] = _rms(kv[:, sl], gk_ref[...]).astype(BF16)
    v_ref[0] = kv[:, xa:].astype(BF16)


def _mem_kv(mem, g_mem, w_kv, g_k):
    B, M, D = mem.shape
    xa = w_kv.shape[1] // 2
    blk = lambda w: pl.BlockSpec((1, M, w), lambda b: (b, 0, 0))
    return pl.pallas_call(
        _mem_kv_kernel,
        grid=(B,),
        in_specs=[blk(D), _const_spec(g_mem.shape), _const_spec(w_kv.shape), _const_spec(g_k.shape)],
        out_specs=[blk(xa), blk(xa)],
        out_shape=[jax.ShapeDtypeStruct((B, M, xa), BF16), jax.ShapeDtypeStruct((B, M, xa), BF16)],
        compiler_params=pltpu.CompilerParams(dimension_semantics=("arbitrary",)),
        name="mem_kv",
    )(mem, g_mem, w_kv, g_k)


def _post_kernel(x_ref, ya_ref, yb_ref, woa_ref, wob_ref, gxa_ref, wq_ref, gq_ref, k_ref, v_ref, wo_ref,
                 gff_ref, wg_ref, wu_ref, wd_ref, o_ref, hs_scr):
    xa = wq_ref.shape[1]
    dff = wg_ref.shape[1]
    tm = x_ref.shape[1]
    inv_sqrt = 1.0 / math.sqrt(LANES)

    for r0 in range(0, tm, tm // POST_SPLIT):
        rs = slice(r0, r0 + tm // POST_SPLIT)
        x1 = x_ref[0, rs] + _dot(ya_ref[0, rs], woa_ref[...]) + _dot(yb_ref[0, rs], wob_ref[...])

        hx = _rms(x1, gxa_ref[...]).astype(BF16)
        q = _dot(hx, wq_ref[...])
        outs = []
        for h in range(xa // LANES):
            sl = slice(h * LANES, (h + 1) * LANES)
            qn = _rms(q[:, sl], gq_ref[...]).astype(BF16)
            s = _dot_nt(qn, k_ref[0, :, sl]) * inv_sqrt
            m = jnp.max(s, axis=-1, keepdims=True)
            p = jnp.exp(s - m)
            l = jnp.sum(p, axis=-1, keepdims=True)
            outs.append((_dot(p.astype(BF16), v_ref[0, :, sl]) * (1.0 / l)).astype(BF16))
        x2 = x1 + _dot(jnp.concatenate(outs, axis=-1), wo_ref[...])

        hf = _rms(x2, gff_ref[...]).astype(BF16)
        c0 = 0
        while c0 < dff:
            c1 = min(c0 + FF_CHUNK, dff)
            a = _dot(hf, wg_ref[:, c0:c1])
            u = _dot(hf, wu_ref[:, c0:c1])
            hs_scr[rs, c0:c1] = (a * jax.nn.sigmoid(a) * u).astype(BF16)
            c0 = c1
        o_ref[0, rs] = x2 + _dot(hs_scr[rs, :], wd_ref[...])


def _post(x, ya, yb, wo_a, wo_b, g_xa, w_q, g_q, k_mem, v_mem, w_o, g_ffn, w_gate, w_up, w_down):
    B, S, D = x.shape
    tm = ROW_TILE
    M = k_mem.shape[1]
    xa = w_q.shape[1]
    dff = w_gate.shape[1]
    row3 = lambda w: pl.BlockSpec((1, tm, w), lambda b, t: (b, t, 0))
    memb = pl.BlockSpec((1, M, xa), lambda b, t: (b, 0, 0))
    cs = lambda a: _const_spec(a.shape, single=True)
    return pl.pallas_call(
        _post_kernel,
        grid=(B, S // tm),
        in_specs=[row3(D), row3(ya.shape[2]), row3(yb.shape[2]), cs(wo_a), cs(wo_b), cs(g_xa), cs(w_q),
                  cs(g_q), memb, memb, cs(w_o), cs(g_ffn), cs(w_gate), cs(w_up), cs(w_down)],
        out_specs=row3(D),
        out_shape=jax.ShapeDtypeStruct((B, S, D), F32),
        scratch_shapes=[pltpu.VMEM((tm, dff), BF16)],
        compiler_params=pltpu.CompilerParams(
            dimension_semantics=("arbitrary", "arbitrary"),
            vmem_limit_bytes=_vmem_limit(58 * 1024 * 1024)),
        name="post",
    )(x, ya, yb, wo_a, wo_b, g_xa, w_q, g_q, k_mem, v_mem, w_o, g_ffn, w_gate, w_up, w_down)


def kernel(x, mem, g_mix, g_xa, g_mem, xa_wq, xa_wkv, xa_wo, xa_gq, xa_gk, g_ffn, w_gate, w_up, w_down,
           e_w_in, e_b_f, e_g_v, e_w_s, e_b_s, e_g_qn, e_g_kn, e_w_out,
           o_w_in, o_w_pool, o_s_pool, o_conv_w, o_w_out):
    depth = g_mix.shape[0]
    B, S, D = x.shape
    row = lambda a: a.reshape(1, -1)
    tri = (lax.broadcasted_iota(jnp.int32, (ROW_TILE, ROW_TILE), 0)
           >= lax.broadcasted_iota(jnp.int32, (ROW_TILE, ROW_TILE), 1)).astype(BF16)
    pool_windows = (2, 4, 8, 16)[:o_w_pool.shape[1]]

    for layer in range(depth):
        i = layer // 2
        if layer % 2 == 0:
            n_heads = e_b_f.shape[1]
            a_w = e_g_v.shape[1]
            f_w = n_heads * e_g_qn.shape[1]
            n_uvqk = 2 * a_w + 2 * f_w
            w_main = e_w_in[i, :, :n_uvqk].astype(BF16)
            w_vt = e_w_in[i, :, n_uvqk:n_uvqk + f_w].T.astype(BF16)
            w_f = jnp.pad(e_w_in[i, :, n_uvqk + f_w:], ((0, 0), (0, LANES - n_heads))).astype(BF16)
            b_f = jnp.pad(e_b_f[i], (0, LANES - n_heads)).reshape(1, LANES)
            g_q2 = jnp.tile(e_g_qn[i], 2).reshape(1, LANES)
            g_k2 = jnp.tile(e_g_kn[i], 2).reshape(1, LANES)
            ya, qp, kp, vt, cs, ce = _even_in(x, row(g_mix[layer]), w_main, w_vt, w_f, b_f, row(e_g_v[i]),
                                              e_w_s[i], e_b_s[i].T, g_q2, g_k2, tri)
            yb = _fox(cs[:, ::8, :n_heads].reshape(-1), ce[:, ::8, :n_heads].reshape(-1), qp, kp, vt)
            w_out = e_w_out[i].astype(BF16)
        else:
            ya, yb = _odd_in(x, row(g_mix[layer]), o_w_in[i].astype(BF16), o_w_pool[i].astype(BF16),
                             row(o_s_pool[i]), o_conv_w[i], pool_windows)
            a_w = ya.shape[2]
            w_out = o_w_out[i].astype(BF16)
        k_mem, v_mem = _mem_kv(mem, row(g_mem[layer]), xa_wkv[layer].astype(BF16), row(xa_gk[layer]))
        x = _post(x, ya, yb, w_out[:a_w], w_out[a_w:], row(g_xa[layer]), xa_wq[layer].astype(BF16),
                  row(xa_gq[layer]), k_mem, v_mem, xa_wo[layer].astype(BF16), row(g_ffn[layer]),
                  w_gate[layer].astype(BF16), w_up[layer].astype(BF16), w_down[layer].astype(BF16))
    return x
```

```python
import functools
import math

import jax
import jax.numpy as jnp
import numpy as np
from jax import lax
from jax.experimental import pallas as pl
from jax.experimental.pallas import tpu as pltpu

F32 = jnp.float32
BF16 = jnp.bfloat16
EPS = 1e-6
LOG2E = 1.4426950408889634
NEG_BIG = -1e30

LANES = 128
MXU_TILE = 256
V7X_VMEM_BYTES = 64 * 1024 * 1024

ROW_TILE = 512
POST_TILE = 1024
FOX_TQ = 2048
FOX_TK = ROW_TILE
POOL_HALO = 16
CONV_HALO = 8
FF_CHUNK = 256


def _vmem_limit(nbytes):
    return int(min(nbytes, V7X_VMEM_BYTES - 4 * 1024 * 1024))


def _rms(x, g):
    ms = jnp.mean(x * x, axis=-1, keepdims=True)
    return (x * lax.rsqrt(ms + EPS)) * g


def _dot(a, b):
    return jnp.dot(a, b, preferred_element_type=F32)


def _dot_nt(a, b):
    return lax.dot_general(a, b, (((1,), (1,)), ((), ())), preferred_element_type=F32)


def _split3(x):
    hi = x.astype(BF16)
    r1 = x - hi.astype(F32)
    mid = r1.astype(BF16)
    lo = (r1 - mid.astype(F32)).astype(BF16)
    return hi, mid, lo


def _const_spec(shape, single=False):
    nd = len(shape)
    kw = {}
    if single:
        kw["pipeline_mode"] = pl.Buffered(1)
    return pl.BlockSpec(shape, lambda *_: (0,) * nd, **kw)


def _even_in_kernel(x_ref, g_ref, w_ref, wvt_ref, wf_ref, bf_ref, gv_ref, ws_ref, bst_ref, gq_ref, gk_ref,
                    tri_ref, route_ref, ya_ref, qp_ref, kp_ref, vt_ref, cs_ref, ce_ref, run_ref, aoff_ref, cq0_ref,
                    *, tiles_per_q):
    t = pl.program_id(1)
    tm = x_ref.shape[1]
    a_w = ya_ref.shape[2]
    f_w = qp_ref.shape[2] // 2
    n_grp = a_w // LANES

    @pl.when(t == 0)
    def _():
        run_ref[...] = jnp.zeros_like(run_ref)

    h = _rms(x_ref[0], g_ref[...]).astype(BF16)
    z = _dot(h, w_ref[...])
    zvt = _dot_nt(wvt_ref[...], h)
    fl = _dot(h, wf_ref[...]) + bf_ref[...]
    logf = -(jnp.maximum(-fl, 0.0) + jnp.log1p(jnp.exp(-jnp.abs(fl)))) * LOG2E

    uv = jax.nn.gelu(z[:, :2 * a_w])
    row = lax.broadcasted_iota(jnp.int32, (LANES, LANES), 0) // 64
    col = lax.broadcasted_iota(jnp.int32, (LANES, LANES), 1) // 64
    chunk_mask = row >= col
    for g in range(n_grp):
        sl = slice(g * LANES, (g + 1) * LANES)
        vg = uv[:, a_w + g * LANES:a_w + (g + 1) * LANES]
        vn = _rms(vg, gv_ref[:, sl]).astype(BF16)
        wm = jnp.where(chunk_mask, ws_ref[g], 0.0).astype(BF16)
        bias = bst_ref[:, g:g + 1]
        for n in range(tm // LANES):
            rs = slice(n * LANES, (n + 1) * LANES)
            s = _dot(wm, vn[rs]) + bias
            ya_ref[0, rs, sl] = (uv[rs, sl] * s).astype(BF16)

    hi, mid, lo = _split3(logf)
    tri = tri_ref[...]
    lc = _dot(tri, hi) + _dot(tri, mid) + _dot(tri, lo)
    first = logf[0:1]
    tot = lc[tm - 1:tm]
    run = run_ref[...]
    q_start = (t % tiles_per_q) == 0
    a_off = jnp.where(q_start, -first, aoff_ref[...])
    c_q0 = jnp.where(q_start, run + first, cq0_ref[...])
    a_all = a_off + lc
    b_all = tot - lc
    cs_ref[0] = jnp.broadcast_to(c_q0, (8, LANES))
    ce_ref[0] = jnp.broadcast_to(run + tot, (8, LANES))
    aoff_ref[...] = a_off + tot
    cq0_ref[...] = c_q0
    run_ref[...] = run + tot

    lane = lax.broadcasted_iota(jnp.int32, (tm, LANES), 1)
    low = lane < 64
    hd = 64
    q_off = 2 * a_w
    k_off = 2 * a_w + f_w

    def head_norm(blk, gain):
        sq = blk * blk
        s_lo = jnp.sum(jnp.where(low, sq, 0.0), axis=-1, keepdims=True)
        s_hi = jnp.sum(jnp.where(low, 0.0, sq), axis=-1, keepdims=True)
        r = jnp.where(low, lax.rsqrt(s_lo / hd + EPS), lax.rsqrt(s_hi / hd + EPS))
        return (blk * r) * gain

    n_heads = f_w // hd
    ab3 = jnp.concatenate(list(_split3(a_all)) + list(_split3(b_all)), axis=-1)
    routed = _dot(ab3, route_ref[...])
    l64 = lane & 63
    x_q = jnp.where((l64 >= 3 * n_heads) & (l64 < 6 * n_heads), 1.0, routed[:, :LANES])
    y_all = routed[:, LANES:]

    def k_extras(h):
        return jnp.where((l64 >= 3 * h) & (l64 < 3 * h + 3), 1.0,
                         jnp.where((l64 >= 3 * (n_heads + h)) & (l64 < 3 * (n_heads + h) + 3), y_all, 0.0))

    for j in range(f_w // LANES):
        sl = slice(j * LANES, (j + 1) * LANES)
        qn = head_norm(z[:, q_off + j * LANES:q_off + (j + 1) * LANES], gq_ref[...]) * (LOG2E / math.sqrt(hd))
        kn = head_norm(z[:, k_off + j * LANES:k_off + (j + 1) * LANES], gk_ref[...])
        for hh in range(2):
            hidx = 2 * j + hh
            qhalf = low if hh == 0 else jnp.logical_not(low)
            osl = slice(hidx * LANES, (hidx + 1) * LANES)
            qp_ref[0, :, osl] = jnp.where(qhalf, qn, x_q).astype(BF16)
            kp_ref[0, :, osl] = jnp.where(qhalf, kn, k_extras(hidx)).astype(BF16)

    for hidx in range(n_heads):
        vt_ref[0, hidx, 0:hd, :] = zvt[hidx * hd:(hidx + 1) * hd].astype(BF16)
        vt_ref[0, hidx, hd:2 * hd, :] = jnp.ones((hd, tm), BF16)


def _route_matrix(n_heads):
    assert 6 * n_heads <= 64
    r = np.zeros((6 * LANES, 2 * LANES), np.float32)
    for h in range(n_heads):
        for x in range(3):
            for half in (0, 64):
                r[x * LANES + h, half + 3 * h + x] = 1.0
                r[(3 + x) * LANES + h, LANES + half + 3 * (n_heads + h) + x] = 1.0
    return jnp.asarray(r, BF16)


def _even_in(x, g_mix, w_main, w_vt, w_f, b_f, g_v, w_s, b_s_t, g_q2, g_k2, tri):
    B, S, D = x.shape
    tm = ROW_TILE
    assert FOX_TK == tm and FOX_TQ % tm == 0
    tiles_per_q = FOX_TQ // tm
    a_w = g_v.shape[1]
    f_w = w_vt.shape[0]
    n_heads = f_w // 64
    route = _route_matrix(n_heads)
    grid = (B, S // tm)
    row3 = lambda w: pl.BlockSpec((1, tm, w), lambda b, t: (b, t, 0))
    return pl.pallas_call(
        functools.partial(_even_in_kernel, tiles_per_q=tiles_per_q),
        grid=grid,
        in_specs=[row3(D), _const_spec(g_mix.shape), _const_spec(w_main.shape), _const_spec(w_vt.shape),
                  _const_spec(w_f.shape), _const_spec(b_f.shape), _const_spec(g_v.shape), _const_spec(w_s.shape),
                  _const_spec(b_s_t.shape), _const_spec(g_q2.shape), _const_spec(g_k2.shape),
                  _const_spec(tri.shape), _const_spec(route.shape)],
        out_specs=[row3(a_w), row3(n_heads * LANES), row3(n_heads * LANES),
                   pl.BlockSpec((1, n_heads, LANES, tm), lambda b, t: (b, 0, 0, t)),
                   pl.BlockSpec((1, 8, LANES), lambda b, t: (b, t // tiles_per_q, 0)),
                   pl.BlockSpec((1, 8, LANES), lambda b, t: (b, t, 0))],
        out_shape=[jax.ShapeDtypeStruct((B, S, a_w), BF16),
                   jax.ShapeDtypeStruct((B, S, n_heads * LANES), BF16),
                   jax.ShapeDtypeStruct((B, S, n_heads * LANES), BF16),
                   jax.ShapeDtypeStruct((B, n_heads, LANES, S), BF16),
                   jax.ShapeDtypeStruct((B, (S // FOX_TQ) * 8, LANES), F32),
                   jax.ShapeDtypeStruct((B, (S // FOX_TK) * 8, LANES), F32)],
        scratch_shapes=[pltpu.VMEM((1, LANES), F32), pltpu.VMEM((1, LANES), F32), pltpu.VMEM((1, LANES), F32)],
        compiler_params=pltpu.CompilerParams(
            dimension_semantics=("arbitrary", "arbitrary"),
            vmem_limit_bytes=_vmem_limit(48 * 1024 * 1024)),
        name="even_in",
    )(x, g_mix, w_main, w_vt, w_f, b_f, g_v, w_s, b_s_t, g_q2, g_k2, tri, route)


def _fox_kernel(cs_ref, ce_ref, qp_ref, kp_ref, vt_ref, o_ref, s_buf, mx_buf, p_buf, al_buf, m_scr, acc_scr,
                *, nq, nk, n_heads):
    b = pl.program_id(0)
    hp = pl.program_id(1)
    tq, tk = FOX_TQ, FOX_TK
    sub = tq // tk
    T = MXU_TILE
    nct = tq // T
    nkt = tk // T
    zslab = jnp.zeros((T, LANES), BF16)

    def qk(i, j, masked, col_lo=0):
        q0 = pl.multiple_of(i * tq, tq)
        k0 = pl.multiple_of(j * tk, tk)
        for c in range(col_lo // T, nct):
            mxu = c % 2
            pltpu.matmul_push_rhs(qp_ref[0, pl.ds(q0 + c * T, T), :], staging_register=0, mxu_index=mxu,
                                  transpose=True)
            first = True
            for hh in range(2):
                for kt in range(nkt):
                    ks = kp_ref[0, pl.ds(k0 + kt * T, T), hh * LANES:(hh + 1) * LANES]
                    lhs = jnp.concatenate([ks, zslab] if hh == 0 else [zslab, ks], axis=1)
                    a_qk = ((hh * nkt + kt) % 2) * (T // 4)
                    pltpu.matmul_acc_lhs(a_qk, lhs, mxu, load_staged_rhs=0 if first else None)
                    first = False
                    st = pltpu.matmul_pop(a_qk, (T, T), F32, mxu)
                    if masked:
                        ri = lax.broadcasted_iota(jnp.int32, (T, T), 0) + kt * T
                        ci = lax.broadcasted_iota(jnp.int32, (T, T), 1) + (c * T - col_lo)
                        st = jnp.where(ci >= ri, st, NEG_BIG)
                    s_buf[hh, kt * T:(kt + 1) * T, c * T:(c + 1) * T] = st
                    cm = jnp.max(st, axis=0, keepdims=True)
                    cmax = cm if kt == 0 else jnp.maximum(cmax, cm)
                mx_buf[hh, :, c * T:(c + 1) * T] = jnp.broadcast_to(cmax, (8, T))

    def ex(i, j, col_lo=0):
        for hh in range(2):
            head = hp * 2 + hh
            d = cs_ref[(b * nq + i) * n_heads + head] - ce_ref[(b * nk + j) * n_heads + head]
            for c in range(col_lo // T, nct):
                cs_ = slice(c * T, (c + 1) * T)
                m_old = m_scr[hh, :, cs_]
                m_new = jnp.maximum(m_old, mx_buf[hh, :, cs_] + d)
                al_buf[hh, :, cs_] = jnp.exp2(m_old - m_new)
                m_scr[hh, :, cs_] = m_new
                shift = jnp.broadcast_to((m_new - d)[0:1], (T, T))
                for kt in range(nkt):
                    rs = slice(kt * T, (kt + 1) * T)
                    p_buf[hh, rs, cs_] = jnp.exp2(s_buf[hh, rs, cs_] - shift).astype(BF16)

    def pv(i, j, col_lo=0):
        k0 = pl.multiple_of(j * tk, tk)
        n = 0
        for hh in range(2):
            for c in range(col_lo // T, nct):
                mxu = c % 2
                cs_ = slice(c * T, (c + 1) * T)
                a_pv = 2 * (T // 4) + (LANES // 4) * ((n // 2) % 4)
                n += 1
                for kt in range(nkt):
                    pltpu.matmul_push_rhs(p_buf[hh, kt * T:(kt + 1) * T, cs_], staging_register=1, mxu_index=mxu)
                    pltpu.matmul_acc_lhs(a_pv, vt_ref[0, hh, :, pl.ds(k0 + kt * T, T)], mxu, load_staged_rhs=1)
                out = pltpu.matmul_pop(a_pv, (LANES, T), F32, mxu)
                acc_scr[hh, :, cs_] = (jnp.broadcast_to(al_buf[hh, 0:1, cs_], (LANES, T)) * acc_scr[hh, :, cs_]
                                       + out)

    def tail(i, n_full, has_full_blocks):
        for t in range(sub + 2):
            for stage, off in ((pv, t - 2), (ex, t - 1)):
                if off >= 0:
                    stage(i, n_full + off, off * tk)
                elif has_full_blocks:
                    stage(i, n_full + off)
            if t < sub:
                qk(i, n_full + t, True, t * tk)

    def q_body(i, carry):
        q0 = pl.multiple_of(i * tq, tq)
        n_full = i * sub
        for hh in range(2):
            m_scr[hh] = jnp.full((8, tq), NEG_BIG, F32)
            acc_scr[hh] = jnp.zeros((LANES, tq), F32)

        @pl.when(i == 0)
        def _():
            tail(i, 0, False)

        @pl.when(i > 0)
        def _():
            assert sub >= 2
            qk(i, 0, False)
            ex(i, 0)
            qk(i, 1, False)

            def body(s, c):
                pv(i, s - 2)
                ex(i, s - 1)
                qk(i, s, False)
                return c

            lax.fori_loop(2, n_full, body, 0)
            tail(i, n_full, True)

        hd = LANES // 2
        tops = []
        for hh in range(2):
            a = acc_scr[hh]
            tops.append(a[0:hd] * (1.0 / a[hd:2 * hd]))
        o_ref[0, pl.ds(q0, tq), :] = jnp.concatenate(tops, axis=0).T.astype(BF16)
        return carry

    lax.fori_loop(0, nq, q_body, 0)


def _fox(cs_flat, ce_flat, qp, kp, vt):
    B, S, HW = qp.shape
    n_heads = HW // LANES
    nq, nk = S // FOX_TQ, S // FOX_TK
    assert FOX_TQ % MXU_TILE == 0 and FOX_TK % MXU_TILE == 0 and 2 * LANES == MXU_TILE
    grid = (B, n_heads // 2)
    slab = pl.BlockSpec((1, S, 2 * LANES), lambda b, h: (b, 0, h))
    stat = pltpu.VMEM((2, 8, FOX_TQ), F32)
    return pl.pallas_call(
        functools.partial(_fox_kernel, nq=nq, nk=nk, n_heads=n_heads),
        grid=grid,
        in_specs=[pl.BlockSpec(memory_space=pltpu.SMEM), pl.BlockSpec(memory_space=pltpu.SMEM),
                  slab, slab, pl.BlockSpec((1, 2, LANES, S), lambda b, h: (b, h, 0, 0))],
        out_specs=pl.BlockSpec((1, S, LANES), lambda b, h: (b, 0, h)),
        out_shape=jax.ShapeDtypeStruct((B, S, n_heads * 64), BF16),
        scratch_shapes=[pltpu.VMEM((2, FOX_TK, FOX_TQ), F32), stat,
                        pltpu.VMEM((2, FOX_TK, FOX_TQ), BF16), stat,
                        stat, pltpu.VMEM((2, LANES, FOX_TQ), F32)],
        compiler_params=pltpu.CompilerParams(
            dimension_semantics=("arbitrary", "arbitrary"),
            vmem_limit_bytes=_vmem_limit(56 * 1024 * 1024)),
        name="fox_attn",
    )(cs_flat, ce_flat, qp, kp, vt)


def _odd_mix(x_tile, t, g_ref, w_ref, wp_ref, sp_ref, cw_ref, zbuf, xbuf, windows):
    tm = x_tile.shape[0]
    pw = sp_ref.shape[1]
    cwid = cw_ref.shape[1]

    @pl.when(t == 0)
    def _():
        zbuf[0:POOL_HALO, :] = jnp.zeros((POOL_HALO, pw), F32)
        xbuf[0:CONV_HALO, :] = jnp.zeros((CONV_HALO, cwid), F32)

    h = _rms(x_tile, g_ref[...]).astype(BF16)
    z = _dot(h, w_ref[...])
    zc = z[:, :pw]
    hdn = z[:, pw:pw + cwid]
    gb = z[:, pw + cwid:pw + 2 * cwid]
    gc = z[:, pw + 2 * cwid:pw + 3 * cwid]

    zbuf[POOL_HALO:POOL_HALO + tm, :] = zc
    pos = t * tm + lax.broadcasted_iota(jnp.int32, (tm, 1), 0)
    yc = []
    for g, w in enumerate(windows):
        sl = slice(g * LANES, (g + 1) * LANES)
        acc = zbuf[POOL_HALO:POOL_HALO + tm, sl]
        for j in range(1, w):
            acc = acc + zbuf[POOL_HALO - j:POOL_HALO - j + tm, sl]
        inv_cnt = 1.0 / jnp.minimum(pos + 1, w).astype(F32)
        p = acc * inv_cnt - zc[:, sl]
        yc.append((_dot(p.astype(BF16), wp_ref[g]) * sp_ref[:, sl]).astype(BF16))
    zbuf[0:POOL_HALO, :] = zbuf[tm:tm + POOL_HALO, :]

    xg = gc * hdn
    xbuf[CONV_HALO:CONV_HALO + tm, :] = xg
    k = cw_ref.shape[0]
    conv = cw_ref[k - 1:k, :] * xg
    for j in range(1, k):
        conv = conv + cw_ref[k - 1 - j:k - j, :] * xbuf[CONV_HALO - j:CONV_HALO - j + tm, :]
    yd = (gb * conv).astype(BF16)
    xbuf[0:CONV_HALO, :] = xbuf[tm:tm + CONV_HALO, :]
    return jnp.concatenate(yc, axis=-1), yd


def _mem_kv_kernel(m_ref, g_ref, w_ref, gk_ref, k_ref, v_ref):
    xa = k_ref.shape[2]
    hm = _rms(m_ref[0], g_ref[...]).astype(BF16)
    kv = _dot(hm, w_ref[...])
    for h in range(xa // LANES):
        sl = slice(h * LANES, (h + 1) * LANES)
        k_ref[0, :, sl] = _rms(kv[:, sl], gk_ref[...]).astype(BF16)
    v_ref[0] = kv[:, xa:].astype(BF16)


def _mem_kv(mem, g_mem, w_kv, g_k):
    B, M, D = mem.shape
    xa = w_kv.shape[1] // 2
    blk = lambda w: pl.BlockSpec((1, M, w), lambda b: (b, 0, 0))
    return pl.pallas_call(
        _mem_kv_kernel,
        grid=(B,),
        in_specs=[blk(D), _const_spec(g_mem.shape), _const_spec(w_kv.shape), _const_spec(g_k.shape)],
        out_specs=[blk(xa), blk(xa)],
        out_shape=[jax.ShapeDtypeStruct((B, M, xa), BF16), jax.ShapeDtypeStruct((B, M, xa), BF16)],
        compiler_params=pltpu.CompilerParams(dimension_semantics=("arbitrary",)),
        name="mem_kv",
    )(mem, g_mem, w_kv, g_k)


def _post_tile(x_tile, ya, yb, woa_ref, wob_ref, gxa_ref, wq_ref, gq_ref, k_ref, v_ref, wo_ref,
               gff_ref, wg_ref, wu_ref, wd_ref, hs_scr):
    xa = wq_ref.shape[1]
    dff = wg_ref.shape[1]
    inv_sqrt = 1.0 / math.sqrt(LANES)
    x1 = x_tile + _dot(ya, woa_ref[...]) + _dot(yb, wob_ref[...])

    hx = _rms(x1, gxa_ref[...]).astype(BF16)
    q = _dot(hx, wq_ref[...])
    outs = []
    for h in range(xa // LANES):
        sl = slice(h * LANES, (h + 1) * LANES)
        qn = _rms(q[:, sl], gq_ref[...]).astype(BF16)
        s = _dot_nt(qn, k_ref[0, :, sl]) * inv_sqrt
        m = jnp.max(s, axis=-1, keepdims=True)
        p = jnp.exp(s - m)
        l = jnp.sum(p, axis=-1, keepdims=True)
        outs.append((_dot(p.astype(BF16), v_ref[0, :, sl]) * (1.0 / l)).astype(BF16))
    x2 = x1 + _dot(jnp.concatenate(outs, axis=-1), wo_ref[...])

    hf = _rms(x2, gff_ref[...]).astype(BF16)
    c0 = 0
    while c0 < dff:
        c1 = min(c0 + FF_CHUNK, dff)
        a = _dot(hf, wg_ref[:, c0:c1])
        u = _dot(hf, wu_ref[:, c0:c1])
        hs_scr[:, c0:c1] = (a * jax.nn.sigmoid(a) * u).astype(BF16)
        c0 = c1
    return x2 + _dot(hs_scr[...], wd_ref[...])


def _post_kernel(x_ref, ya_ref, yb_ref, *rest):
    *w_refs, o_ref, hs_scr = rest
    o_ref[0] = _post_tile(x_ref[0], ya_ref[0], yb_ref[0], *w_refs, hs_scr)


def _odd_layer_kernel(x_ref, gmix_ref, win_ref, wp_ref, sp_ref, cw_ref, *rest, windows):
    *w_refs, o_ref, hs_scr, zbuf, xbuf = rest
    x_tile = x_ref[0]
    yc, yd = _odd_mix(x_tile, pl.program_id(1), gmix_ref, win_ref, wp_ref, sp_ref, cw_ref, zbuf, xbuf, windows)
    o_ref[0] = _post_tile(x_tile, yc, yd, *w_refs, hs_scr)


def _post_specs(B, S, D, tm, w_out, g_xa, w_q, g_q, k_mem, w_o, g_ffn, w_gate, w_up, w_down):
    M, xa = k_mem.shape[1], w_q.shape[1]
    half = w_out.shape[0] // 2
    memb = pl.BlockSpec((1, M, xa), lambda b, t: (b, 0, 0))
    cs = lambda a: _const_spec(a.shape, single=True)
    wo_half = lambda k: pl.BlockSpec((half, D), lambda b, t: (k, 0), pipeline_mode=pl.Buffered(1))
    return [wo_half(0), wo_half(1), cs(g_xa), cs(w_q), cs(g_q), memb, memb, cs(w_o), cs(g_ffn),
            cs(w_gate), cs(w_up), cs(w_down)]


def _post(x, ya, yb, w_out, g_xa, w_q, g_q, k_mem, v_mem, w_o, g_ffn, w_gate, w_up, w_down):
    B, S, D = x.shape
    tm = POST_TILE
    a_w, b_w = ya.shape[2], yb.shape[2]
    assert a_w == b_w and w_out.shape[0] == a_w + b_w
    row3 = lambda w: pl.BlockSpec((1, tm, w), lambda b, t: (b, t, 0))
    return pl.pallas_call(
        _post_kernel,
        grid=(B, S // tm),
        in_specs=[row3(D), row3(a_w), row3(b_w)] + _post_specs(B, S, D, tm, w_out, g_xa, w_q, g_q, k_mem, w_o,
                                                              g_ffn, w_gate, w_up, w_down),
        out_specs=row3(D),
        out_shape=jax.ShapeDtypeStruct((B, S, D), F32),
        scratch_shapes=[pltpu.VMEM((tm, w_gate.shape[1]), BF16)],
        compiler_params=pltpu.CompilerParams(
            dimension_semantics=("arbitrary", "arbitrary"),
            vmem_limit_bytes=_vmem_limit(58 * 1024 * 1024)),
        name="post",
    )(x, ya, yb, w_out, w_out, g_xa, w_q, g_q, k_mem, v_mem, w_o, g_ffn, w_gate, w_up, w_down)


def _odd_layer(x, g_mix, w_in, w_pool, s_pool, conv_w, windows,
               w_out, g_xa, w_q, g_q, k_mem, v_mem, w_o, g_ffn, w_gate, w_up, w_down):
    B, S, D = x.shape
    tm = POST_TILE
    pw, cwid = s_pool.shape[1], conv_w.shape[1]
    assert max(windows) <= POOL_HALO and conv_w.shape[0] - 1 <= CONV_HALO
    assert pw == cwid and w_out.shape[0] == pw + cwid
    row3 = lambda w: pl.BlockSpec((1, tm, w), lambda b, t: (b, t, 0))
    cs = lambda a: _const_spec(a.shape, single=True)
    return pl.pallas_call(
        functools.partial(_odd_layer_kernel, windows=windows),
        grid=(B, S // tm),
        in_specs=[row3(D), cs(g_mix), cs(w_in), cs(w_pool), cs(s_pool), cs(conv_w)]
        + _post_specs(B, S, D, tm, w_out, g_xa, w_q, g_q, k_mem, w_o, g_ffn, w_gate, w_up, w_down),
        out_specs=row3(D),
        out_shape=jax.ShapeDtypeStruct((B, S, D), F32),
        scratch_shapes=[pltpu.VMEM((tm, w_gate.shape[1]), BF16),
                        pltpu.VMEM((POOL_HALO + tm, pw), F32), pltpu.VMEM((CONV_HALO + tm, cwid), F32)],
        compiler_params=pltpu.CompilerParams(
            dimension_semantics=("arbitrary", "arbitrary"),
            vmem_limit_bytes=_vmem_limit(58 * 1024 * 1024)),
        name="odd_layer",
    )(x, g_mix, w_in, w_pool, s_pool, conv_w,
      w_out, w_out, g_xa, w_q, g_q, k_mem, v_mem, w_o, g_ffn, w_gate, w_up, w_down)


def kernel(x, mem, g_mix, g_xa, g_mem, xa_wq, xa_wkv, xa_wo, xa_gq, xa_gk, g_ffn, w_gate, w_up, w_down,
           e_w_in, e_b_f, e_g_v, e_w_s, e_b_s, e_g_qn, e_g_kn, e_w_out,
           o_w_in, o_w_pool, o_s_pool, o_conv_w, o_w_out):
    depth = g_mix.shape[0]
    B, S, D = x.shape
    row = lambda a: a.reshape(1, -1)
    tri = (lax.broadcasted_iota(jnp.int32, (ROW_TILE, ROW_TILE), 0)
           >= lax.broadcasted_iota(jnp.int32, (ROW_TILE, ROW_TILE), 1)).astype(BF16)
    pool_windows = (2, 4, 8, 16)[:o_w_pool.shape[1]]

    for layer in range(depth):
        i = layer // 2
        if layer % 2 == 0:
            n_heads = e_b_f.shape[1]
            a_w = e_g_v.shape[1]
            f_w = n_heads * e_g_qn.shape[1]
            n_uvqk = 2 * a_w + 2 * f_w
            w_main = e_w_in[i, :, :n_uvqk].astype(BF16)
            w_vt = e_w_in[i, :, n_uvqk:n_uvqk + f_w].T.astype(BF16)
            w_f = jnp.pad(e_w_in[i, :, n_uvqk + f_w:], ((0, 0), (0, LANES - n_heads))).astype(BF16)
            b_f = jnp.pad(e_b_f[i], (0, LANES - n_heads)).reshape(1, LANES)
            g_q2 = jnp.tile(e_g_qn[i], 2).reshape(1, LANES)
            g_k2 = jnp.tile(e_g_kn[i], 2).reshape(1, LANES)
            ya, qp, kp, vt, cs, ce = _even_in(x, row(g_mix[layer]), w_main, w_vt, w_f, b_f, row(e_g_v[i]),
                                              e_w_s[i], e_b_s[i].T, g_q2, g_k2, tri)
            yb = _fox(cs[:, ::8, :n_heads].reshape(-1), ce[:, ::8, :n_heads].reshape(-1), qp, kp, vt)
        k_mem, v_mem = _mem_kv(mem, row(g_mem[layer]), xa_wkv[layer].astype(BF16), row(xa_gk[layer]))
        post_args = (row(g_xa[layer]), xa_wq[layer].astype(BF16), row(xa_gq[layer]), k_mem, v_mem,
                     xa_wo[layer].astype(BF16), row(g_ffn[layer]),
                     w_gate[layer].astype(BF16), w_up[layer].astype(BF16), w_down[layer].astype(BF16))
        if layer % 2 == 0:
            x = _post(x, ya, yb, e_w_out[i].astype(BF16), *post_args)
        else:
            x = _odd_layer(x, row(g_mix[layer]), o_w_in[i].astype(BF16), o_w_pool[i].astype(BF16),
                           row(o_s_pool[i]), o_conv_w[i], pool_windows, o_w_out[i].astype(BF16), *post_args)
    return x
```

```python
import functools
import math

import jax
import jax.numpy as jnp
import numpy as np
from jax import lax
from jax.experimental import pallas as pl
from jax.experimental.pallas import tpu as pltpu

F32 = jnp.float32
BF16 = jnp.bfloat16
EPS = 1e-6
LOG2E = 1.4426950408889634
NEG_BIG = -1e30

LANES = 128
MXU_TILE = 256
V7X_VMEM_BYTES = 64 * 1024 * 1024

ROW_TILE = 1024
POST_TILE = 1024
FOX_TQ = 2048
FOX_TK = 512
POOL_HALO = 16
CONV_HALO = 8
FF_CHUNK = 256


def _vmem_limit(nbytes):
    return int(min(nbytes, V7X_VMEM_BYTES - 4 * 1024 * 1024))


def _rms(x, g):
    ms = jnp.mean(x * x, axis=-1, keepdims=True)
    return (x * lax.rsqrt(ms + EPS)) * g


def _dot(a, b):
    return jnp.dot(a, b, preferred_element_type=F32)


def _dot_nt(a, b):
    return lax.dot_general(a, b, (((1,), (1,)), ((), ())), preferred_element_type=F32)


def _split3(x):
    hi = x.astype(BF16)
    r1 = x - hi.astype(F32)
    mid = r1.astype(BF16)
    lo = (r1 - mid.astype(F32)).astype(BF16)
    return hi, mid, lo


def _const_spec(shape, single=False):
    nd = len(shape)
    kw = {}
    if single:
        kw["pipeline_mode"] = pl.Buffered(1)
    return pl.BlockSpec(shape, lambda *_: (0,) * nd, **kw)


def _even_in_kernel(x_ref, g_ref, w_ref, wvt_ref, wf_ref, bf_ref, gv_ref, ws_ref, bst_ref, gq_ref, gk_ref,
                    tri_ref, route_ref, ya_ref, qp_ref, kp_ref, vt_ref, cs_ref, ce_ref, run_ref, aoff_ref, cq0_ref,
                    *, blocks_per_q):
    t = pl.program_id(1)
    tm = x_ref.shape[1]
    a_w = ya_ref.shape[2]
    f_w = qp_ref.shape[2] // 2
    n_grp = a_w // LANES

    @pl.when(t == 0)
    def _():
        run_ref[...] = jnp.zeros_like(run_ref)

    h = _rms(x_ref[0], g_ref[...]).astype(BF16)
    z = _dot(h, w_ref[:, 0:2 * a_w + 2 * f_w])
    zvt = _dot_nt(wvt_ref[...], h)
    fl = _dot(h, wf_ref[...]) + bf_ref[...]
    logf = -(jnp.maximum(-fl, 0.0) + jnp.log1p(jnp.exp(-jnp.abs(fl)))) * LOG2E

    uv = jax.nn.gelu(z[:, :2 * a_w])
    row = lax.broadcasted_iota(jnp.int32, (LANES, LANES), 0) // 64
    col = lax.broadcasted_iota(jnp.int32, (LANES, LANES), 1) // 64
    chunk_mask = row >= col
    for g in range(n_grp):
        sl = slice(g * LANES, (g + 1) * LANES)
        vg = uv[:, a_w + g * LANES:a_w + (g + 1) * LANES]
        vn = _rms(vg, gv_ref[:, sl]).astype(BF16)
        wm = jnp.where(chunk_mask, ws_ref[g], 0.0).astype(BF16)
        bias = bst_ref[:, g:g + 1]
        for n in range(tm // LANES):
            rs = slice(n * LANES, (n + 1) * LANES)
            s = _dot(wm, vn[rs]) + bias
            ya_ref[0, rs, sl] = (uv[rs, sl] * s).astype(BF16)

    tri = tri_ref[...]
    nsub = tm // FOX_TK
    run = run_ref[...]
    a_off = aoff_ref[...]
    c_q0 = cq0_ref[...]
    a_parts, b_parts = [], []
    for r in range(nsub):
        lf = logf[r * FOX_TK:(r + 1) * FOX_TK]
        hi, mid, lo = _split3(lf)
        lc = _dot(tri, hi) + _dot(tri, mid) + _dot(tri, lo)
        first = lf[0:1]
        tot = lc[FOX_TK - 1:FOX_TK]
        q_start = ((t * nsub + r) % blocks_per_q) == 0
        a_off = jnp.where(q_start, -first, a_off)
        c_q0 = jnp.where(q_start, run + first, c_q0)
        a_parts.append(a_off + lc)
        b_parts.append(tot - lc)
        ce_ref[0, r * 8:(r + 1) * 8] = jnp.broadcast_to(run + tot, (8, LANES))
        a_off = a_off + tot
        run = run + tot
    a_all = jnp.concatenate(a_parts, axis=0)
    b_all = jnp.concatenate(b_parts, axis=0)
    cs_ref[0] = jnp.broadcast_to(c_q0, (8, LANES))
    aoff_ref[...] = a_off
    cq0_ref[...] = c_q0
    run_ref[...] = run

    lane = lax.broadcasted_iota(jnp.int32, (tm, LANES), 1)
    low = lane < 64
    hd = 64
    q_off = 2 * a_w
    k_off = 2 * a_w + f_w

    def head_norm(blk, gain):
        sq = blk * blk
        s_lo = jnp.sum(jnp.where(low, sq, 0.0), axis=-1, keepdims=True)
        s_hi = jnp.sum(jnp.where(low, 0.0, sq), axis=-1, keepdims=True)
        r = jnp.where(low, lax.rsqrt(s_lo / hd + EPS), lax.rsqrt(s_hi / hd + EPS))
        return (blk * r) * gain

    n_heads = f_w // hd
    ab3 = jnp.concatenate(list(_split3(a_all)) + list(_split3(b_all)), axis=-1)
    routed = _dot(ab3, route_ref[...])
    l64 = lane & 63
    x_q = jnp.where((l64 >= 3 * n_heads) & (l64 < 6 * n_heads), 1.0, routed[:, :LANES])
    y_all = routed[:, LANES:]

    def k_extras(h):
        return jnp.where((l64 >= 3 * h) & (l64 < 3 * h + 3), 1.0,
                         jnp.where((l64 >= 3 * (n_heads + h)) & (l64 < 3 * (n_heads + h) + 3), y_all, 0.0))

    for j in range(f_w // LANES):
        sl = slice(j * LANES, (j + 1) * LANES)
        qn = head_norm(z[:, q_off + j * LANES:q_off + (j + 1) * LANES], gq_ref[...]) * (LOG2E / math.sqrt(hd))
        kn = head_norm(z[:, k_off + j * LANES:k_off + (j + 1) * LANES], gk_ref[...])
        for hh in range(2):
            hidx = 2 * j + hh
            qhalf = low if hh == 0 else jnp.logical_not(low)
            osl = slice(hidx * LANES, (hidx + 1) * LANES)
            qp_ref[0, :, osl] = jnp.where(qhalf, qn, x_q).astype(BF16)
            kp_ref[0, :, osl] = jnp.where(qhalf, kn, k_extras(hidx)).astype(BF16)

    for hidx in range(n_heads):
        vt_ref[0, hidx, 0:hd, :] = zvt[hidx * hd:(hidx + 1) * hd].astype(BF16)
        vt_ref[0, hidx, hd:2 * hd, :] = jnp.ones((hd, tm), BF16)


def _route_matrix(n_heads):
    assert 6 * n_heads <= 64
    r = np.zeros((6 * LANES, 2 * LANES), np.float32)
    for h in range(n_heads):
        for x in range(3):
            for half in (0, 64):
                r[x * LANES + h, half + 3 * h + x] = 1.0
                r[(3 + x) * LANES + h, LANES + half + 3 * (n_heads + h) + x] = 1.0
    return jnp.asarray(r, BF16)


def _even_in(x, g_mix, w_main, w_vt, w_f, b_f, g_v, w_s, b_s_t, g_q2, g_k2, tri):
    B, S, D = x.shape
    tm = ROW_TILE
    assert tm % FOX_TK == 0 and FOX_TQ % tm == 0
    nsub = tm // FOX_TK
    blocks_per_q = FOX_TQ // FOX_TK
    a_w = g_v.shape[1]
    f_w = w_vt.shape[0]
    n_heads = f_w // 64
    route = _route_matrix(n_heads)
    grid = (B, S // tm)
    row3 = lambda w: pl.BlockSpec((1, tm, w), lambda b, t: (b, t, 0))
    return pl.pallas_call(
        functools.partial(_even_in_kernel, blocks_per_q=blocks_per_q),
        grid=grid,
        in_specs=[row3(D), _const_spec(g_mix.shape), _const_spec(w_main.shape), _const_spec(w_vt.shape),
                  _const_spec(w_f.shape), _const_spec(b_f.shape), _const_spec(g_v.shape), _const_spec(w_s.shape),
                  _const_spec(b_s_t.shape), _const_spec(g_q2.shape), _const_spec(g_k2.shape),
                  _const_spec(tri.shape), _const_spec(route.shape)],
        out_specs=[row3(a_w), row3(n_heads * LANES), row3(n_heads * LANES),
                   pl.BlockSpec((1, n_heads, LANES, tm), lambda b, t: (b, 0, 0, t)),
                   pl.BlockSpec((1, 8, LANES), lambda b, t: (b, (t * nsub) // blocks_per_q, 0)),
                   pl.BlockSpec((1, 8 * nsub, LANES), lambda b, t: (b, t, 0))],
        out_shape=[jax.ShapeDtypeStruct((B, S, a_w), BF16),
                   jax.ShapeDtypeStruct((B, S, n_heads * LANES), BF16),
                   jax.ShapeDtypeStruct((B, S, n_heads * LANES), BF16),
                   jax.ShapeDtypeStruct((B, n_heads, LANES, S), BF16),
                   jax.ShapeDtypeStruct((B, (S // FOX_TQ) * 8, LANES), F32),
                   jax.ShapeDtypeStruct((B, (S // FOX_TK) * 8, LANES), F32)],
        scratch_shapes=[pltpu.VMEM((1, LANES), F32), pltpu.VMEM((1, LANES), F32), pltpu.VMEM((1, LANES), F32)],
        compiler_params=pltpu.CompilerParams(
            dimension_semantics=("arbitrary", "arbitrary"),
            vmem_limit_bytes=_vmem_limit(48 * 1024 * 1024)),
        name="even_in",
    )(x, g_mix, w_main, w_vt, w_f, b_f, g_v, w_s, b_s_t, g_q2, g_k2, tri, route)


def _fox_kernel(cs_ref, ce_ref, qp_ref, kp_ref, vt_ref, o_ref, s_buf, mx_buf, p_buf, al_buf, m_scr, acc_scr,
                *, nq, nk, n_heads):
    b = pl.program_id(0)
    hp = pl.program_id(1)
    tq, tk = FOX_TQ, FOX_TK
    sub = tq // tk
    T = MXU_TILE
    nct = tq // T
    nkt = tk // T
    zslab = jnp.zeros((T, LANES), BF16)

    def qk(i, j, masked, col_lo=0):
        q0 = pl.multiple_of(i * tq, tq)
        k0 = pl.multiple_of(j * tk, tk)
        for c in range(col_lo // T, nct):
            mxu = c % 2
            pltpu.matmul_push_rhs(qp_ref[0, pl.ds(q0 + c * T, T), :], staging_register=0, mxu_index=mxu,
                                  transpose=True)
            first = True
            for hh in range(2):
                for kt in range(nkt):
                    ks = kp_ref[0, pl.ds(k0 + kt * T, T), hh * LANES:(hh + 1) * LANES]
                    lhs = jnp.concatenate([ks, zslab] if hh == 0 else [zslab, ks], axis=1)
                    a_qk = ((hh * nkt + kt) % 2) * (T // 4)
                    pltpu.matmul_acc_lhs(a_qk, lhs, mxu, load_staged_rhs=0 if first else None)
                    first = False
                    st = pltpu.matmul_pop(a_qk, (T, T), F32, mxu)
                    if masked:
                        ri = lax.broadcasted_iota(jnp.int32, (T, T), 0) + kt * T
                        ci = lax.broadcasted_iota(jnp.int32, (T, T), 1) + (c * T - col_lo)
                        st = jnp.where(ci >= ri, st, NEG_BIG)
                    s_buf[hh, kt * T:(kt + 1) * T, c * T:(c + 1) * T] = st
                    cm = jnp.max(st, axis=0, keepdims=True)
                    cmax = cm if kt == 0 else jnp.maximum(cmax, cm)
                mx_buf[hh, :, c * T:(c + 1) * T] = jnp.broadcast_to(cmax, (8, T))

    def ex(i, j, col_lo=0):
        for hh in range(2):
            head = hp * 2 + hh
            d = cs_ref[(b * nq + i) * n_heads + head] - ce_ref[(b * nk + j) * n_heads + head]
            for c in range(col_lo // T, nct):
                cs_ = slice(c * T, (c + 1) * T)
                m_old = m_scr[hh, :, cs_]
                m_new = jnp.maximum(m_old, mx_buf[hh, :, cs_] + d)
                al_buf[hh, :, cs_] = jnp.exp2(m_old - m_new)
                m_scr[hh, :, cs_] = m_new
                shift = jnp.broadcast_to((m_new - d)[0:1], (T, T))
                for kt in range(nkt):
                    rs = slice(kt * T, (kt + 1) * T)
                    p_buf[hh, rs, cs_] = jnp.exp2(s_buf[hh, rs, cs_] - shift).astype(BF16)

    def pv(i, j, col_lo=0):
        k0 = pl.multiple_of(j * tk, tk)
        n = 0
        for hh in range(2):
            for c in range(col_lo // T, nct):
                mxu = c % 2
                cs_ = slice(c * T, (c + 1) * T)
                a_pv = 2 * (T // 4) + (LANES // 4) * ((n // 2) % 4)
                n += 1
                for kt in range(nkt):
                    pltpu.matmul_push_rhs(p_buf[hh, kt * T:(kt + 1) * T, cs_], staging_register=1, mxu_index=mxu)
                    pltpu.matmul_acc_lhs(a_pv, vt_ref[0, hh, :, pl.ds(k0 + kt * T, T)], mxu, load_staged_rhs=1)
                out = pltpu.matmul_pop(a_pv, (LANES, T), F32, mxu)
                acc_scr[hh, :, cs_] = (jnp.broadcast_to(al_buf[hh, 0:1, cs_], (LANES, T)) * acc_scr[hh, :, cs_]
                                       + out)

    def tail(i, n_full, has_full_blocks):
        for t in range(sub + 2):
            for stage, off in ((pv, t - 2), (ex, t - 1)):
                if off >= 0:
                    stage(i, n_full + off, off * tk)
                elif has_full_blocks:
                    stage(i, n_full + off)
            if t < sub:
                qk(i, n_full + t, True, t * tk)

    def q_body(i, carry):
        q0 = pl.multiple_of(i * tq, tq)
        n_full = i * sub
        for hh in range(2):
            m_scr[hh] = jnp.full((8, tq), NEG_BIG, F32)
            acc_scr[hh] = jnp.zeros((LANES, tq), F32)

        @pl.when(i == 0)
        def _():
            tail(i, 0, False)

        @pl.when(i > 0)
        def _():
            assert sub >= 2
            qk(i, 0, False)
            ex(i, 0)
            qk(i, 1, False)

            def body(s, c):
                pv(i, s - 2)
                ex(i, s - 1)
                qk(i, s, False)
                return c

            lax.fori_loop(2, n_full, body, 0)
            tail(i, n_full, True)

        hd = LANES // 2
        tops = []
        for hh in range(2):
            a = acc_scr[hh]
            tops.append(a[0:hd] * (1.0 / a[hd:2 * hd]))
        o_ref[0, pl.ds(q0, tq), :] = jnp.concatenate(tops, axis=0).T.astype(BF16)
        return carry

    lax.fori_loop(0, nq, q_body, 0)


def _fox(cs_flat, ce_flat, qp, kp, vt):
    B, S, HW = qp.shape
    n_heads = HW // LANES
    nq, nk = S // FOX_TQ, S // FOX_TK
    assert FOX_TQ % MXU_TILE == 0 and FOX_TK % MXU_TILE == 0 and 2 * LANES == MXU_TILE
    grid = (B, n_heads // 2)
    slab = pl.BlockSpec((1, S, 2 * LANES), lambda b, h: (b, 0, h))
    stat = pltpu.VMEM((2, 8, FOX_TQ), F32)
    return pl.pallas_call(
        functools.partial(_fox_kernel, nq=nq, nk=nk, n_heads=n_heads),
        grid=grid,
        in_specs=[pl.BlockSpec(memory_space=pltpu.SMEM), pl.BlockSpec(memory_space=pltpu.SMEM),
                  slab, slab, pl.BlockSpec((1, 2, LANES, S), lambda b, h: (b, h, 0, 0))],
        out_specs=pl.BlockSpec((1, S, LANES), lambda b, h: (b, 0, h)),
        out_shape=jax.ShapeDtypeStruct((B, S, n_heads * 64), BF16),
        scratch_shapes=[pltpu.VMEM((2, FOX_TK, FOX_TQ), F32), stat,
                        pltpu.VMEM((2, FOX_TK, FOX_TQ), BF16), stat,
                        stat, pltpu.VMEM((2, LANES, FOX_TQ), F32)],
        compiler_params=pltpu.CompilerParams(
            dimension_semantics=("arbitrary", "arbitrary"),
            vmem_limit_bytes=_vmem_limit(56 * 1024 * 1024)),
        name="fox_attn",
    )(cs_flat, ce_flat, qp, kp, vt)


def _odd_mix(x_tile, t, g_ref, w_ref, wp_ref, sp_ref, cw_ref, zbuf, xbuf, windows):
    tm = x_tile.shape[0]
    pw = sp_ref.shape[1]
    cwid = cw_ref.shape[1]

    @pl.when(t == 0)
    def _():
        zbuf[0:POOL_HALO, :] = jnp.zeros((POOL_HALO, pw), F32)
        xbuf[0:CONV_HALO, :] = jnp.zeros((CONV_HALO, cwid), F32)

    h = _rms(x_tile, g_ref[...]).astype(BF16)
    z = _dot(h, w_ref[...])
    zc = z[:, :pw]
    hdn = z[:, pw:pw + cwid]
    gb = z[:, pw + cwid:pw + 2 * cwid]
    gc = z[:, pw + 2 * cwid:pw + 3 * cwid]

    zbuf[POOL_HALO:POOL_HALO + tm, :] = zc
    pos = t * tm + lax.broadcasted_iota(jnp.int32, (tm, 1), 0)
    yc = []
    for g, w in enumerate(windows):
        sl = slice(g * LANES, (g + 1) * LANES)
        acc = zbuf[POOL_HALO:POOL_HALO + tm, sl]
        for j in range(1, w):
            acc = acc + zbuf[POOL_HALO - j:POOL_HALO - j + tm, sl]
        inv_cnt = 1.0 / jnp.minimum(pos + 1, w).astype(F32)
        p = acc * inv_cnt - zc[:, sl]
        yc.append((_dot(p.astype(BF16), wp_ref[g]) * sp_ref[:, sl]).astype(BF16))
    zbuf[0:POOL_HALO, :] = zbuf[tm:tm + POOL_HALO, :]

    xg = gc * hdn
    xbuf[CONV_HALO:CONV_HALO + tm, :] = xg
    k = cw_ref.shape[0]
    conv = cw_ref[k - 1:k, :] * xg
    for j in range(1, k):
        conv = conv + cw_ref[k - 1 - j:k - j, :] * xbuf[CONV_HALO - j:CONV_HALO - j + tm, :]
    yd = (gb * conv).astype(BF16)
    xbuf[0:CONV_HALO, :] = xbuf[tm:tm + CONV_HALO, :]
    return jnp.concatenate(yc, axis=-1), yd


def _mem_kv_kernel(m_ref, g_ref, w_ref, gk_ref, k_ref, v_ref):
    xa = k_ref.shape[2]
    hm = _rms(m_ref[0], g_ref[...]).astype(BF16)
    kv = _dot(hm, w_ref[...])
    for h in range(xa // LANES):
        sl = slice(h * LANES, (h + 1) * LANES)
        k_ref[0, :, sl] = _rms(kv[:, sl], gk_ref[...]).astype(BF16)
    v_ref[0] = kv[:, xa:].astype(BF16)


def _mem_kv(mem, g_mem, w_kv, g_k):
    B, M, D = mem.shape
    xa = w_kv.shape[1] // 2
    blk = lambda w: pl.BlockSpec((1, M, w), lambda b: (b, 0, 0))
    return pl.pallas_call(
        _mem_kv_kernel,
        grid=(B,),
        in_specs=[blk(D), _const_spec(g_mem.shape), _const_spec(w_kv.shape), _const_spec(g_k.shape)],
        out_specs=[blk(xa), blk(xa)],
        out_shape=[jax.ShapeDtypeStruct((B, M, xa), BF16), jax.ShapeDtypeStruct((B, M, xa), BF16)],
        compiler_params=pltpu.CompilerParams(dimension_semantics=("arbitrary",)),
        name="mem_kv",
    )(mem, g_mem, w_kv, g_k)


def _post_tile(x_tile, ya, yb, woa_ref, wob_ref, gxa_ref, wq_ref, gq_ref, k_ref, v_ref, wo_ref,
               gff_ref, wg_ref, wu_ref, wd_ref, hs_scr):
    xa = wq_ref.shape[1]
    dff = wg_ref.shape[1]
    inv_sqrt = 1.0 / math.sqrt(LANES)
    x1 = x_tile + _dot(ya, woa_ref[...]) + _dot(yb, wob_ref[...])

    hx = _rms(x1, gxa_ref[...]).astype(BF16)
    q = _dot(hx, wq_ref[...])
    outs = []
    for h in range(xa // LANES):
        sl = slice(h * LANES, (h + 1) * LANES)
        qn = _rms(q[:, sl], gq_ref[...]).astype(BF16)
        s = _dot_nt(qn, k_ref[0, :, sl]) * inv_sqrt
        m = jnp.max(s, axis=-1, keepdims=True)
        p = jnp.exp(s - m)
        l = jnp.sum(p, axis=-1, keepdims=True)
        outs.append((_dot(p.astype(BF16), v_ref[0, :, sl]) * (1.0 / l)).astype(BF16))
    x2 = x1 + _dot(jnp.concatenate(outs, axis=-1), wo_ref[...])

    hf = _rms(x2, gff_ref[...]).astype(BF16)
    c0 = 0
    while c0 < dff:
        c1 = min(c0 + FF_CHUNK, dff)
        a = _dot(hf, wg_ref[:, c0:c1])
        u = _dot(hf, wu_ref[:, c0:c1])
        hs_scr[:, c0:c1] = (a * jax.nn.sigmoid(a) * u).astype(BF16)
        c0 = c1
    return x2 + _dot(hs_scr[...], wd_ref[...])


def _post_kernel(x_ref, ya_ref, yb_ref, *rest):
    *w_refs, o_ref, hs_scr = rest
    o_ref[0] = _post_tile(x_ref[0], ya_ref[0], yb_ref[0], *w_refs, hs_scr)


def _odd_layer_kernel(x_ref, gmix_ref, win_ref, wp_ref, sp_ref, cw_ref, *rest, windows):
    *w_refs, o_ref, hs_scr, zbuf, xbuf = rest
    x_tile = x_ref[0]
    yc, yd = _odd_mix(x_tile, pl.program_id(1), gmix_ref, win_ref, wp_ref, sp_ref, cw_ref, zbuf, xbuf, windows)
    o_ref[0] = _post_tile(x_tile, yc, yd, *w_refs, hs_scr)


def _post_specs(B, S, D, tm, w_out, g_xa, w_q, g_q, k_mem, w_o, g_ffn, w_gate, w_up, w_down):
    M, xa = k_mem.shape[1], w_q.shape[1]
    half = w_out.shape[0] // 2
    memb = pl.BlockSpec((1, M, xa), lambda b, t: (b, 0, 0))
    cs = lambda a: _const_spec(a.shape, single=True)
    wo_half = lambda k: pl.BlockSpec((half, D), lambda b, t: (k, 0), pipeline_mode=pl.Buffered(1))
    return [wo_half(0), wo_half(1), cs(g_xa), cs(w_q), cs(g_q), memb, memb, cs(w_o), cs(g_ffn),
            cs(w_gate), cs(w_up), cs(w_down)]


def _post(x, ya, yb, w_out, g_xa, w_q, g_q, k_mem, v_mem, w_o, g_ffn, w_gate, w_up, w_down):
    B, S, D = x.shape
    tm = POST_TILE
    a_w, b_w = ya.shape[2], yb.shape[2]
    assert a_w == b_w and w_out.shape[0] == a_w + b_w
    row3 = lambda w: pl.BlockSpec((1, tm, w), lambda b, t: (b, t, 0))
    return pl.pallas_call(
        _post_kernel,
        grid=(B, S // tm),
        in_specs=[row3(D), row3(a_w), row3(b_w)] + _post_specs(B, S, D, tm, w_out, g_xa, w_q, g_q, k_mem, w_o,
                                                              g_ffn, w_gate, w_up, w_down),
        out_specs=row3(D),
        out_shape=jax.ShapeDtypeStruct((B, S, D), F32),
        scratch_shapes=[pltpu.VMEM((tm, w_gate.shape[1]), BF16)],
        compiler_params=pltpu.CompilerParams(
            dimension_semantics=("arbitrary", "arbitrary"),
            vmem_limit_bytes=_vmem_limit(58 * 1024 * 1024)),
        name="post",
    )(x, ya, yb, w_out, w_out, g_xa, w_q, g_q, k_mem, v_mem, w_o, g_ffn, w_gate, w_up, w_down)


def _odd_layer(x, g_mix, w_in, w_pool, s_pool, conv_w, windows,
               w_out, g_xa, w_q, g_q, k_mem, v_mem, w_o, g_ffn, w_gate, w_up, w_down):
    B, S, D = x.shape
    tm = POST_TILE
    pw, cwid = s_pool.shape[1], conv_w.shape[1]
    assert max(windows) <= POOL_HALO and conv_w.shape[0] - 1 <= CONV_HALO
    assert pw == cwid and w_out.shape[0] == pw + cwid
    row3 = lambda w: pl.BlockSpec((1, tm, w), lambda b, t: (b, t, 0))
    cs = lambda a: _const_spec(a.shape, single=True)
    return pl.pallas_call(
        functools.partial(_odd_layer_kernel, windows=windows),
        grid=(B, S // tm),
        in_specs=[row3(D), cs(g_mix), cs(w_in), cs(w_pool), cs(s_pool), cs(conv_w)]
        + _post_specs(B, S, D, tm, w_out, g_xa, w_q, g_q, k_mem, w_o, g_ffn, w_gate, w_up, w_down),
        out_specs=row3(D),
        out_shape=jax.ShapeDtypeStruct((B, S, D), F32),
        scratch_shapes=[pltpu.VMEM((tm, w_gate.shape[1]), BF16),
                        pltpu.VMEM((POOL_HALO + tm, pw), F32), pltpu.VMEM((CONV_HALO + tm, cwid), F32)],
        compiler_params=pltpu.CompilerParams(
            dimension_semantics=("arbitrary", "arbitrary"),
            vmem_limit_bytes=_vmem_limit(58 * 1024 * 1024)),
        name="odd_layer",
    )(x, g_mix, w_in, w_pool, s_pool, conv_w,
      w_out, w_out, g_xa, w_q, g_q, k_mem, v_mem, w_o, g_ffn, w_gate, w_up, w_down)


def kernel(x, mem, g_mix, g_xa, g_mem, xa_wq, xa_wkv, xa_wo, xa_gq, xa_gk, g_ffn, w_gate, w_up, w_down,
           e_w_in, e_b_f, e_g_v, e_w_s, e_b_s, e_g_qn, e_g_kn, e_w_out,
           o_w_in, o_w_pool, o_s_pool, o_conv_w, o_w_out):
    depth = g_mix.shape[0]
    B, S, D = x.shape
    row = lambda a: a.reshape(1, -1)
    tri = (lax.broadcasted_iota(jnp.int32, (FOX_TK, FOX_TK), 0)
           >= lax.broadcasted_iota(jnp.int32, (FOX_TK, FOX_TK), 1)).astype(BF16)
    pool_windows = (2, 4, 8, 16)[:o_w_pool.shape[1]]

    for layer in range(depth):
        i = layer // 2
        if layer % 2 == 0:
            n_heads = e_b_f.shape[1]
            a_w = e_g_v.shape[1]
            f_w = n_heads * e_g_qn.shape[1]
            n_uvqk = 2 * a_w + 2 * f_w
            w_main = e_w_in[i].astype(BF16)
            w_vt = w_main[:, n_uvqk:n_uvqk + f_w].T
            w_f = jnp.pad(w_main[:, n_uvqk + f_w:], ((0, 0), (0, LANES - n_heads)))
            b_f = jnp.pad(e_b_f[i], (0, LANES - n_heads)).reshape(1, LANES)
            g_q2 = jnp.tile(e_g_qn[i], 2).reshape(1, LANES)
            g_k2 = jnp.tile(e_g_kn[i], 2).reshape(1, LANES)
            ya, qp, kp, vt, cs, ce = _even_in(x, row(g_mix[layer]), w_main, w_vt, w_f, b_f, row(e_g_v[i]),
                                              e_w_s[i], e_b_s[i].T, g_q2, g_k2, tri)
            yb = _fox(cs[:, ::8, :n_heads].reshape(-1), ce[:, ::8, :n_heads].reshape(-1), qp, kp, vt)
        k_mem, v_mem = _mem_kv(mem, row(g_mem[layer]), xa_wkv[layer].astype(BF16), row(xa_gk[layer]))
        post_args = (row(g_xa[layer]), xa_wq[layer].astype(BF16), row(xa_gq[layer]), k_mem, v_mem,
                     xa_wo[layer].astype(BF16), row(g_ffn[layer]),
                     w_gate[layer].astype(BF16), w_up[layer].astype(BF16), w_down[layer].astype(BF16))
        if layer % 2 == 0:
            x = _post(x, ya, yb, e_w_out[i].astype(BF16), *post_args)
        else:
            x = _odd_layer(x, row(g_mix[layer]), o_w_in[i].astype(BF16), o_w_pool[i].astype(BF16),
                           row(o_s_pool[i]), o_conv_w[i], pool_windows, o_w_out[i].astype(BF16), *post_args)
    return x
```

```python
import functools
import math

import jax
import jax.numpy as jnp
import numpy as np
from jax import lax
from jax.experimental import pallas as pl
from jax.experimental.pallas import tpu as pltpu

F32 = jnp.float32
BF16 = jnp.bfloat16
EPS = 1e-6
LOG2E = 1.4426950408889634
NEG_BIG = -1e30

LANES = 128
MXU_TILE = 256
V7X_VMEM_BYTES = 64 * 1024 * 1024

ROW_TILE = 1024
POST_TILE = 1024
FOX_TQ = 2048
FOX_TK = 512
POOL_HALO = 16
CONV_HALO = 8
FF_CHUNK = 256


def _vmem_limit(nbytes):
    return int(min(nbytes, V7X_VMEM_BYTES - 4 * 1024 * 1024))


def _rms(x, g):
    ms = jnp.mean(x * x, axis=-1, keepdims=True)
    return (x * lax.rsqrt(ms + EPS)) * g


def _dot(a, b):
    return jnp.dot(a, b, preferred_element_type=F32)


def _dot_nt(a, b):
    return lax.dot_general(a, b, (((1,), (1,)), ((), ())), preferred_element_type=F32)


def _split3(x):
    hi = x.astype(BF16)
    r1 = x - hi.astype(F32)
    mid = r1.astype(BF16)
    lo = (r1 - mid.astype(F32)).astype(BF16)
    return hi, mid, lo


def _const_spec(shape, single=False):
    nd = len(shape)
    kw = {}
    if single:
        kw["pipeline_mode"] = pl.Buffered(1)
    return pl.BlockSpec(shape, lambda *_: (0,) * nd, **kw)


def _even_in_kernel(x_ref, g_ref, w_ref, wvt_ref, wf_ref, bf_ref, gv_ref, ws_ref, bst_ref, gq_ref, gk_ref,
                    tri_ref, route_ref, ya_ref, qp_ref, kp_ref, vt_ref, cs_ref, ce_ref, run_ref, aoff_ref, cq0_ref,
                    *, blocks_per_q):
    t = pl.program_id(1)
    tm = x_ref.shape[1]
    a_w = ya_ref.shape[2]
    f_w = qp_ref.shape[2] // 2
    n_grp = a_w // LANES

    @pl.when(t == 0)
    def _():
        run_ref[...] = jnp.zeros_like(run_ref)

    h = _rms(x_ref[0], g_ref[...]).astype(BF16)
    z = _dot(h, w_ref[:, 0:2 * a_w + 2 * f_w])
    zvt = _dot_nt(wvt_ref[...], h)
    fl = _dot(h, wf_ref[...]) + bf_ref[...]
    logf = -(jnp.maximum(-fl, 0.0) + jnp.log1p(jnp.exp(-jnp.abs(fl)))) * LOG2E

    uv = jax.nn.gelu(z[:, :2 * a_w])
    row = lax.broadcasted_iota(jnp.int32, (LANES, LANES), 0) // 64
    col = lax.broadcasted_iota(jnp.int32, (LANES, LANES), 1) // 64
    chunk_mask = row >= col
    for g in range(n_grp):
        sl = slice(g * LANES, (g + 1) * LANES)
        vg = uv[:, a_w + g * LANES:a_w + (g + 1) * LANES]
        vn = _rms(vg, gv_ref[:, sl]).astype(BF16)
        wm = jnp.where(chunk_mask, ws_ref[g], 0.0).astype(BF16)
        bias = bst_ref[:, g:g + 1]
        for n in range(tm // LANES):
            rs = slice(n * LANES, (n + 1) * LANES)
            s = _dot(wm, vn[rs]) + bias
            ya_ref[0, rs, sl] = (uv[rs, sl] * s).astype(BF16)

    tri = tri_ref[...]
    nsub = tm // FOX_TK
    run = run_ref[...]
    a_off = aoff_ref[...]
    c_q0 = cq0_ref[...]
    a_parts, b_parts = [], []
    for r in range(nsub):
        lf = logf[r * FOX_TK:(r + 1) * FOX_TK]
        hi, mid, lo = _split3(lf)
        lc = _dot(tri, hi) + _dot(tri, mid) + _dot(tri, lo)
        first = lf[0:1]
        tot = lc[FOX_TK - 1:FOX_TK]
        q_start = ((t * nsub + r) % blocks_per_q) == 0
        a_off = jnp.where(q_start, -first, a_off)
        c_q0 = jnp.where(q_start, run + first, c_q0)
        a_parts.append(a_off + lc)
        b_parts.append(tot - lc)
        ce_ref[0, r * 8:(r + 1) * 8] = jnp.broadcast_to(run + tot, (8, LANES))
        a_off = a_off + tot
        run = run + tot
    a_all = jnp.concatenate(a_parts, axis=0)
    b_all = jnp.concatenate(b_parts, axis=0)
    cs_ref[0] = jnp.broadcast_to(c_q0, (8, LANES))
    aoff_ref[...] = a_off
    cq0_ref[...] = c_q0
    run_ref[...] = run

    lane = lax.broadcasted_iota(jnp.int32, (tm, LANES), 1)
    low = lane < 64
    hd = 64
    q_off = 2 * a_w
    k_off = 2 * a_w + f_w

    def head_norm(blk, gain):
        sq = blk * blk
        s_lo = jnp.sum(jnp.where(low, sq, 0.0), axis=-1, keepdims=True)
        s_hi = jnp.sum(jnp.where(low, 0.0, sq), axis=-1, keepdims=True)
        r = jnp.where(low, lax.rsqrt(s_lo / hd + EPS), lax.rsqrt(s_hi / hd + EPS))
        return (blk * r) * gain

    n_heads = f_w // hd
    ab3 = jnp.concatenate(list(_split3(a_all)) + list(_split3(b_all)), axis=-1)
    routed = _dot(ab3, route_ref[...])
    l64 = lane & 63
    x_q = jnp.where((l64 >= 3 * n_heads) & (l64 < 6 * n_heads), 1.0, routed[:, :LANES])
    y_all = routed[:, LANES:]

    def k_extras(h):
        return jnp.where((l64 >= 3 * h) & (l64 < 3 * h + 3), 1.0,
                         jnp.where((l64 >= 3 * (n_heads + h)) & (l64 < 3 * (n_heads + h) + 3), y_all, 0.0))

    for j in range(f_w // LANES):
        sl = slice(j * LANES, (j + 1) * LANES)
        qn = head_norm(z[:, q_off + j * LANES:q_off + (j + 1) * LANES], gq_ref[...]) * (LOG2E / math.sqrt(hd))
        kn = head_norm(z[:, k_off + j * LANES:k_off + (j + 1) * LANES], gk_ref[...])
        for hh in range(2):
            hidx = 2 * j + hh
            qhalf = low if hh == 0 else jnp.logical_not(low)
            osl = slice(hidx * LANES, (hidx + 1) * LANES)
            qp_ref[0, :, osl] = jnp.where(qhalf, qn, x_q).astype(BF16)
            kp_ref[0, :, osl] = jnp.where(qhalf, kn, k_extras(hidx)).astype(BF16)

    for hidx in range(n_heads):
        vt_ref[0, hidx, 0:hd, :] = zvt[hidx * hd:(hidx + 1) * hd].astype(BF16)
        vt_ref[0, hidx, hd:2 * hd, :] = jnp.ones((hd, tm), BF16)


def _route_matrix(n_heads):
    assert 6 * n_heads <= 64
    r = np.zeros((6 * LANES, 2 * LANES), np.float32)
    for h in range(n_heads):
        for x in range(3):
            for half in (0, 64):
                r[x * LANES + h, half + 3 * h + x] = 1.0
                r[(3 + x) * LANES + h, LANES + half + 3 * (n_heads + h) + x] = 1.0
    return jnp.asarray(r, BF16)


def _even_in(x, g_mix, w_main, w_vt, w_f, b_f, g_v, w_s, b_s_t, g_q2, g_k2, tri):
    B, S, D = x.shape
    tm = ROW_TILE
    assert tm % FOX_TK == 0 and FOX_TQ % tm == 0
    nsub = tm // FOX_TK
    blocks_per_q = FOX_TQ // FOX_TK
    a_w = g_v.shape[1]
    f_w = w_vt.shape[0]
    n_heads = f_w // 64
    route = _route_matrix(n_heads)
    grid = (B, S // tm)
    row3 = lambda w: pl.BlockSpec((1, tm, w), lambda b, t: (b, t, 0))
    return pl.pallas_call(
        functools.partial(_even_in_kernel, blocks_per_q=blocks_per_q),
        grid=grid,
        in_specs=[row3(D), _const_spec(g_mix.shape), _const_spec(w_main.shape), _const_spec(w_vt.shape),
                  _const_spec(w_f.shape), _const_spec(b_f.shape), _const_spec(g_v.shape), _const_spec(w_s.shape),
                  _const_spec(b_s_t.shape), _const_spec(g_q2.shape), _const_spec(g_k2.shape),
                  _const_spec(tri.shape), _const_spec(route.shape)],
        out_specs=[row3(a_w), row3(n_heads * LANES), row3(n_heads * LANES),
                   pl.BlockSpec((1, n_heads, LANES, tm), lambda b, t: (b, 0, 0, t)),
                   pl.BlockSpec((1, 8, LANES), lambda b, t: (b, (t * nsub) // blocks_per_q, 0)),
                   pl.BlockSpec((1, 8 * nsub, LANES), lambda b, t: (b, t, 0))],
        out_shape=[jax.ShapeDtypeStruct((B, S, a_w), BF16),
                   jax.ShapeDtypeStruct((B, S, n_heads * LANES), BF16),
                   jax.ShapeDtypeStruct((B, S, n_heads * LANES), BF16),
                   jax.ShapeDtypeStruct((B, n_heads, LANES, S), BF16),
                   jax.ShapeDtypeStruct((B, (S // FOX_TQ) * 8, LANES), F32),
                   jax.ShapeDtypeStruct((B, (S // FOX_TK) * 8, LANES), F32)],
        scratch_shapes=[pltpu.VMEM((1, LANES), F32), pltpu.VMEM((1, LANES), F32), pltpu.VMEM((1, LANES), F32)],
        compiler_params=pltpu.CompilerParams(
            dimension_semantics=("arbitrary", "arbitrary"),
            vmem_limit_bytes=_vmem_limit(48 * 1024 * 1024)),
        name="even_in",
    )(x, g_mix, w_main, w_vt, w_f, b_f, g_v, w_s, b_s_t, g_q2, g_k2, tri, route)


def _fox_kernel(cs_ref, ce_ref, qp_ref, kp_ref, vt_ref, o_ref, s_buf, mx_buf, p_buf, al_buf, m_scr, acc_scr,
                *, nq, nk, n_heads):
    b = pl.program_id(0)
    hp = pl.program_id(1)
    tq, tk = FOX_TQ, FOX_TK
    sub = tq // tk
    T = MXU_TILE
    nct = tq // T
    nkt = tk // T
    zslab = jnp.zeros((T, LANES), BF16)

    def qk(i, j, masked, col_lo=0):
        q0 = pl.multiple_of(i * tq, tq)
        k0 = pl.multiple_of(j * tk, tk)
        for c in range(col_lo // T, nct):
            mxu = c % 2
            pltpu.matmul_push_rhs(qp_ref[0, pl.ds(q0 + c * T, T), :], staging_register=0, mxu_index=mxu,
                                  transpose=True)
            first = True
            for hh in range(2):
                for kt in range(nkt):
                    ks = kp_ref[0, pl.ds(k0 + kt * T, T), hh * LANES:(hh + 1) * LANES]
                    lhs = jnp.concatenate([ks, zslab] if hh == 0 else [zslab, ks], axis=1)
                    a_qk = ((hh * nkt + kt) % 2) * (T // 4)
                    pltpu.matmul_acc_lhs(a_qk, lhs, mxu, load_staged_rhs=0 if first else None)
                    first = False
                    st = pltpu.matmul_pop(a_qk, (T, T), F32, mxu)
                    if masked:
                        ri = lax.broadcasted_iota(jnp.int32, (T, T), 0) + kt * T
                        ci = lax.broadcasted_iota(jnp.int32, (T, T), 1) + (c * T - col_lo)
                        st = jnp.where(ci >= ri, st, NEG_BIG)
                    s_buf[hh, kt * T:(kt + 1) * T, c * T:(c + 1) * T] = st
                    cm = jnp.max(st, axis=0, keepdims=True)
                    cmax = cm if kt == 0 else jnp.maximum(cmax, cm)
                mx_buf[hh, :, c * T:(c + 1) * T] = jnp.broadcast_to(cmax, (8, T))

    def ex(i, j, col_lo=0):
        for hh in range(2):
            head = hp * 2 + hh
            d = cs_ref[(b * nq + i) * n_heads + head] - ce_ref[(b * nk + j) * n_heads + head]
            for c in range(col_lo // T, nct):
                cs_ = slice(c * T, (c + 1) * T)
                m_old = m_scr[hh, :, cs_]
                m_new = jnp.maximum(m_old, mx_buf[hh, :, cs_] + d)
                al_buf[hh, :, cs_] = jnp.exp2(m_old - m_new)
                m_scr[hh, :, cs_] = m_new
                shift = jnp.broadcast_to((m_new - d)[0:1], (T, T))
                for kt in range(nkt):
                    rs = slice(kt * T, (kt + 1) * T)
                    p_buf[hh, rs, cs_] = jnp.exp2(s_buf[hh, rs, cs_] - shift).astype(BF16)

    def pv(i, j, col_lo=0):
        k0 = pl.multiple_of(j * tk, tk)
        n = 0
        for hh in range(2):
            for c in range(col_lo // T, nct):
                mxu = c % 2
                cs_ = slice(c * T, (c + 1) * T)
                a_pv = 2 * (T // 4) + (LANES // 4) * ((n // 2) % 4)
                n += 1
                for kt in range(nkt):
                    pltpu.matmul_push_rhs(p_buf[hh, kt * T:(kt + 1) * T, cs_], staging_register=1, mxu_index=mxu)
                    pltpu.matmul_acc_lhs(a_pv, vt_ref[0, hh, :, pl.ds(k0 + kt * T, T)], mxu, load_staged_rhs=1)
                out = pltpu.matmul_pop(a_pv, (LANES, T), F32, mxu)
                acc_scr[hh, :, cs_] = (jnp.broadcast_to(al_buf[hh, 0:1, cs_], (LANES, T)) * acc_scr[hh, :, cs_]
                                       + out)

    def tail(i, n_full, has_full_blocks):
        for t in range(sub + 2):
            for stage, off in ((pv, t - 2), (ex, t - 1)):
                if off >= 0:
                    stage(i, n_full + off, off * tk)
                elif has_full_blocks:
                    stage(i, n_full + off)
            if t < sub:
                qk(i, n_full + t, True, t * tk)

    def q_body(i, carry):
        q0 = pl.multiple_of(i * tq, tq)
        n_full = i * sub
        for hh in range(2):
            m_scr[hh] = jnp.full((8, tq), NEG_BIG, F32)
            acc_scr[hh] = jnp.zeros((LANES, tq), F32)

        @pl.when(i == 0)
        def _():
            tail(i, 0, False)

        @pl.when(i > 0)
        def _():
            assert sub >= 2
            qk(i, 0, False)
            ex(i, 0)
            qk(i, 1, False)

            def body(s, c):
                pv(i, s - 2)
                ex(i, s - 1)
                qk(i, s, False)
                return c

            lax.fori_loop(2, n_full, body, 0)
            tail(i, n_full, True)

        hd = LANES // 2
        tops = []
        for hh in range(2):
            a = acc_scr[hh]
            tops.append(a[0:hd] * (1.0 / a[hd:2 * hd]))
        o_ref[0, pl.ds(q0, tq), :] = jnp.concatenate(tops, axis=0).T.astype(BF16)
        return carry

    lax.fori_loop(0, nq, q_body, 0)


def _fox(cs_flat, ce_flat, qp, kp, vt):
    B, S, HW = qp.shape
    n_heads = HW // LANES
    nq, nk = S // FOX_TQ, S // FOX_TK
    assert FOX_TQ % MXU_TILE == 0 and FOX_TK % MXU_TILE == 0 and 2 * LANES == MXU_TILE
    grid = (B, n_heads // 2)
    slab = pl.BlockSpec((1, S, 2 * LANES), lambda b, h: (b, 0, h))
    stat = pltpu.VMEM((2, 8, FOX_TQ), F32)
    return pl.pallas_call(
        functools.partial(_fox_kernel, nq=nq, nk=nk, n_heads=n_heads),
        grid=grid,
        in_specs=[pl.BlockSpec(memory_space=pltpu.SMEM), pl.BlockSpec(memory_space=pltpu.SMEM),
                  slab, slab, pl.BlockSpec((1, 2, LANES, S), lambda b, h: (b, h, 0, 0))],
        out_specs=pl.BlockSpec((1, S, LANES), lambda b, h: (b, 0, h)),
        out_shape=jax.ShapeDtypeStruct((B, S, n_heads * 64), BF16),
        scratch_shapes=[pltpu.VMEM((2, FOX_TK, FOX_TQ), F32), stat,
                        pltpu.VMEM((2, FOX_TK, FOX_TQ), BF16), stat,
                        stat, pltpu.VMEM((2, LANES, FOX_TQ), F32)],
        compiler_params=pltpu.CompilerParams(
            dimension_semantics=("arbitrary", "arbitrary"),
            vmem_limit_bytes=_vmem_limit(56 * 1024 * 1024)),
        name="fox_attn",
    )(cs_flat, ce_flat, qp, kp, vt)


def _odd_mix(x_tile, t, g_ref, w_ref, wp_ref, sp_ref, cw_ref, zbuf, xbuf, windows):
    tm = x_tile.shape[0]
    pw = sp_ref.shape[1]
    cwid = cw_ref.shape[1]

    @pl.when(t == 0)
    def _():
        zbuf[0:POOL_HALO, :] = jnp.zeros((POOL_HALO, pw), F32)
        xbuf[0:CONV_HALO, :] = jnp.zeros((CONV_HALO, cwid), F32)

    h = _rms(x_tile, g_ref[...]).astype(BF16)
    z = _dot(h, w_ref[...])
    zc = z[:, :pw]
    hdn = z[:, pw:pw + cwid]
    gb = z[:, pw + cwid:pw + 2 * cwid]
    gc = z[:, pw + 2 * cwid:pw + 3 * cwid]

    zbuf[POOL_HALO:POOL_HALO + tm, :] = zc
    pos = t * tm + lax.broadcasted_iota(jnp.int32, (tm, 1), 0)
    yc = []
    for g, w in enumerate(windows):
        sl = slice(g * LANES, (g + 1) * LANES)
        acc = zbuf[POOL_HALO:POOL_HALO + tm, sl]
        for j in range(1, w):
            acc = acc + zbuf[POOL_HALO - j:POOL_HALO - j + tm, sl]
        inv_cnt = 1.0 / jnp.minimum(pos + 1, w).astype(F32)
        p = acc * inv_cnt - zc[:, sl]
        yc.append((_dot(p.astype(BF16), wp_ref[g]) * sp_ref[:, sl]).astype(BF16))
    zbuf[0:POOL_HALO, :] = zbuf[tm:tm + POOL_HALO, :]

    xg = gc * hdn
    xbuf[CONV_HALO:CONV_HALO + tm, :] = xg
    k = cw_ref.shape[0]
    conv = cw_ref[k - 1:k, :] * xg
    for j in range(1, k):
        conv = conv + cw_ref[k - 1 - j:k - j, :] * xbuf[CONV_HALO - j:CONV_HALO - j + tm, :]
    yd = (gb * conv).astype(BF16)
    xbuf[0:CONV_HALO, :] = xbuf[tm:tm + CONV_HALO, :]
    return jnp.concatenate(yc, axis=-1), yd


def _mem_kv_kernel(m_ref, g_ref, w_ref, gk_ref, k_ref, v_ref):
    xa = k_ref.shape[2]
    hm = _rms(m_ref[0], g_ref[...]).astype(BF16)
    kv = _dot(hm, w_ref[...])
    for h in range(xa // LANES):
        sl = slice(h * LANES, (h + 1) * LANES)
        k_ref[0, :, sl] = _rms(kv[:, sl], gk_ref[...]).astype(BF16)
    v_ref[0] = kv[:, xa:].astype(BF16)


def _mem_kv(mem, g_mem, w_kv, g_k):
    B, M, D = mem.shape
    xa = w_kv.shape[1] // 2
    blk = lambda w: pl.BlockSpec((1, M, w), lambda b: (b, 0, 0))
    return pl.pallas_call(
        _mem_kv_kernel,
        grid=(B,),
        in_specs=[blk(D), _const_spec(g_mem.shape), _const_spec(w_kv.shape), _const_spec(g_k.shape)],
        out_specs=[blk(xa), blk(xa)],
        out_shape=[jax.ShapeDtypeStruct((B, M, xa), BF16), jax.ShapeDtypeStruct((B, M, xa), BF16)],
        compiler_params=pltpu.CompilerParams(dimension_semantics=("arbitrary",)),
        name="mem_kv",
    )(mem, g_mem, w_kv, g_k)


def _post_tile(x_tile, ya, yb, woa_ref, wob_ref, gxa_ref, wq_ref, gq_ref, k_ref, v_ref, wo_ref,
               gff_ref, wg_ref, wu_ref, wd_ref, hs_scr):
    xa = wq_ref.shape[1]
    dff = wg_ref.shape[1]
    inv_sqrt = 1.0 / math.sqrt(LANES)
    x1 = x_tile + _dot(ya, woa_ref[...]) + _dot(yb, wob_ref[...])

    hx = _rms(x1, gxa_ref[...]).astype(BF16)
    q = _dot(hx, wq_ref[...])
    outs = []
    for h in range(xa // LANES):
        sl = slice(h * LANES, (h + 1) * LANES)
        qn = _rms(q[:, sl], gq_ref[...]).astype(BF16)
        s = _dot_nt(qn, k_ref[0, :, sl]) * inv_sqrt
        m = jnp.max(s, axis=-1, keepdims=True)
        p = jnp.exp(s - m)
        l = jnp.sum(p, axis=-1, keepdims=True)
        outs.append((_dot(p.astype(BF16), v_ref[0, :, sl]) * (1.0 / l)).astype(BF16))
    x2 = x1 + _dot(jnp.concatenate(outs, axis=-1), wo_ref[...])

    hf = _rms(x2, gff_ref[...]).astype(BF16)
    c0 = 0
    while c0 < dff:
        c1 = min(c0 + FF_CHUNK, dff)
        a = _dot(hf, wg_ref[:, c0:c1])
        u = _dot(hf, wu_ref[:, c0:c1])
        hs_scr[:, c0:c1] = (a * jax.nn.sigmoid(a) * u).astype(BF16)
        c0 = c1
    return x2 + _dot(hs_scr[...], wd_ref[...])


def _post_kernel(x_ref, ya_ref, yb_ref, *rest):
    *w_refs, o_ref, hs_scr = rest
    o_ref[0] = _post_tile(x_ref[0], ya_ref[0], yb_ref[0], *w_refs, hs_scr)


def _odd_layer_kernel(x_ref, gmix_ref, win_ref, wp_ref, sp_ref, cw_ref, *rest, windows):
    *w_refs, o_ref, hs_scr, zbuf, xbuf = rest
    x_tile = x_ref[0]
    yc, yd = _odd_mix(x_tile, pl.program_id(1), gmix_ref, win_ref, wp_ref, sp_ref, cw_ref, zbuf, xbuf, windows)
    o_ref[0] = _post_tile(x_tile, yc, yd, *w_refs, hs_scr)


def _layer_spec(stack, idx):
    return pl.BlockSpec((None,) + stack.shape[1:], lambda b, t: (idx, 0, 0), pipeline_mode=pl.Buffered(1))


def _post_specs(D, w_out, g_xa, w_q, g_q, k_mem, w_o, g_ffn, w_gate, w_up, w_down):
    M, xa = k_mem.shape[1], w_q[0].shape[2]
    wo_stack, wo_idx = w_out
    half = wo_stack.shape[1] // 2
    memb = pl.BlockSpec((1, M, xa), lambda b, t: (b, 0, 0))
    cs = lambda a: _const_spec(a.shape, single=True)
    wo_half = lambda k: pl.BlockSpec((None, half, D), lambda b, t: (wo_idx, k, 0), pipeline_mode=pl.Buffered(1))
    return [wo_half(0), wo_half(1), cs(g_xa), _layer_spec(*w_q), cs(g_q), memb, memb, _layer_spec(*w_o),
            cs(g_ffn), _layer_spec(*w_gate), _layer_spec(*w_up), _layer_spec(*w_down)]


def _post(x, ya, yb, w_out, g_xa, w_q, g_q, k_mem, v_mem, w_o, g_ffn, w_gate, w_up, w_down):
    B, S, D = x.shape
    tm = POST_TILE
    a_w, b_w = ya.shape[2], yb.shape[2]
    assert a_w == b_w and w_out[0].shape[1] == a_w + b_w
    row3 = lambda w: pl.BlockSpec((1, tm, w), lambda b, t: (b, t, 0))
    return pl.pallas_call(
        _post_kernel,
        grid=(B, S // tm),
        in_specs=[row3(D), row3(a_w), row3(b_w)] + _post_specs(D, w_out, g_xa, w_q, g_q, k_mem, w_o,
                                                              g_ffn, w_gate, w_up, w_down),
        out_specs=row3(D),
        out_shape=jax.ShapeDtypeStruct((B, S, D), F32),
        scratch_shapes=[pltpu.VMEM((tm, w_gate[0].shape[2]), BF16)],
        compiler_params=pltpu.CompilerParams(
            dimension_semantics=("arbitrary", "arbitrary"),
            vmem_limit_bytes=_vmem_limit(58 * 1024 * 1024)),
        name="post",
    )(x, ya, yb, w_out[0], w_out[0], g_xa, w_q[0], g_q, k_mem, v_mem, w_o[0], g_ffn,
      w_gate[0], w_up[0], w_down[0])


def _odd_layer(x, g_mix, w_in, w_pool, s_pool, conv_w, windows,
               w_out, g_xa, w_q, g_q, k_mem, v_mem, w_o, g_ffn, w_gate, w_up, w_down):
    B, S, D = x.shape
    tm = POST_TILE
    pw, cwid = s_pool.shape[1], conv_w.shape[1]
    assert max(windows) <= POOL_HALO and conv_w.shape[0] - 1 <= CONV_HALO
    assert pw == cwid and w_out[0].shape[1] == pw + cwid
    row3 = lambda w: pl.BlockSpec((1, tm, w), lambda b, t: (b, t, 0))
    cs = lambda a: _const_spec(a.shape, single=True)
    return pl.pallas_call(
        functools.partial(_odd_layer_kernel, windows=windows),
        grid=(B, S // tm),
        in_specs=[row3(D), cs(g_mix), cs(w_in), cs(w_pool), cs(s_pool), cs(conv_w)]
        + _post_specs(D, w_out, g_xa, w_q, g_q, k_mem, w_o, g_ffn, w_gate, w_up, w_down),
        out_specs=row3(D),
        out_shape=jax.ShapeDtypeStruct((B, S, D), F32),
        scratch_shapes=[pltpu.VMEM((tm, w_gate[0].shape[2]), BF16),
                        pltpu.VMEM((POOL_HALO + tm, pw), F32), pltpu.VMEM((CONV_HALO + tm, cwid), F32)],
        compiler_params=pltpu.CompilerParams(
            dimension_semantics=("arbitrary", "arbitrary"),
            vmem_limit_bytes=_vmem_limit(58 * 1024 * 1024)),
        name="odd_layer",
    )(x, g_mix, w_in, w_pool, s_pool, conv_w,
      w_out[0], w_out[0], g_xa, w_q[0], g_q, k_mem, v_mem, w_o[0], g_ffn, w_gate[0], w_up[0], w_down[0])


def kernel(x, mem, g_mix, g_xa, g_mem, xa_wq, xa_wkv, xa_wo, xa_gq, xa_gk, g_ffn, w_gate, w_up, w_down,
           e_w_in, e_b_f, e_g_v, e_w_s, e_b_s, e_g_qn, e_g_kn, e_w_out,
           o_w_in, o_w_pool, o_s_pool, o_conv_w, o_w_out):
    depth = g_mix.shape[0]
    B, S, D = x.shape
    row = lambda a: a.reshape(1, -1)
    tri = (lax.broadcasted_iota(jnp.int32, (FOX_TK, FOX_TK), 0)
           >= lax.broadcasted_iota(jnp.int32, (FOX_TK, FOX_TK), 1)).astype(BF16)
    pool_windows = (2, 4, 8, 16)[:o_w_pool.shape[1]]
    xa_wq_b, xa_wo_b, w_gate_b, w_up_b, w_down_b, e_w_out_b, o_w_out_b = (
        w.astype(BF16) for w in (xa_wq, xa_wo, w_gate, w_up, w_down, e_w_out, o_w_out))

    for layer in range(depth):
        i = layer // 2
        if layer % 2 == 0:
            n_heads = e_b_f.shape[1]
            a_w = e_g_v.shape[1]
            f_w = n_heads * e_g_qn.shape[1]
            n_uvqk = 2 * a_w + 2 * f_w
            w_main = e_w_in[i].astype(BF16)
            w_vt = w_main[:, n_uvqk:n_uvqk + f_w].T
            w_f = jnp.pad(w_main[:, n_uvqk + f_w:], ((0, 0), (0, LANES - n_heads)))
            b_f = jnp.pad(e_b_f[i], (0, LANES - n_heads)).reshape(1, LANES)
            g_q2 = jnp.tile(e_g_qn[i], 2).reshape(1, LANES)
            g_k2 = jnp.tile(e_g_kn[i], 2).reshape(1, LANES)
            ya, qp, kp, vt, cs, ce = _even_in(x, row(g_mix[layer]), w_main, w_vt, w_f, b_f, row(e_g_v[i]),
                                              e_w_s[i], e_b_s[i].T, g_q2, g_k2, tri)
            yb = _fox(cs[:, ::8, :n_heads].reshape(-1), ce[:, ::8, :n_heads].reshape(-1), qp, kp, vt)
        k_mem, v_mem = _mem_kv(mem, row(g_mem[layer]), xa_wkv[layer].astype(BF16), row(xa_gk[layer]))
        post_args = (row(g_xa[layer]), (xa_wq_b, layer), row(xa_gq[layer]), k_mem, v_mem,
                     (xa_wo_b, layer), row(g_ffn[layer]),
                     (w_gate_b, layer), (w_up_b, layer), (w_down_b, layer))
        if layer % 2 == 0:
            x = _post(x, ya, yb, (e_w_out_b, i), *post_args)
        else:
            x = _odd_layer(x, row(g_mix[layer]), o_w_in[i].astype(BF16), o_w_pool[i].astype(BF16),
                           row(o_s_pool[i]), o_conv_w[i], pool_windows, (o_w_out_b, i), *post_args)
    return x
```

```python
import functools
import math

import jax
import jax.numpy as jnp
import numpy as np
from jax import lax
from jax.experimental import pallas as pl
from jax.experimental.pallas import tpu as pltpu

F32 = jnp.float32
BF16 = jnp.bfloat16
EPS = 1e-6
LOG2E = 1.4426950408889634
NEG_BIG = -1e30

LANES = 128
MXU_TILE = 256
V7X_VMEM_BYTES = 64 * 1024 * 1024

ROW_TILE = 1024
POST_TILE = 1024
FOX_TQ = 2048
FOX_TK = 512
POOL_HALO = 16
CONV_HALO = 8
FF_CHUNK = 256


def _vmem_limit(nbytes):
    return int(min(nbytes, V7X_VMEM_BYTES - 4 * 1024 * 1024))


def _rms(x, g):
    ms = jnp.mean(x * x, axis=-1, keepdims=True)
    return (x * lax.rsqrt(ms + EPS)) * g


def _dot(a, b):
    return jnp.dot(a, b, preferred_element_type=F32)


def _dot_nt(a, b):
    return lax.dot_general(a, b, (((1,), (1,)), ((), ())), preferred_element_type=F32)


def _split3(x):
    hi = x.astype(BF16)
    r1 = x - hi.astype(F32)
    mid = r1.astype(BF16)
    lo = (r1 - mid.astype(F32)).astype(BF16)
    return hi, mid, lo


def _const_spec(shape, single=False):
    nd = len(shape)
    kw = {}
    if single:
        kw["pipeline_mode"] = pl.Buffered(1)
    return pl.BlockSpec(shape, lambda *_: (0,) * nd, **kw)


def _even_in_kernel(x_ref, g_ref, w_ref, wvt_ref, wf_ref, bf_ref, gv_ref, ws_ref, bst_ref, gq_ref, gk_ref,
                    tri_ref, route_ref, ya_ref, qp_ref, kp_ref, vt_ref, cs_ref, ce_ref, run_ref, aoff_ref, cq0_ref,
                    *, blocks_per_q):
    t = pl.program_id(1)
    tm = x_ref.shape[1]
    a_w = ya_ref.shape[2]
    f_w = qp_ref.shape[2] // 2
    n_grp = a_w // LANES

    @pl.when(t == 0)
    def _():
        run_ref[...] = jnp.zeros_like(run_ref)

    h = _rms(x_ref[0], g_ref[...]).astype(BF16)
    z = _dot(h, w_ref[:, 0:2 * a_w + 2 * f_w])
    zvt = _dot_nt(wvt_ref[...], h)
    fl = _dot(h, wf_ref[...]) + bf_ref[...]
    logf = -(jnp.maximum(-fl, 0.0) + jnp.log1p(jnp.exp(-jnp.abs(fl)))) * LOG2E

    uv = jax.nn.gelu(z[:, :2 * a_w])
    row = lax.broadcasted_iota(jnp.int32, (LANES, LANES), 0) // 64
    col = lax.broadcasted_iota(jnp.int32, (LANES, LANES), 1) // 64
    chunk_mask = row >= col
    for g in range(n_grp):
        sl = slice(g * LANES, (g + 1) * LANES)
        vg = uv[:, a_w + g * LANES:a_w + (g + 1) * LANES]
        vn = _rms(vg, gv_ref[:, sl]).astype(BF16)
        wm = jnp.where(chunk_mask, ws_ref[g], 0.0).astype(BF16)
        bias = bst_ref[:, g:g + 1]
        for n in range(tm // LANES):
            rs = slice(n * LANES, (n + 1) * LANES)
            s = _dot(wm, vn[rs]) + bias
            ya_ref[0, rs, sl] = (uv[rs, sl] * s).astype(BF16)

    tri = tri_ref[...]
    nsub = tm // FOX_TK
    run = run_ref[...]
    a_off = aoff_ref[...]
    c_q0 = cq0_ref[...]
    a_parts, b_parts = [], []
    for r in range(nsub):
        lf = logf[r * FOX_TK:(r + 1) * FOX_TK]
        hi, mid, lo = _split3(lf)
        lc = _dot(tri, hi) + _dot(tri, mid) + _dot(tri, lo)
        first = lf[0:1]
        tot = lc[FOX_TK - 1:FOX_TK]
        q_start = ((t * nsub + r) % blocks_per_q) == 0
        a_off = jnp.where(q_start, -first, a_off)
        c_q0 = jnp.where(q_start, run + first, c_q0)
        a_parts.append(a_off + lc)
        b_parts.append(tot - lc)
        ce_ref[0, r * 8:(r + 1) * 8] = jnp.broadcast_to(run + tot, (8, LANES))
        a_off = a_off + tot
        run = run + tot
    a_all = jnp.concatenate(a_parts, axis=0)
    b_all = jnp.concatenate(b_parts, axis=0)
    cs_ref[0] = jnp.broadcast_to(c_q0, (8, LANES))
    aoff_ref[...] = a_off
    cq0_ref[...] = c_q0
    run_ref[...] = run

    lane = lax.broadcasted_iota(jnp.int32, (tm, LANES), 1)
    low = lane < 64
    hd = 64
    q_off = 2 * a_w
    k_off = 2 * a_w + f_w

    def head_norm(blk, gain):
        sq = blk * blk
        s_lo = jnp.sum(jnp.where(low, sq, 0.0), axis=-1, keepdims=True)
        s_hi = jnp.sum(jnp.where(low, 0.0, sq), axis=-1, keepdims=True)
        r = jnp.where(low, lax.rsqrt(s_lo / hd + EPS), lax.rsqrt(s_hi / hd + EPS))
        return (blk * r) * gain

    n_heads = f_w // hd
    ab3 = jnp.concatenate(list(_split3(a_all)) + list(_split3(b_all)), axis=-1)
    routed = _dot(ab3, route_ref[...])
    l64 = lane & 63
    x_q = jnp.where((l64 >= 3 * n_heads) & (l64 < 6 * n_heads), 1.0, routed[:, :LANES])
    y_all = routed[:, LANES:]

    def k_extras(h):
        return jnp.where((l64 >= 3 * h) & (l64 < 3 * h + 3), 1.0,
                         jnp.where((l64 >= 3 * (n_heads + h)) & (l64 < 3 * (n_heads + h) + 3), y_all, 0.0))

    for j in range(f_w // LANES):
        sl = slice(j * LANES, (j + 1) * LANES)
        qn = head_norm(z[:, q_off + j * LANES:q_off + (j + 1) * LANES], gq_ref[...]) * (LOG2E / math.sqrt(hd))
        kn = head_norm(z[:, k_off + j * LANES:k_off + (j + 1) * LANES], gk_ref[...])
        for hh in range(2):
            hidx = 2 * j + hh
            qhalf = low if hh == 0 else jnp.logical_not(low)
            osl = slice(hidx * LANES, (hidx + 1) * LANES)
            qp_ref[0, :, osl] = jnp.where(qhalf, qn, x_q).astype(BF16)
            kp_ref[0, :, osl] = jnp.where(qhalf, kn, k_extras(hidx)).astype(BF16)

    for hidx in range(n_heads):
        vt_ref[0, hidx, 0:hd, :] = zvt[hidx * hd:(hidx + 1) * hd].astype(BF16)
        vt_ref[0, hidx, hd:2 * hd, :] = jnp.ones((hd, tm), BF16)


def _route_matrix(n_heads):
    assert 6 * n_heads <= 64
    r = np.zeros((6 * LANES, 2 * LANES), np.float32)
    for h in range(n_heads):
        for x in range(3):
            for half in (0, 64):
                r[x * LANES + h, half + 3 * h + x] = 1.0
                r[(3 + x) * LANES + h, LANES + half + 3 * (n_heads + h) + x] = 1.0
    return jnp.asarray(r, BF16)


def _even_in(x, g_mix, w_main, w_vt, w_f, b_f, g_v, w_s, b_s_t, g_q2, g_k2, tri):
    B, S, D = x.shape
    tm = ROW_TILE
    assert tm % FOX_TK == 0 and FOX_TQ % tm == 0
    nsub = tm // FOX_TK
    blocks_per_q = FOX_TQ // FOX_TK
    a_w = g_v.shape[1]
    f_w = w_vt.shape[0]
    n_heads = f_w // 64
    route = _route_matrix(n_heads)
    grid = (B, S // tm)
    row3 = lambda w: pl.BlockSpec((1, tm, w), lambda b, t: (b, t, 0))
    return pl.pallas_call(
        functools.partial(_even_in_kernel, blocks_per_q=blocks_per_q),
        grid=grid,
        in_specs=[row3(D), _const_spec(g_mix.shape), _const_spec(w_main.shape), _const_spec(w_vt.shape),
                  _const_spec(w_f.shape), _const_spec(b_f.shape), _const_spec(g_v.shape), _const_spec(w_s.shape),
                  _const_spec(b_s_t.shape), _const_spec(g_q2.shape), _const_spec(g_k2.shape),
                  _const_spec(tri.shape), _const_spec(route.shape)],
        out_specs=[row3(a_w), row3(n_heads * LANES), row3(n_heads * LANES),
                   pl.BlockSpec((1, n_heads, LANES, tm), lambda b, t: (b, 0, 0, t)),
                   pl.BlockSpec((1, 8, LANES), lambda b, t: (b, (t * nsub) // blocks_per_q, 0)),
                   pl.BlockSpec((1, 8 * nsub, LANES), lambda b, t: (b, t, 0))],
        out_shape=[jax.ShapeDtypeStruct((B, S, a_w), BF16),
                   jax.ShapeDtypeStruct((B, S, n_heads * LANES), BF16),
                   jax.ShapeDtypeStruct((B, S, n_heads * LANES), BF16),
                   jax.ShapeDtypeStruct((B, n_heads, LANES, S), BF16),
                   jax.ShapeDtypeStruct((B, (S // FOX_TQ) * 8, LANES), F32),
                   jax.ShapeDtypeStruct((B, (S // FOX_TK) * 8, LANES), F32)],
        scratch_shapes=[pltpu.VMEM((1, LANES), F32), pltpu.VMEM((1, LANES), F32), pltpu.VMEM((1, LANES), F32)],
        compiler_params=pltpu.CompilerParams(
            dimension_semantics=("arbitrary", "arbitrary"),
            vmem_limit_bytes=_vmem_limit(48 * 1024 * 1024)),
        name="even_in",
    )(x, g_mix, w_main, w_vt, w_f, b_f, g_v, w_s, b_s_t, g_q2, g_k2, tri, route)


def _fox_kernel(cs_ref, ce_ref, qp_ref, kp_ref, vt_ref, o_ref, s_buf, mx_buf, p_buf, al_buf, m_scr, acc_scr,
                *, nq, nk, n_heads):
    b = pl.program_id(0)
    hp = pl.program_id(1)
    tq, tk = FOX_TQ, FOX_TK
    sub = tq // tk
    T = MXU_TILE
    nct = tq // T
    nkt = tk // T
    zslab = jnp.zeros((T, LANES), BF16)

    def qk(i, j, masked, col_lo=0):
        q0 = pl.multiple_of(i * tq, tq)
        k0 = pl.multiple_of(j * tk, tk)
        for c in range(col_lo // T, nct):
            mxu = c % 2
            pltpu.matmul_push_rhs(qp_ref[0, pl.ds(q0 + c * T, T), :], staging_register=0, mxu_index=mxu,
                                  transpose=True)
            first = True
            for hh in range(2):
                for kt in range(nkt):
                    ks = kp_ref[0, pl.ds(k0 + kt * T, T), hh * LANES:(hh + 1) * LANES]
                    lhs = jnp.concatenate([ks, zslab] if hh == 0 else [zslab, ks], axis=1)
                    a_qk = ((hh * nkt + kt) % 2) * (T // 4)
                    pltpu.matmul_acc_lhs(a_qk, lhs, mxu, load_staged_rhs=0 if first else None)
                    first = False
                    st = pltpu.matmul_pop(a_qk, (T, T), F32, mxu)
                    if masked:
                        ri = lax.broadcasted_iota(jnp.int32, (T, T), 0) + kt * T
                        ci = lax.broadcasted_iota(jnp.int32, (T, T), 1) + (c * T - col_lo)
                        st = jnp.where(ci >= ri, st, NEG_BIG)
                    s_buf[hh, kt * T:(kt + 1) * T, c * T:(c + 1) * T] = st
                    cm = jnp.max(st, axis=0, keepdims=True)
                    cmax = cm if kt == 0 else jnp.maximum(cmax, cm)
                mx_buf[hh, :, c * T:(c + 1) * T] = jnp.broadcast_to(cmax, (8, T))

    def ex(i, j, col_lo=0):
        for hh in range(2):
            head = hp * 2 + hh
            d = cs_ref[(b * nq + i) * n_heads + head] - ce_ref[(b * nk + j) * n_heads + head]
            for c in range(col_lo // T, nct):
                cs_ = slice(c * T, (c + 1) * T)
                m_old = m_scr[hh, :, cs_]
                m_new = jnp.maximum(m_old, mx_buf[hh, :, cs_] + d)
                al_buf[hh, :, cs_] = jnp.exp2(m_old - m_new)
                m_scr[hh, :, cs_] = m_new
                shift = jnp.broadcast_to((m_new - d)[0:1], (T, T))
                for kt in range(nkt):
                    rs = slice(kt * T, (kt + 1) * T)
                    p_buf[hh, rs, cs_] = jnp.exp2(s_buf[hh, rs, cs_] - shift).astype(BF16)

    def pv(i, j, col_lo=0):
        k0 = pl.multiple_of(j * tk, tk)
        n = 0
        for hh in range(2):
            for c in range(col_lo // T, nct):
                mxu = c % 2
                cs_ = slice(c * T, (c + 1) * T)
                a_pv = 2 * (T // 4) + (LANES // 4) * ((n // 2) % 4)
                n += 1
                for kt in range(nkt):
                    pltpu.matmul_push_rhs(p_buf[hh, kt * T:(kt + 1) * T, cs_], staging_register=1, mxu_index=mxu)
                    pltpu.matmul_acc_lhs(a_pv, vt_ref[0, hh, :, pl.ds(k0 + kt * T, T)], mxu, load_staged_rhs=1)
                out = pltpu.matmul_pop(a_pv, (LANES, T), F32, mxu)
                acc_scr[hh, :, cs_] = (jnp.broadcast_to(al_buf[hh, 0:1, cs_], (LANES, T)) * acc_scr[hh, :, cs_]
                                       + out)

    def tail(i, n_full, has_full_blocks):
        for t in range(sub + 2):
            for stage, off in ((pv, t - 2), (ex, t - 1)):
                if off >= 0:
                    stage(i, n_full + off, off * tk)
                elif has_full_blocks:
                    stage(i, n_full + off)
            if t < sub:
                qk(i, n_full + t, True, t * tk)

    def q_body(i, carry):
        q0 = pl.multiple_of(i * tq, tq)
        n_full = i * sub
        for hh in range(2):
            m_scr[hh] = jnp.full((8, tq), NEG_BIG, F32)
            acc_scr[hh] = jnp.zeros((LANES, tq), F32)

        @pl.when(i == 0)
        def _():
            tail(i, 0, False)

        @pl.when(i > 0)
        def _():
            assert sub >= 2
            qk(i, 0, False)
            ex(i, 0)
            qk(i, 1, False)

            def body(s, c):
                pv(i, s - 2)
                ex(i, s - 1)
                qk(i, s, False)
                return c

            lax.fori_loop(2, n_full, body, 0)
            tail(i, n_full, True)

        hd = LANES // 2
        tops = []
        for hh in range(2):
            a = acc_scr[hh]
            tops.append(a[0:hd] * (1.0 / a[hd:2 * hd]))
        o_ref[0, pl.ds(q0, tq), :] = jnp.concatenate(tops, axis=0).T.astype(BF16)
        return carry

    lax.fori_loop(0, nq, q_body, 0)


def _fox(cs_flat, ce_flat, qp, kp, vt):
    B, S, HW = qp.shape
    n_heads = HW // LANES
    nq, nk = S // FOX_TQ, S // FOX_TK
    assert FOX_TQ % MXU_TILE == 0 and FOX_TK % MXU_TILE == 0 and 2 * LANES == MXU_TILE
    grid = (B, n_heads // 2)
    slab = pl.BlockSpec((1, S, 2 * LANES), lambda b, h: (b, 0, h))
    stat = pltpu.VMEM((2, 8, FOX_TQ), F32)
    return pl.pallas_call(
        functools.partial(_fox_kernel, nq=nq, nk=nk, n_heads=n_heads),
        grid=grid,
        in_specs=[pl.BlockSpec(memory_space=pltpu.SMEM), pl.BlockSpec(memory_space=pltpu.SMEM),
                  slab, slab, pl.BlockSpec((1, 2, LANES, S), lambda b, h: (b, h, 0, 0))],
        out_specs=pl.BlockSpec((1, S, LANES), lambda b, h: (b, 0, h)),
        out_shape=jax.ShapeDtypeStruct((B, S, n_heads * 64), BF16),
        scratch_shapes=[pltpu.VMEM((2, FOX_TK, FOX_TQ), F32), stat,
                        pltpu.VMEM((2, FOX_TK, FOX_TQ), BF16), stat,
                        stat, pltpu.VMEM((2, LANES, FOX_TQ), F32)],
        compiler_params=pltpu.CompilerParams(
            dimension_semantics=("arbitrary", "arbitrary"),
            vmem_limit_bytes=_vmem_limit(56 * 1024 * 1024)),
        name="fox_attn",
    )(cs_flat, ce_flat, qp, kp, vt)


def _odd_mix(x_tile, t, g_ref, w_ref, wp_ref, sp_ref, cw_ref, zbuf, xbuf, windows):
    tm = x_tile.shape[0]
    pw = sp_ref.shape[1]
    cwid = cw_ref.shape[1]

    @pl.when(t == 0)
    def _():
        zbuf[0:POOL_HALO, :] = jnp.zeros((POOL_HALO, pw), F32)
        xbuf[0:CONV_HALO, :] = jnp.zeros((CONV_HALO, cwid), F32)

    h = _rms(x_tile, g_ref[...]).astype(BF16)
    z = _dot(h, w_ref[...])
    zc = z[:, :pw]
    hdn = z[:, pw:pw + cwid]
    gb = z[:, pw + cwid:pw + 2 * cwid]
    gc = z[:, pw + 2 * cwid:pw + 3 * cwid]

    zbuf[POOL_HALO:POOL_HALO + tm, :] = zc
    pos = t * tm + lax.broadcasted_iota(jnp.int32, (tm, 1), 0)
    yc = []
    for g, w in enumerate(windows):
        sl = slice(g * LANES, (g + 1) * LANES)
        assert w & (w - 1) == 0
        acc = zbuf[:, sl]
        k = 1
        while k < w:
            acc = acc + pltpu.roll(acc, k, axis=0)
            k *= 2
        acc = acc[POOL_HALO:POOL_HALO + tm]
        inv_cnt = 1.0 / jnp.minimum(pos + 1, w).astype(F32)
        p = acc * inv_cnt - zc[:, sl]
        yc.append((_dot(p.astype(BF16), wp_ref[g]) * sp_ref[:, sl]).astype(BF16))
    zbuf[0:POOL_HALO, :] = zbuf[tm:tm + POOL_HALO, :]

    xg = gc * hdn
    xbuf[CONV_HALO:CONV_HALO + tm, :] = xg
    k = cw_ref.shape[0]
    conv = cw_ref[k - 1:k, :] * xg
    for j in range(1, k):
        conv = conv + cw_ref[k - 1 - j:k - j, :] * xbuf[CONV_HALO - j:CONV_HALO - j + tm, :]
    yd = (gb * conv).astype(BF16)
    xbuf[0:CONV_HALO, :] = xbuf[tm:tm + CONV_HALO, :]
    return jnp.concatenate(yc, axis=-1), yd


def _mem_kv_kernel(m_ref, g_ref, w_ref, gk_ref, k_ref, v_ref):
    xa = k_ref.shape[2]
    hm = _rms(m_ref[0], g_ref[...]).astype(BF16)
    kv = _dot(hm, w_ref[...])
    for h in range(xa // LANES):
        sl = slice(h * LANES, (h + 1) * LANES)
        k_ref[0, :, sl] = _rms(kv[:, sl], gk_ref[...]).astype(BF16)
    v_ref[0] = kv[:, xa:].astype(BF16)


def _mem_kv(mem, g_mem, w_kv, g_k):
    B, M, D = mem.shape
    xa = w_kv.shape[1] // 2
    blk = lambda w: pl.BlockSpec((1, M, w), lambda b: (b, 0, 0))
    return pl.pallas_call(
        _mem_kv_kernel,
        grid=(B,),
        in_specs=[blk(D), _const_spec(g_mem.shape), _const_spec(w_kv.shape), _const_spec(g_k.shape)],
        out_specs=[blk(xa), blk(xa)],
        out_shape=[jax.ShapeDtypeStruct((B, M, xa), BF16), jax.ShapeDtypeStruct((B, M, xa), BF16)],
        compiler_params=pltpu.CompilerParams(dimension_semantics=("arbitrary",)),
        name="mem_kv",
    )(mem, g_mem, w_kv, g_k)


def _post_tile(x_tile, ya, yb, woa_ref, wob_ref, gxa_ref, wq_ref, gq_ref, k_ref, v_ref, wo_ref,
               gff_ref, wg_ref, wu_ref, wd_ref, hs_scr):
    xa = wq_ref.shape[1]
    dff = wg_ref.shape[1]
    inv_sqrt = 1.0 / math.sqrt(LANES)
    x1 = x_tile + _dot(ya, woa_ref[...]) + _dot(yb, wob_ref[...])

    hx = _rms(x1, gxa_ref[...]).astype(BF16)
    q = _dot(hx, wq_ref[...])
    outs = []
    for h in range(xa // LANES):
        sl = slice(h * LANES, (h + 1) * LANES)
        qn = _rms(q[:, sl], gq_ref[...]).astype(BF16)
        s = _dot_nt(qn, k_ref[0, :, sl]) * inv_sqrt
        m = jnp.max(s, axis=-1, keepdims=True)
        p = jnp.exp(s - m)
        l = jnp.sum(p, axis=-1, keepdims=True)
        outs.append((_dot(p.astype(BF16), v_ref[0, :, sl]) * (1.0 / l)).astype(BF16))
    x2 = x1 + _dot(jnp.concatenate(outs, axis=-1), wo_ref[...])

    hf = _rms(x2, gff_ref[...]).astype(BF16)
    c0 = 0
    while c0 < dff:
        c1 = min(c0 + FF_CHUNK, dff)
        a = _dot(hf, wg_ref[:, c0:c1])
        u = _dot(hf, wu_ref[:, c0:c1])
        hs_scr[:, c0:c1] = (a * jax.nn.sigmoid(a) * u).astype(BF16)
        c0 = c1
    return x2 + _dot(hs_scr[...], wd_ref[...])


def _post_kernel(x_ref, ya_ref, yb_ref, *rest):
    *w_refs, o_ref, hs_scr = rest
    o_ref[0] = _post_tile(x_ref[0], ya_ref[0], yb_ref[0], *w_refs, hs_scr)


def _odd_layer_kernel(x_ref, gmix_ref, win_ref, wp_ref, sp_ref, cw_ref, *rest, windows):
    *w_refs, o_ref, hs_scr, zbuf, xbuf = rest
    x_tile = x_ref[0]
    yc, yd = _odd_mix(x_tile, pl.program_id(1), gmix_ref, win_ref, wp_ref, sp_ref, cw_ref, zbuf, xbuf, windows)
    o_ref[0] = _post_tile(x_tile, yc, yd, *w_refs, hs_scr)


def _layer_spec(stack, idx):
    return pl.BlockSpec((None,) + stack.shape[1:], lambda b, t: (idx, 0, 0), pipeline_mode=pl.Buffered(1))


def _post_specs(D, w_out, g_xa, w_q, g_q, k_mem, w_o, g_ffn, w_gate, w_up, w_down):
    M, xa = k_mem.shape[1], w_q[0].shape[2]
    wo_stack, wo_idx = w_out
    half = wo_stack.shape[1] // 2
    memb = pl.BlockSpec((1, M, xa), lambda b, t: (b, 0, 0))
    cs = lambda a: _const_spec(a.shape, single=True)
    wo_half = lambda k: pl.BlockSpec((None, half, D), lambda b, t: (wo_idx, k, 0), pipeline_mode=pl.Buffered(1))
    return [wo_half(0), wo_half(1), cs(g_xa), _layer_spec(*w_q), cs(g_q), memb, memb, _layer_spec(*w_o),
            cs(g_ffn), _layer_spec(*w_gate), _layer_spec(*w_up), _layer_spec(*w_down)]


def _post(x, ya, yb, w_out, g_xa, w_q, g_q, k_mem, v_mem, w_o, g_ffn, w_gate, w_up, w_down):
    B, S, D = x.shape
    tm = POST_TILE
    a_w, b_w = ya.shape[2], yb.shape[2]
    assert a_w == b_w and w_out[0].shape[1] == a_w + b_w
    row3 = lambda w: pl.BlockSpec((1, tm, w), lambda b, t: (b, t, 0))
    return pl.pallas_call(
        _post_kernel,
        grid=(B, S // tm),
        in_specs=[row3(D), row3(a_w), row3(b_w)] + _post_specs(D, w_out, g_xa, w_q, g_q, k_mem, w_o,
                                                              g_ffn, w_gate, w_up, w_down),
        out_specs=row3(D),
        out_shape=jax.ShapeDtypeStruct((B, S, D), F32),
        scratch_shapes=[pltpu.VMEM((tm, w_gate[0].shape[2]), BF16)],
        compiler_params=pltpu.CompilerParams(
            dimension_semantics=("arbitrary", "arbitrary"),
            vmem_limit_bytes=_vmem_limit(58 * 1024 * 1024)),
        name="post",
    )(x, ya, yb, w_out[0], w_out[0], g_xa, w_q[0], g_q, k_mem, v_mem, w_o[0], g_ffn,
      w_gate[0], w_up[0], w_down[0])


def _odd_layer(x, g_mix, w_in, w_pool, s_pool, conv_w, windows,
               w_out, g_xa, w_q, g_q, k_mem, v_mem, w_o, g_ffn, w_gate, w_up, w_down):
    B, S, D = x.shape
    tm = POST_TILE
    pw, cwid = s_pool.shape[1], conv_w.shape[1]
    assert max(windows) <= POOL_HALO and conv_w.shape[0] - 1 <= CONV_HALO
    assert pw == cwid and w_out[0].shape[1] == pw + cwid
    row3 = lambda w: pl.BlockSpec((1, tm, w), lambda b, t: (b, t, 0))
    cs = lambda a: _const_spec(a.shape, single=True)
    return pl.pallas_call(
        functools.partial(_odd_layer_kernel, windows=windows),
        grid=(B, S // tm),
        in_specs=[row3(D), cs(g_mix), cs(w_in), cs(w_pool), cs(s_pool), cs(conv_w)]
        + _post_specs(D, w_out, g_xa, w_q, g_q, k_mem, w_o, g_ffn, w_gate, w_up, w_down),
        out_specs=row3(D),
        out_shape=jax.ShapeDtypeStruct((B, S, D), F32),
        scratch_shapes=[pltpu.VMEM((tm, w_gate[0].shape[2]), BF16),
                        pltpu.VMEM((POOL_HALO + tm, pw), F32), pltpu.VMEM((CONV_HALO + tm, cwid), F32)],
        compiler_params=pltpu.CompilerParams(
            dimension_semantics=("arbitrary", "arbitrary"),
            vmem_limit_bytes=_vmem_limit(58 * 1024 * 1024)),
        name="odd_layer",
    )(x, g_mix, w_in, w_pool, s_pool, conv_w,
      w_out[0], w_out[0], g_xa, w_q[0], g_q, k_mem, v_mem, w_o[0], g_ffn, w_gate[0], w_up[0], w_down[0])


def kernel(x, mem, g_mix, g_xa, g_mem, xa_wq, xa_wkv, xa_wo, xa_gq, xa_gk, g_ffn, w_gate, w_up, w_down,
           e_w_in, e_b_f, e_g_v, e_w_s, e_b_s, e_g_qn, e_g_kn, e_w_out,
           o_w_in, o_w_pool, o_s_pool, o_conv_w, o_w_out):
    depth = g_mix.shape[0]
    S = x.shape[1]
    assert S % FOX_TQ == 0 and S % POST_TILE == 0 and S % ROW_TILE == 0
    row = lambda a: a.reshape(1, -1)
    tri = (lax.broadcasted_iota(jnp.int32, (FOX_TK, FOX_TK), 0)
           >= lax.broadcasted_iota(jnp.int32, (FOX_TK, FOX_TK), 1)).astype(BF16)
    pool_windows = (2, 4, 8, 16)[:o_w_pool.shape[1]]
    xa_wq_b, xa_wo_b, w_gate_b, w_up_b, w_down_b, e_w_out_b, o_w_out_b = (
        w.astype(BF16) for w in (xa_wq, xa_wo, w_gate, w_up, w_down, e_w_out, o_w_out))

    for layer in range(depth):
        i = layer // 2
        if layer % 2 == 0:
            n_heads = e_b_f.shape[1]
            a_w = e_g_v.shape[1]
            f_w = n_heads * e_g_qn.shape[1]
            n_uvqk = 2 * a_w + 2 * f_w
            w_main = e_w_in[i].astype(BF16)
            w_vt = w_main[:, n_uvqk:n_uvqk + f_w].T
            w_f = jnp.pad(w_main[:, n_uvqk + f_w:], ((0, 0), (0, LANES - n_heads)))
            b_f = jnp.pad(e_b_f[i], (0, LANES - n_heads)).reshape(1, LANES)
            g_q2 = jnp.tile(e_g_qn[i], 2).reshape(1, LANES)
            g_k2 = jnp.tile(e_g_kn[i], 2).reshape(1, LANES)
            ya, qp, kp, vt, cs, ce = _even_in(x, row(g_mix[layer]), w_main, w_vt, w_f, b_f, row(e_g_v[i]),
                                              e_w_s[i], e_b_s[i].T, g_q2, g_k2, tri)
            yb = _fox(cs[:, ::8, :n_heads].reshape(-1), ce[:, ::8, :n_heads].reshape(-1), qp, kp, vt)
        k_mem, v_mem = _mem_kv(mem, row(g_mem[layer]), xa_wkv[layer].astype(BF16), row(xa_gk[layer]))
        post_args = (row(g_xa[layer]), (xa_wq_b, layer), row(xa_gq[layer]), k_mem, v_mem,
                     (xa_wo_b, layer), row(g_ffn[layer]),
                     (w_gate_b, layer), (w_up_b, layer), (w_down_b, layer))
        if layer % 2 == 0:
            x = _post(x, ya, yb, (e_w_out_b, i), *post_args)
        else:
            x = _odd_layer(x, row(g_mix[layer]), o_w_in[i].astype(BF16), o_w_pool[i].astype(BF16),
                           row(o_s_pool[i]), o_conv_w[i], pool_windows, (o_w_out_b, i), *post_args)
    return x
```

```python
import functools
import math

import jax
import jax.numpy as jnp
import numpy as np
from jax import lax
from jax.experimental import pallas as pl
from jax.experimental.pallas import tpu as pltpu

F32 = jnp.float32
BF16 = jnp.bfloat16
EPS = 1e-6
LOG2E = 1.4426950408889634
NEG_BIG = -1e30

LANES = 128
MXU_TILE = 256
V7X_VMEM_BYTES = 64 * 1024 * 1024

ROW_TILE = 1024
POST_TILE = 1024
FOX_TQ = 2048
FOX_TK = 512
POOL_HALO = 16
CONV_HALO = 8
FF_CHUNK = 256


def _vmem_limit(nbytes):
    return int(min(nbytes, V7X_VMEM_BYTES - 4 * 1024 * 1024))


def _rms(x, g):
    ms = jnp.mean(x * x, axis=-1, keepdims=True)
    return (x * lax.rsqrt(ms + EPS)) * g


def _dot(a, b):
    return jnp.dot(a, b, preferred_element_type=F32)


def _dot_nt(a, b):
    return lax.dot_general(a, b, (((1,), (1,)), ((), ())), preferred_element_type=F32)


def _split3(x):
    hi = x.astype(BF16)
    r1 = x - hi.astype(F32)
    mid = r1.astype(BF16)
    lo = (r1 - mid.astype(F32)).astype(BF16)
    return hi, mid, lo


def _const_spec(shape, single=False):
    nd = len(shape)
    kw = {}
    if single:
        kw["pipeline_mode"] = pl.Buffered(1)
    return pl.BlockSpec(shape, lambda *_: (0,) * nd, **kw)


def _even_in_kernel(x_ref, g_ref, w_ref, wvt_ref, wf_ref, bf_ref, gv_ref, ws_ref, bst_ref, gq_ref, gk_ref,
                    tri_ref, route_ref, ya_ref, qp_ref, kp_ref, vt_ref, cs_ref, ce_ref, run_ref, aoff_ref, cq0_ref,
                    *, blocks_per_q):
    t = pl.program_id(1)
    tm = x_ref.shape[1]
    a_w = ya_ref.shape[2]
    f_w = qp_ref.shape[2] // 2
    n_grp = a_w // LANES

    @pl.when(t == 0)
    def _():
        run_ref[...] = jnp.zeros_like(run_ref)

    h = _rms(x_ref[0], g_ref[...]).astype(BF16)
    z = _dot(h, w_ref[:, 0:2 * a_w + 2 * f_w])
    zvt = _dot_nt(wvt_ref[...], h)
    fl = _dot(h, wf_ref[...]) + bf_ref[...]
    logf = -(jnp.maximum(-fl, 0.0) + jnp.log1p(jnp.exp(-jnp.abs(fl)))) * LOG2E

    uv = jax.nn.gelu(z[:, :2 * a_w])
    row = lax.broadcasted_iota(jnp.int32, (LANES, LANES), 0) // 64
    col = lax.broadcasted_iota(jnp.int32, (LANES, LANES), 1) // 64
    chunk_mask = row >= col
    for g in range(n_grp):
        sl = slice(g * LANES, (g + 1) * LANES)
        vg = uv[:, a_w + g * LANES:a_w + (g + 1) * LANES]
        vn = _rms(vg, gv_ref[:, sl]).astype(BF16)
        wm = jnp.where(chunk_mask, ws_ref[g], 0.0).astype(BF16)
        bias = bst_ref[:, g:g + 1]
        for n in range(tm // LANES):
            rs = slice(n * LANES, (n + 1) * LANES)
            s = _dot(wm, vn[rs]) + bias
            ya_ref[0, rs, sl] = (uv[rs, sl] * s).astype(BF16)

    tri = tri_ref[...]
    nsub = tm // FOX_TK
    run = run_ref[...]
    a_off = aoff_ref[...]
    c_q0 = cq0_ref[...]
    a_parts, b_parts = [], []
    for r in range(nsub):
        lf = logf[r * FOX_TK:(r + 1) * FOX_TK]
        hi, mid, lo = _split3(lf)
        lc = _dot(tri, hi) + _dot(tri, mid) + _dot(tri, lo)
        first = lf[0:1]
        tot = lc[FOX_TK - 1:FOX_TK]
        q_start = ((t * nsub + r) % blocks_per_q) == 0
        a_off = jnp.where(q_start, -first, a_off)
        c_q0 = jnp.where(q_start, run + first, c_q0)
        a_parts.append(a_off + lc)
        b_parts.append(tot - lc)
        ce_ref[0, r * 8:(r + 1) * 8] = jnp.broadcast_to(run + tot, (8, LANES))
        a_off = a_off + tot
        run = run + tot
    a_all = jnp.concatenate(a_parts, axis=0)
    b_all = jnp.concatenate(b_parts, axis=0)
    cs_ref[0] = jnp.broadcast_to(c_q0, (8, LANES))
    aoff_ref[...] = a_off
    cq0_ref[...] = c_q0
    run_ref[...] = run

    lane = lax.broadcasted_iota(jnp.int32, (tm, LANES), 1)
    low = lane < 64
    hd = 64
    q_off = 2 * a_w
    k_off = 2 * a_w + f_w

    def head_norm(blk, gain):
        sq = blk * blk
        s_lo = jnp.sum(jnp.where(low, sq, 0.0), axis=-1, keepdims=True)
        s_hi = jnp.sum(jnp.where(low, 0.0, sq), axis=-1, keepdims=True)
        r = jnp.where(low, lax.rsqrt(s_lo / hd + EPS), lax.rsqrt(s_hi / hd + EPS))
        return (blk * r) * gain

    n_heads = f_w // hd
    ab3 = jnp.concatenate(list(_split3(a_all)) + list(_split3(b_all)), axis=-1)
    routed = _dot(ab3, route_ref[...])
    l64 = lane & 63
    x_q = jnp.where((l64 >= 3 * n_heads) & (l64 < 6 * n_heads), 1.0, routed[:, :LANES])
    y_all = routed[:, LANES:]

    def k_extras(h):
        return jnp.where((l64 >= 3 * h) & (l64 < 3 * h + 3), 1.0,
                         jnp.where((l64 >= 3 * (n_heads + h)) & (l64 < 3 * (n_heads + h) + 3), y_all, 0.0))

    for j in range(f_w // LANES):
        sl = slice(j * LANES, (j + 1) * LANES)
        qn = head_norm(z[:, q_off + j * LANES:q_off + (j + 1) * LANES], gq_ref[...]) * (LOG2E / math.sqrt(hd))
        kn = head_norm(z[:, k_off + j * LANES:k_off + (j + 1) * LANES], gk_ref[...])
        for hh in range(2):
            hidx = 2 * j + hh
            qhalf = low if hh == 0 else jnp.logical_not(low)
            osl = slice(hidx * LANES, (hidx + 1) * LANES)
            qp_ref[0, :, osl] = jnp.where(qhalf, qn, x_q).astype(BF16)
            kp_ref[0, :, osl] = jnp.where(qhalf, kn, k_extras(hidx)).astype(BF16)

    for hidx in range(n_heads):
        vt_ref[0, hidx, 0:hd, :] = zvt[hidx * hd:(hidx + 1) * hd].astype(BF16)
        vt_ref[0, hidx, hd:2 * hd, :] = jnp.ones((hd, tm), BF16)


def _route_matrix(n_heads):
    assert 6 * n_heads <= 64
    r = np.zeros((6 * LANES, 2 * LANES), np.float32)
    for h in range(n_heads):
        for x in range(3):
            for half in (0, 64):
                r[x * LANES + h, half + 3 * h + x] = 1.0
                r[(3 + x) * LANES + h, LANES + half + 3 * (n_heads + h) + x] = 1.0
    return jnp.asarray(r, BF16)


def _even_in(x, g_mix, w_main, w_vt, w_f, b_f, g_v, w_s, b_s_t, g_q2, g_k2, tri):
    B, S, D = x.shape
    tm = ROW_TILE
    assert tm % FOX_TK == 0 and FOX_TQ % tm == 0
    nsub = tm // FOX_TK
    blocks_per_q = FOX_TQ // FOX_TK
    a_w = g_v.shape[1]
    f_w = w_vt.shape[0]
    n_heads = f_w // 64
    route = _route_matrix(n_heads)
    grid = (B, S // tm)
    row3 = lambda w: pl.BlockSpec((1, tm, w), lambda b, t: (b, t, 0))
    return pl.pallas_call(
        functools.partial(_even_in_kernel, blocks_per_q=blocks_per_q),
        grid=grid,
        in_specs=[row3(D), _const_spec(g_mix.shape), _const_spec(w_main.shape), _const_spec(w_vt.shape),
                  _const_spec(w_f.shape), _const_spec(b_f.shape), _const_spec(g_v.shape), _const_spec(w_s.shape),
                  _const_spec(b_s_t.shape), _const_spec(g_q2.shape), _const_spec(g_k2.shape),
                  _const_spec(tri.shape), _const_spec(route.shape)],
        out_specs=[row3(a_w), row3(n_heads * LANES), row3(n_heads * LANES),
                   pl.BlockSpec((1, n_heads, LANES, tm), lambda b, t: (b, 0, 0, t)),
                   pl.BlockSpec((1, 8, LANES), lambda b, t: (b, (t * nsub) // blocks_per_q, 0)),
                   pl.BlockSpec((1, 8 * nsub, LANES), lambda b, t: (b, t, 0))],
        out_shape=[jax.ShapeDtypeStruct((B, S, a_w), BF16),
                   jax.ShapeDtypeStruct((B, S, n_heads * LANES), BF16),
                   jax.ShapeDtypeStruct((B, S, n_heads * LANES), BF16),
                   jax.ShapeDtypeStruct((B, n_heads, LANES, S), BF16),
                   jax.ShapeDtypeStruct((B, (S // FOX_TQ) * 8, LANES), F32),
                   jax.ShapeDtypeStruct((B, (S // FOX_TK) * 8, LANES), F32)],
        scratch_shapes=[pltpu.VMEM((1, LANES), F32), pltpu.VMEM((1, LANES), F32), pltpu.VMEM((1, LANES), F32)],
        compiler_params=pltpu.CompilerParams(
            dimension_semantics=("arbitrary", "arbitrary"),
            vmem_limit_bytes=_vmem_limit(48 * 1024 * 1024)),
        name="even_in",
    )(x, g_mix, w_main, w_vt, w_f, b_f, g_v, w_s, b_s_t, g_q2, g_k2, tri, route)


def _fox_kernel(cs_ref, ce_ref, qp_ref, kp_ref, vt_ref, o_ref, s_buf, mx_buf, p_buf, al_buf, m_scr, acc_scr,
                *, nq, nk, n_heads):
    b = pl.program_id(0)
    hp = pl.program_id(1)
    tq, tk = FOX_TQ, FOX_TK
    sub = tq // tk
    T = MXU_TILE
    nct = tq // T
    nkt = tk // T
    zslab = jnp.zeros((T, LANES), BF16)

    def qk(i, j, masked, col_lo=0):
        q0 = pl.multiple_of(i * tq, tq)
        k0 = pl.multiple_of(j * tk, tk)
        for c in range(col_lo // T, nct):
            mxu = c % 2
            pltpu.matmul_push_rhs(qp_ref[0, pl.ds(q0 + c * T, T), :], staging_register=0, mxu_index=mxu,
                                  transpose=True)
            first = True
            for hh in range(2):
                for kt in range(nkt):
                    ks = kp_ref[0, pl.ds(k0 + kt * T, T), hh * LANES:(hh + 1) * LANES]
                    lhs = jnp.concatenate([ks, zslab] if hh == 0 else [zslab, ks], axis=1)
                    a_qk = ((hh * nkt + kt) % 2) * (T // 4)
                    pltpu.matmul_acc_lhs(a_qk, lhs, mxu, load_staged_rhs=0 if first else None)
                    first = False
                    st = pltpu.matmul_pop(a_qk, (T, T), F32, mxu)
                    if masked:
                        ri = lax.broadcasted_iota(jnp.int32, (T, T), 0) + kt * T
                        ci = lax.broadcasted_iota(jnp.int32, (T, T), 1) + (c * T - col_lo)
                        st = jnp.where(ci >= ri, st, NEG_BIG)
                    s_buf[hh, kt * T:(kt + 1) * T, c * T:(c + 1) * T] = st
                    cm = jnp.max(st, axis=0, keepdims=True)
                    cmax = cm if kt == 0 else jnp.maximum(cmax, cm)
                mx_buf[hh, :, c * T:(c + 1) * T] = jnp.broadcast_to(cmax, (8, T))

    def ex(i, j, par, col_lo=0):
        for hh in range(2):
            head = hp * 2 + hh
            d = cs_ref[(b * nq + i) * n_heads + head] - ce_ref[(b * nk + j) * n_heads + head]
            for c in range(col_lo // T, nct):
                cs_ = slice(c * T, (c + 1) * T)
                m_old = m_scr[hh, :, cs_]
                m_new = jnp.maximum(m_old, mx_buf[hh, :, cs_] + d)
                al_buf[par, hh, :, cs_] = jnp.exp2(m_old - m_new)
                m_scr[hh, :, cs_] = m_new
                shift = jnp.broadcast_to((m_new - d)[0:1], (T, T))
                for kt in range(nkt):
                    rs = slice(kt * T, (kt + 1) * T)
                    p_buf[par, hh, rs, cs_] = jnp.exp2(s_buf[hh, rs, cs_] - shift).astype(BF16)

    def pv(i, j, par, col_lo=0):
        k0 = pl.multiple_of(j * tk, tk)
        n = 0
        for hh in range(2):
            for c in range(col_lo // T, nct):
                mxu = c % 2
                cs_ = slice(c * T, (c + 1) * T)
                a_pv = 2 * (T // 4) + (LANES // 4) * ((n // 2) % 4)
                n += 1
                for kt in range(nkt):
                    pltpu.matmul_push_rhs(p_buf[par, hh, kt * T:(kt + 1) * T, cs_], staging_register=1,
                                          mxu_index=mxu)
                    pltpu.matmul_acc_lhs(a_pv, vt_ref[0, hh, :, pl.ds(k0 + kt * T, T)], mxu, load_staged_rhs=1)
                out = pltpu.matmul_pop(a_pv, (LANES, T), F32, mxu)
                acc_scr[hh, :, cs_] = (jnp.broadcast_to(al_buf[par, hh, 0:1, cs_], (LANES, T)) * acc_scr[hh, :, cs_]
                                       + out)

    def tail(i, n_full, has_full_blocks):
        for t in range(sub + 2):
            for stage, off in ((pv, t - 2), (ex, t - 1)):
                if off >= 0:
                    stage(i, n_full + off, off % 2, off * tk)
                elif has_full_blocks:
                    stage(i, n_full + off, off % 2)
            if t < sub:
                qk(i, n_full + t, True, t * tk)

    def q_body(i, carry):
        q0 = pl.multiple_of(i * tq, tq)
        n_full = i * sub
        for hh in range(2):
            m_scr[hh] = jnp.full((8, tq), NEG_BIG, F32)
            acc_scr[hh, :, 0:tq] = jnp.zeros((LANES, tq), F32)

        @pl.when(i == 0)
        def _():
            tail(i, 0, False)

        @pl.when(i > 0)
        def _():
            assert sub >= 2 and sub % 2 == 0
            qk(i, 0, False)
            ex(i, 0, 0)
            qk(i, 1, False)

            def body(u, c):
                s = 2 + 2 * u
                pv(i, s - 2, 0)
                ex(i, s - 1, 1)
                qk(i, s, False)
                pv(i, s - 1, 1)
                ex(i, s, 0)
                qk(i, s + 1, False)
                return c

            lax.fori_loop(0, (n_full - 2) // 2, body, 0)
            tail(i, n_full, True)

        hd = LANES // 2
        tops = []
        for hh in range(2):
            a = acc_scr[hh, :, 0:tq]
            tops.append(a[0:hd] * (1.0 / a[hd:2 * hd]))
        o_ref[0, pl.ds(q0, tq), :] = jnp.concatenate(tops, axis=0).T.astype(BF16)
        return carry

    lax.fori_loop(0, nq, q_body, 0)


def _fox(cs_flat, ce_flat, qp, kp, vt):
    B, S, HW = qp.shape
    n_heads = HW // LANES
    nq, nk = S // FOX_TQ, S // FOX_TK
    assert FOX_TQ % MXU_TILE == 0 and FOX_TK % MXU_TILE == 0 and 2 * LANES == MXU_TILE
    grid = (B, n_heads // 2)
    slab = pl.BlockSpec((1, S, 2 * LANES), lambda b, h: (b, 0, h))
    stat = pltpu.VMEM((2, 8, FOX_TQ), F32)
    return pl.pallas_call(
        functools.partial(_fox_kernel, nq=nq, nk=nk, n_heads=n_heads),
        grid=grid,
        in_specs=[pl.BlockSpec(memory_space=pltpu.SMEM), pl.BlockSpec(memory_space=pltpu.SMEM),
                  slab, slab, pl.BlockSpec((1, 2, LANES, S), lambda b, h: (b, h, 0, 0))],
        out_specs=pl.BlockSpec((1, S, LANES), lambda b, h: (b, 0, h)),
        out_shape=jax.ShapeDtypeStruct((B, S, n_heads * 64), BF16),
        scratch_shapes=[pltpu.VMEM((2, FOX_TK, FOX_TQ + LANES), F32), stat,
                        pltpu.VMEM((2, 2, FOX_TK, FOX_TQ + LANES), BF16),
                        pltpu.VMEM((2, 2, 8, FOX_TQ), F32),
                        stat, pltpu.VMEM((2, LANES, FOX_TQ + LANES), F32)],
        compiler_params=pltpu.CompilerParams(
            dimension_semantics=("arbitrary", "arbitrary"),
            vmem_limit_bytes=_vmem_limit(56 * 1024 * 1024)),
        name="fox_attn",
    )(cs_flat, ce_flat, qp, kp, vt)


def _odd_mix(x_tile, t, g_ref, w_ref, wp_ref, sp_ref, cw_ref, zbuf, xbuf, windows):
    tm = x_tile.shape[0]
    pw = sp_ref.shape[1]
    cwid = cw_ref.shape[1]

    @pl.when(t == 0)
    def _():
        zbuf[0:POOL_HALO, :] = jnp.zeros((POOL_HALO, pw), F32)
        xbuf[0:CONV_HALO, :] = jnp.zeros((CONV_HALO, cwid), F32)

    h = _rms(x_tile, g_ref[...]).astype(BF16)
    z = _dot(h, w_ref[...])
    zc = z[:, :pw]
    hdn = z[:, pw:pw + cwid]
    gb = z[:, pw + cwid:pw + 2 * cwid]
    gc = z[:, pw + 2 * cwid:pw + 3 * cwid]

    zbuf[POOL_HALO:POOL_HALO + tm, :] = zc
    pos = t * tm + lax.broadcasted_iota(jnp.int32, (tm, 1), 0)
    yc = []
    for g, w in enumerate(windows):
        sl = slice(g * LANES, (g + 1) * LANES)
        assert w & (w - 1) == 0
        acc = zbuf[:, sl]
        k = 1
        while k < w:
            acc = acc + pltpu.roll(acc, k, axis=0)
            k *= 2
        acc = acc[POOL_HALO:POOL_HALO + tm]
        inv_cnt = 1.0 / jnp.minimum(pos + 1, w).astype(F32)
        p = acc * inv_cnt - zc[:, sl]
        yc.append((_dot(p.astype(BF16), wp_ref[g]) * sp_ref[:, sl]).astype(BF16))
    zbuf[0:POOL_HALO, :] = zbuf[tm:tm + POOL_HALO, :]

    xg = gc * hdn
    xbuf[CONV_HALO:CONV_HALO + tm, :] = xg
    k = cw_ref.shape[0]
    conv = cw_ref[k - 1:k, :] * xg
    for j in range(1, k):
        conv = conv + cw_ref[k - 1 - j:k - j, :] * xbuf[CONV_HALO - j:CONV_HALO - j + tm, :]
    yd = (gb * conv).astype(BF16)
    xbuf[0:CONV_HALO, :] = xbuf[tm:tm + CONV_HALO, :]
    return jnp.concatenate(yc, axis=-1), yd


def _mem_kv_kernel(m_ref, g_ref, w_ref, gk_ref, k_ref, v_ref):
    xa = k_ref.shape[2]
    hm = _rms(m_ref[0], g_ref[...]).astype(BF16)
    kv = _dot(hm, w_ref[...])
    for h in range(xa // LANES):
        sl = slice(h * LANES, (h + 1) * LANES)
        k_ref[0, :, sl] = _rms(kv[:, sl], gk_ref[...]).astype(BF16)
    v_ref[0] = kv[:, xa:].astype(BF16)


def _mem_kv(mem, g_mem, w_kv, g_k):
    B, M, D = mem.shape
    xa = w_kv.shape[1] // 2
    blk = lambda w: pl.BlockSpec((1, M, w), lambda b: (b, 0, 0))
    return pl.pallas_call(
        _mem_kv_kernel,
        grid=(B,),
        in_specs=[blk(D), _const_spec(g_mem.shape), _const_spec(w_kv.shape), _const_spec(g_k.shape)],
        out_specs=[blk(xa), blk(xa)],
        out_shape=[jax.ShapeDtypeStruct((B, M, xa), BF16), jax.ShapeDtypeStruct((B, M, xa), BF16)],
        compiler_params=pltpu.CompilerParams(dimension_semantics=("arbitrary",)),
        name="mem_kv",
    )(mem, g_mem, w_kv, g_k)


def _post_tile(x_tile, ya, yb, woa_ref, wob_ref, gxa_ref, wq_ref, gq_ref, k_ref, v_ref, wo_ref,
               gff_ref, wg_ref, wu_ref, wd_ref, hs_scr):
    xa = wq_ref.shape[1]
    dff = wg_ref.shape[1]
    inv_sqrt = 1.0 / math.sqrt(LANES)
    x1 = x_tile + _dot(ya, woa_ref[...]) + _dot(yb, wob_ref[...])

    hx = _rms(x1, gxa_ref[...]).astype(BF16)
    q = _dot(hx, wq_ref[...])
    outs = []
    for h in range(xa // LANES):
        sl = slice(h * LANES, (h + 1) * LANES)
        qn = _rms(q[:, sl], gq_ref[...]).astype(BF16)
        s = _dot_nt(qn, k_ref[0, :, sl]) * inv_sqrt
        m = jnp.max(s, axis=-1, keepdims=True)
        p = jnp.exp(s - m)
        l = jnp.sum(p, axis=-1, keepdims=True)
        outs.append((_dot(p.astype(BF16), v_ref[0, :, sl]) * (1.0 / l)).astype(BF16))
    x2 = x1 + _dot(jnp.concatenate(outs, axis=-1), wo_ref[...])

    hf = _rms(x2, gff_ref[...]).astype(BF16)
    c0 = 0
    while c0 < dff:
        c1 = min(c0 + FF_CHUNK, dff)
        a = _dot(hf, wg_ref[:, c0:c1])
        u = _dot(hf, wu_ref[:, c0:c1])
        hs_scr[:, c0:c1] = (a * jax.nn.sigmoid(a) * u).astype(BF16)
        c0 = c1
    return x2 + _dot(hs_scr[...], wd_ref[...])


def _post_kernel(x_ref, ya_ref, yb_ref, *rest):
    *w_refs, o_ref, hs_scr = rest
    o_ref[0] = _post_tile(x_ref[0], ya_ref[0], yb_ref[0], *w_refs, hs_scr)


def _odd_layer_kernel(x_ref, gmix_ref, win_ref, wp_ref, sp_ref, cw_ref, *rest, windows):
    *w_refs, o_ref, hs_scr, zbuf, xbuf = rest
    x_tile = x_ref[0]
    yc, yd = _odd_mix(x_tile, pl.program_id(1), gmix_ref, win_ref, wp_ref, sp_ref, cw_ref, zbuf, xbuf, windows)
    o_ref[0] = _post_tile(x_tile, yc, yd, *w_refs, hs_scr)


def _layer_spec(stack, idx):
    return pl.BlockSpec((None,) + stack.shape[1:], lambda b, t: (idx, 0, 0), pipeline_mode=pl.Buffered(1))


def _post_specs(D, w_out, g_xa, w_q, g_q, k_mem, w_o, g_ffn, w_gate, w_up, w_down):
    M, xa = k_mem.shape[1], w_q[0].shape[2]
    wo_stack, wo_idx = w_out
    half = wo_stack.shape[1] // 2
    memb = pl.BlockSpec((1, M, xa), lambda b, t: (b, 0, 0))
    cs = lambda a: _const_spec(a.shape, single=True)
    wo_half = lambda k: pl.BlockSpec((None, half, D), lambda b, t: (wo_idx, k, 0), pipeline_mode=pl.Buffered(1))
    return [wo_half(0), wo_half(1), cs(g_xa), _layer_spec(*w_q), cs(g_q), memb, memb, _layer_spec(*w_o),
            cs(g_ffn), _layer_spec(*w_gate), _layer_spec(*w_up), _layer_spec(*w_down)]


def _post(x, ya, yb, w_out, g_xa, w_q, g_q, k_mem, v_mem, w_o, g_ffn, w_gate, w_up, w_down):
    B, S, D = x.shape
    tm = POST_TILE
    a_w, b_w = ya.shape[2], yb.shape[2]
    assert a_w == b_w and w_out[0].shape[1] == a_w + b_w
    row3 = lambda w: pl.BlockSpec((1, tm, w), lambda b, t: (b, t, 0))
    return pl.pallas_call(
        _post_kernel,
        grid=(B, S // tm),
        in_specs=[row3(D), row3(a_w), row3(b_w)] + _post_specs(D, w_out, g_xa, w_q, g_q, k_mem, w_o,
                                                              g_ffn, w_gate, w_up, w_down),
        out_specs=row3(D),
        out_shape=jax.ShapeDtypeStruct((B, S, D), F32),
        scratch_shapes=[pltpu.VMEM((tm, w_gate[0].shape[2]), BF16)],
        compiler_params=pltpu.CompilerParams(
            dimension_semantics=("arbitrary", "arbitrary"),
            vmem_limit_bytes=_vmem_limit(58 * 1024 * 1024)),
        name="post",
    )(x, ya, yb, w_out[0], w_out[0], g_xa, w_q[0], g_q, k_mem, v_mem, w_o[0], g_ffn,
      w_gate[0], w_up[0], w_down[0])


def _odd_layer(x, g_mix, w_in, w_pool, s_pool, conv_w, windows,
               w_out, g_xa, w_q, g_q, k_mem, v_mem, w_o, g_ffn, w_gate, w_up, w_down):
    B, S, D = x.shape
    tm = POST_TILE
    pw, cwid = s_pool.shape[1], conv_w.shape[1]
    assert max(windows) <= POOL_HALO and conv_w.shape[0] - 1 <= CONV_HALO
    assert pw == cwid and w_out[0].shape[1] == pw + cwid
    row3 = lambda w: pl.BlockSpec((1, tm, w), lambda b, t: (b, t, 0))
    cs = lambda a: _const_spec(a.shape, single=True)
    return pl.pallas_call(
        functools.partial(_odd_layer_kernel, windows=windows),
        grid=(B, S // tm),
        in_specs=[row3(D), cs(g_mix), cs(w_in), cs(w_pool), cs(s_pool), cs(conv_w)]
        + _post_specs(D, w_out, g_xa, w_q, g_q, k_mem, w_o, g_ffn, w_gate, w_up, w_down),
        out_specs=row3(D),
        out_shape=jax.ShapeDtypeStruct((B, S, D), F32),
        scratch_shapes=[pltpu.VMEM((tm, w_gate[0].shape[2]), BF16),
                        pltpu.VMEM((POOL_HALO + tm, pw), F32), pltpu.VMEM((CONV_HALO + tm, cwid), F32)],
        compiler_params=pltpu.CompilerParams(
            dimension_semantics=("arbitrary", "arbitrary"),
            vmem_limit_bytes=_vmem_limit(58 * 1024 * 1024)),
        name="odd_layer",
    )(x, g_mix, w_in, w_pool, s_pool, conv_w,
      w_out[0], w_out[0], g_xa, w_q[0], g_q, k_mem, v_mem, w_o[0], g_ffn, w_gate[0], w_up[0], w_down[0])


def kernel(x, mem, g_mix, g_xa, g_mem, xa_wq, xa_wkv, xa_wo, xa_gq, xa_gk, g_ffn, w_gate, w_up, w_down,
           e_w_in, e_b_f, e_g_v, e_w_s, e_b_s, e_g_qn, e_g_kn, e_w_out,
           o_w_in, o_w_pool, o_s_pool, o_conv_w, o_w_out):
    depth = g_mix.shape[0]
    S = x.shape[1]
    assert S % FOX_TQ == 0 and S % POST_TILE == 0 and S % ROW_TILE == 0
    row = lambda a: a.reshape(1, -1)
    tri = (lax.broadcasted_iota(jnp.int32, (FOX_TK, FOX_TK), 0)
           >= lax.broadcasted_iota(jnp.int32, (FOX_TK, FOX_TK), 1)).astype(BF16)
    pool_windows = (2, 4, 8, 16)[:o_w_pool.shape[1]]
    xa_wq_b, xa_wo_b, w_gate_b, w_up_b, w_down_b, e_w_out_b, o_w_out_b = (
        w.astype(BF16) for w in (xa_wq, xa_wo, w_gate, w_up, w_down, e_w_out, o_w_out))

    for layer in range(depth):
        i = layer // 2
        if layer % 2 == 0:
            n_heads = e_b_f.shape[1]
            a_w = e_g_v.shape[1]
            f_w = n_heads * e_g_qn.shape[1]
            n_uvqk = 2 * a_w + 2 * f_w
            w_main = e_w_in[i].astype(BF16)
            w_vt = w_main[:, n_uvqk:n_uvqk + f_w].T
            w_f = jnp.pad(w_main[:, n_uvqk + f_w:], ((0, 0), (0, LANES - n_heads)))
            b_f = jnp.pad(e_b_f[i], (0, LANES - n_heads)).reshape(1, LANES)
            g_q2 = jnp.tile(e_g_qn[i], 2).reshape(1, LANES)
            g_k2 = jnp.tile(e_g_kn[i], 2).reshape(1, LANES)
            ya, qp, kp, vt, cs, ce = _even_in(x, row(g_mix[layer]), w_main, w_vt, w_f, b_f, row(e_g_v[i]),
                                              e_w_s[i], e_b_s[i].T, g_q2, g_k2, tri)
            yb = _fox(cs[:, ::8, :n_heads].reshape(-1), ce[:, ::8, :n_heads].reshape(-1), qp, kp, vt)
        k_mem, v_mem = _mem_kv(mem, row(g_mem[layer]), xa_wkv[layer].astype(BF16), row(xa_gk[layer]))
        post_args = (row(g_xa[layer]), (xa_wq_b, layer), row(xa_gq[layer]), k_mem, v_mem,
                     (xa_wo_b, layer), row(g_ffn[layer]),
                     (w_gate_b, layer), (w_up_b, layer), (w_down_b, layer))
        if layer % 2 == 0:
            x = _post(x, ya, yb, (e_w_out_b, i), *post_args)
        else:
            x = _odd_layer(x, row(g_mix[layer]), o_w_in[i].astype(BF16), o_w_pool[i].astype(BF16),
                           row(o_s_pool[i]), o_conv_w[i], pool_windows, (o_w_out_b, i), *post_args)
    return x
```

```python
import functools
import math

import jax
import jax.numpy as jnp
import numpy as np
from jax import lax
from jax.experimental import pallas as pl
from jax.experimental.pallas import tpu as pltpu

F32 = jnp.float32
BF16 = jnp.bfloat16
EPS = 1e-6
LOG2E = 1.4426950408889634
NEG_BIG = -1e30

LANES = 128
MXU_TILE = 256
V7X_VMEM_BYTES = 64 * 1024 * 1024

ROW_TILE = 1024
POST_TILE = 1024
FOX_TQ = 2048
FOX_TK = 512
POOL_HALO = 16
CONV_HALO = 8
FF_CHUNK = 256


def _vmem_limit(nbytes):
    return int(min(nbytes, V7X_VMEM_BYTES - 4 * 1024 * 1024))


def _rms(x, g):
    ms = jnp.mean(x * x, axis=-1, keepdims=True)
    return (x * lax.rsqrt(ms + EPS)) * g


def _dot(a, b):
    return jnp.dot(a, b, preferred_element_type=F32)


def _dot_nt(a, b):
    return lax.dot_general(a, b, (((1,), (1,)), ((), ())), preferred_element_type=F32)


def _split3(x):
    hi = x.astype(BF16)
    r1 = x - hi.astype(F32)
    mid = r1.astype(BF16)
    lo = (r1 - mid.astype(F32)).astype(BF16)
    return hi, mid, lo


def _const_spec(shape, single=False):
    nd = len(shape)
    kw = {}
    if single:
        kw["pipeline_mode"] = pl.Buffered(1)
    return pl.BlockSpec(shape, lambda *_: (0,) * nd, **kw)


def _even_in_kernel(x_ref, g_ref, w_ref, wvt_ref, wf_ref, bf_ref, gv_ref, ws_ref, bst_ref, gq_ref, gk_ref,
                    tri_ref, route_ref, ya_ref, qp_ref, kp_ref, vt_ref, cs_ref, ce_ref, run_ref, aoff_ref, cq0_ref,
                    *, blocks_per_q):
    t = pl.program_id(1)
    tm = x_ref.shape[1]
    a_w = ya_ref.shape[2]
    f_w = qp_ref.shape[2] // 2
    n_grp = a_w // LANES

    @pl.when(t == 0)
    def _():
        run_ref[...] = jnp.zeros_like(run_ref)

    h = _rms(x_ref[0], g_ref[...]).astype(BF16)
    z = _dot(h, w_ref[:, 0:2 * a_w + 2 * f_w])
    zvt = _dot_nt(wvt_ref[...], h)
    fl = _dot(h, wf_ref[...]) + bf_ref[...]
    logf = -(jnp.maximum(-fl, 0.0) + jnp.log1p(jnp.exp(-jnp.abs(fl)))) * LOG2E

    uv = jax.nn.gelu(z[:, :2 * a_w])
    row = lax.broadcasted_iota(jnp.int32, (LANES, LANES), 0) // 64
    col = lax.broadcasted_iota(jnp.int32, (LANES, LANES), 1) // 64
    chunk_mask = row >= col
    for g in range(n_grp):
        sl = slice(g * LANES, (g + 1) * LANES)
        vg = uv[:, a_w + g * LANES:a_w + (g + 1) * LANES]
        vn = _rms(vg, gv_ref[:, sl]).astype(BF16)
        wm = jnp.where(chunk_mask, ws_ref[g], 0.0).astype(BF16)
        bias = bst_ref[:, g:g + 1]
        for n in range(tm // LANES):
            rs = slice(n * LANES, (n + 1) * LANES)
            s = _dot(wm, vn[rs]) + bias
            ya_ref[0, rs, sl] = (uv[rs, sl] * s).astype(BF16)

    tri = tri_ref[...]
    nsub = tm // FOX_TK
    run = run_ref[...]
    a_off = aoff_ref[...]
    c_q0 = cq0_ref[...]
    a_parts, b_parts = [], []
    for r in range(nsub):
        lf = logf[r * FOX_TK:(r + 1) * FOX_TK]
        hi, mid, lo = _split3(lf)
        lc = _dot(tri, hi) + _dot(tri, mid) + _dot(tri, lo)
        first = lf[0:1]
        tot = lc[FOX_TK - 1:FOX_TK]
        q_start = ((t * nsub + r) % blocks_per_q) == 0
        a_off = jnp.where(q_start, -first, a_off)
        c_q0 = jnp.where(q_start, run + first, c_q0)
        a_parts.append(a_off + lc)
        b_parts.append(tot - lc)
        ce_ref[0, r * 8:(r + 1) * 8] = jnp.broadcast_to(run + tot, (8, LANES))
        a_off = a_off + tot
        run = run + tot
    a_all = jnp.concatenate(a_parts, axis=0)
    b_all = jnp.concatenate(b_parts, axis=0)
    cs_ref[0] = jnp.broadcast_to(c_q0, (8, LANES))
    aoff_ref[...] = a_off
    cq0_ref[...] = c_q0
    run_ref[...] = run

    lane = lax.broadcasted_iota(jnp.int32, (tm, LANES), 1)
    low = lane < 64
    hd = 64
    q_off = 2 * a_w
    k_off = 2 * a_w + f_w

    def head_norm(blk, gain):
        sq = blk * blk
        s_lo = jnp.sum(jnp.where(low, sq, 0.0), axis=-1, keepdims=True)
        s_hi = jnp.sum(jnp.where(low, 0.0, sq), axis=-1, keepdims=True)
        r = jnp.where(low, lax.rsqrt(s_lo / hd + EPS), lax.rsqrt(s_hi / hd + EPS))
        return (blk * r) * gain

    n_heads = f_w // hd
    ab3 = jnp.concatenate(list(_split3(a_all)) + list(_split3(b_all)), axis=-1)
    routed = _dot(ab3, route_ref[...])
    l64 = lane & 63
    x_q = jnp.where((l64 >= 3 * n_heads) & (l64 < 6 * n_heads), 1.0, routed[:, :LANES])
    y_all = routed[:, LANES:]

    def k_extras(h):
        return jnp.where((l64 >= 3 * h) & (l64 < 3 * h + 3), 1.0,
                         jnp.where((l64 >= 3 * (n_heads + h)) & (l64 < 3 * (n_heads + h) + 3), y_all, 0.0))

    for j in range(f_w // LANES):
        sl = slice(j * LANES, (j + 1) * LANES)
        qn = head_norm(z[:, q_off + j * LANES:q_off + (j + 1) * LANES], gq_ref[...]) * (LOG2E / math.sqrt(hd))
        kn = head_norm(z[:, k_off + j * LANES:k_off + (j + 1) * LANES], gk_ref[...])
        for hh in range(2):
            hidx = 2 * j + hh
            qhalf = low if hh == 0 else jnp.logical_not(low)
            osl = slice(hidx * LANES, (hidx + 1) * LANES)
            qp_ref[0, :, osl] = jnp.where(qhalf, qn, x_q).astype(BF16)
            kp_ref[0, :, osl] = jnp.where(qhalf, kn, k_extras(hidx)).astype(BF16)

    for hidx in range(n_heads):
        vt_ref[0, hidx, 0:hd, :] = zvt[hidx * hd:(hidx + 1) * hd].astype(BF16)
        vt_ref[0, hidx, hd:2 * hd, :] = jnp.ones((hd, tm), BF16)


def _route_matrix(n_heads):
    assert 6 * n_heads <= 64
    r = np.zeros((6 * LANES, 2 * LANES), np.float32)
    for h in range(n_heads):
        for x in range(3):
            for half in (0, 64):
                r[x * LANES + h, half + 3 * h + x] = 1.0
                r[(3 + x) * LANES + h, LANES + half + 3 * (n_heads + h) + x] = 1.0
    return jnp.asarray(r, BF16)


def _even_in(x, g_mix, w_main, w_vt, w_f, b_f, g_v, w_s, b_s_t, g_q2, g_k2, tri):
    B, S, D = x.shape
    tm = ROW_TILE
    assert tm % FOX_TK == 0 and FOX_TQ % tm == 0
    nsub = tm // FOX_TK
    blocks_per_q = FOX_TQ // FOX_TK
    a_w = g_v.shape[1]
    f_w = w_vt.shape[0]
    n_heads = f_w // 64
    route = _route_matrix(n_heads)
    grid = (B, S // tm)
    row3 = lambda w: pl.BlockSpec((1, tm, w), lambda b, t: (b, t, 0))
    return pl.pallas_call(
        functools.partial(_even_in_kernel, blocks_per_q=blocks_per_q),
        grid=grid,
        in_specs=[row3(D), _const_spec(g_mix.shape), _const_spec(w_main.shape), _const_spec(w_vt.shape),
                  _const_spec(w_f.shape), _const_spec(b_f.shape), _const_spec(g_v.shape), _const_spec(w_s.shape),
                  _const_spec(b_s_t.shape), _const_spec(g_q2.shape), _const_spec(g_k2.shape),
                  _const_spec(tri.shape), _const_spec(route.shape)],
        out_specs=[row3(a_w), row3(n_heads * LANES), row3(n_heads * LANES),
                   pl.BlockSpec((1, n_heads, LANES, tm), lambda b, t: (b, 0, 0, t)),
                   pl.BlockSpec((1, 8, LANES), lambda b, t: (b, (t * nsub) // blocks_per_q, 0)),
                   pl.BlockSpec((1, 8 * nsub, LANES), lambda b, t: (b, t, 0))],
        out_shape=[jax.ShapeDtypeStruct((B, S, a_w), BF16),
                   jax.ShapeDtypeStruct((B, S, n_heads * LANES), BF16),
                   jax.ShapeDtypeStruct((B, S, n_heads * LANES), BF16),
                   jax.ShapeDtypeStruct((B, n_heads, LANES, S), BF16),
                   jax.ShapeDtypeStruct((B, (S // FOX_TQ) * 8, LANES), F32),
                   jax.ShapeDtypeStruct((B, (S // FOX_TK) * 8, LANES), F32)],
        scratch_shapes=[pltpu.VMEM((1, LANES), F32), pltpu.VMEM((1, LANES), F32), pltpu.VMEM((1, LANES), F32)],
        compiler_params=pltpu.CompilerParams(
            dimension_semantics=("arbitrary", "arbitrary"),
            vmem_limit_bytes=_vmem_limit(48 * 1024 * 1024)),
        name="even_in",
    )(x, g_mix, w_main, w_vt, w_f, b_f, g_v, w_s, b_s_t, g_q2, g_k2, tri, route)


def _fox_kernel(cs_ref, ce_ref, qp_ref, kp_ref, vt_ref, o_ref, s_buf, mx_buf, p_buf, al_buf, m_scr, acc_scr,
                *, nq, nk, n_heads):
    b = pl.program_id(0)
    hp = pl.program_id(1)
    tq, tk = FOX_TQ, FOX_TK
    sub = tq // tk
    T = MXU_TILE
    nct = tq // T
    nkt = tk // T
    zslab = jnp.zeros((T, LANES), BF16)

    def key_tiles(c, col_lo, diag):
        return [kt for kt in range(nkt) if not diag or kt <= c - col_lo // T]

    def qk(i, j, masked, col_lo=0):
        q0 = pl.multiple_of(i * tq, tq)
        k0 = pl.multiple_of(j * tk, tk)
        for c in range(col_lo // T, nct):
            mxu = c % 2
            pltpu.matmul_push_rhs(qp_ref[0, pl.ds(q0 + c * T, T), :], staging_register=0, mxu_index=mxu,
                                  transpose=True)
            first = True
            for hh in range(2):
                for kt in key_tiles(c, col_lo, masked):
                    ks = kp_ref[0, pl.ds(k0 + kt * T, T), hh * LANES:(hh + 1) * LANES]
                    lhs = jnp.concatenate([ks, zslab] if hh == 0 else [zslab, ks], axis=1)
                    a_qk = ((hh * nkt + kt) % 2) * (T // 4)
                    pltpu.matmul_acc_lhs(a_qk, lhs, mxu, load_staged_rhs=0 if first else None)
                    first = False
                    st = pltpu.matmul_pop(a_qk, (T, T), F32, mxu)
                    if masked and kt == c - col_lo // T:
                        ri = lax.broadcasted_iota(jnp.int32, (T, T), 0) + kt * T
                        ci = lax.broadcasted_iota(jnp.int32, (T, T), 1) + (c * T - col_lo)
                        st = jnp.where(ci >= ri, st, NEG_BIG)
                    s_buf[hh, kt * T:(kt + 1) * T, c * T:(c + 1) * T] = st
                    cm = jnp.max(st, axis=0, keepdims=True)
                    cmax = cm if kt == 0 else jnp.maximum(cmax, cm)
                mx_buf[hh, :, c * T:(c + 1) * T] = jnp.broadcast_to(cmax, (8, T))

    def ex(i, j, par, col_lo=0, diag=False):
        for hh in range(2):
            head = hp * 2 + hh
            d = cs_ref[(b * nq + i) * n_heads + head] - ce_ref[(b * nk + j) * n_heads + head]
            for c in range(col_lo // T, nct):
                cs_ = slice(c * T, (c + 1) * T)
                m_old = m_scr[hh, :, cs_]
                m_new = jnp.maximum(m_old, mx_buf[hh, :, cs_] + d)
                al_buf[par, hh, :, cs_] = jnp.exp2(m_old - m_new)
                m_scr[hh, :, cs_] = m_new
                shift = jnp.broadcast_to((m_new - d)[0:1], (T, T))
                for kt in key_tiles(c, col_lo, diag):
                    rs = slice(kt * T, (kt + 1) * T)
                    p_buf[par, hh, rs, cs_] = jnp.exp2(s_buf[hh, rs, cs_] - shift).astype(BF16)

    def pv(i, j, par, col_lo=0, diag=False):
        k0 = pl.multiple_of(j * tk, tk)
        n = 0
        for hh in range(2):
            for c in range(col_lo // T, nct):
                mxu = c % 2
                cs_ = slice(c * T, (c + 1) * T)
                a_pv = 2 * (T // 4) + (LANES // 4) * ((n // 2) % 4)
                n += 1
                for kt in key_tiles(c, col_lo, diag):
                    pltpu.matmul_push_rhs(p_buf[par, hh, kt * T:(kt + 1) * T, cs_], staging_register=1,
                                          mxu_index=mxu)
                    pltpu.matmul_acc_lhs(a_pv, vt_ref[0, hh, :, pl.ds(k0 + kt * T, T)], mxu, load_staged_rhs=1)
                out = pltpu.matmul_pop(a_pv, (LANES, T), F32, mxu)
                acc_scr[hh, :, cs_] = (jnp.broadcast_to(al_buf[par, hh, 0:1, cs_], (LANES, T)) * acc_scr[hh, :, cs_]
                                       + out)

    def tail(i, n_full, has_full_blocks):
        for t in range(sub + 2):
            for stage, off in ((pv, t - 2), (ex, t - 1)):
                if off >= 0:
                    stage(i, n_full + off, off % 2, off * tk, True)
                elif has_full_blocks:
                    stage(i, n_full + off, off % 2)
            if t < sub:
                qk(i, n_full + t, True, t * tk)

    def q_body(i, carry):
        q0 = pl.multiple_of(i * tq, tq)
        n_full = i * sub
        for hh in range(2):
            m_scr[hh] = jnp.full((8, tq), NEG_BIG, F32)
            acc_scr[hh, :, 0:tq] = jnp.zeros((LANES, tq), F32)

        @pl.when(i == 0)
        def _():
            tail(i, 0, False)

        @pl.when(i > 0)
        def _():
            assert sub >= 2 and sub % 2 == 0
            qk(i, 0, False)
            ex(i, 0, 0)
            qk(i, 1, False)

            def body(u, c):
                s = 2 + 2 * u
                pv(i, s - 2, 0)
                ex(i, s - 1, 1)
                qk(i, s, False)
                pv(i, s - 1, 1)
                ex(i, s, 0)
                qk(i, s + 1, False)
                return c

            lax.fori_loop(0, (n_full - 2) // 2, body, 0)
            tail(i, n_full, True)

        hd = LANES // 2
        tops = []
        for hh in range(2):
            a = acc_scr[hh, :, 0:tq]
            tops.append(a[0:hd] * (1.0 / a[hd:2 * hd]))
        o_ref[0, pl.ds(q0, tq), :] = jnp.concatenate(tops, axis=0).T.astype(BF16)
        return carry

    lax.fori_loop(0, nq, q_body, 0)


def _fox(cs_flat, ce_flat, qp, kp, vt):
    B, S, HW = qp.shape
    n_heads = HW // LANES
    nq, nk = S // FOX_TQ, S // FOX_TK
    assert FOX_TQ % MXU_TILE == 0 and FOX_TK % MXU_TILE == 0 and 2 * LANES == MXU_TILE
    grid = (B, n_heads // 2)
    slab = pl.BlockSpec((1, S, 2 * LANES), lambda b, h: (b, 0, h))
    stat = pltpu.VMEM((2, 8, FOX_TQ), F32)
    return pl.pallas_call(
        functools.partial(_fox_kernel, nq=nq, nk=nk, n_heads=n_heads),
        grid=grid,
        in_specs=[pl.BlockSpec(memory_space=pltpu.SMEM), pl.BlockSpec(memory_space=pltpu.SMEM),
                  slab, slab, pl.BlockSpec((1, 2, LANES, S), lambda b, h: (b, h, 0, 0))],
        out_specs=pl.BlockSpec((1, S, LANES), lambda b, h: (b, 0, h)),
        out_shape=jax.ShapeDtypeStruct((B, S, n_heads * 64), BF16),
        scratch_shapes=[pltpu.VMEM((2, FOX_TK, FOX_TQ + LANES), F32), stat,
                        pltpu.VMEM((2, 2, FOX_TK, FOX_TQ + LANES), BF16),
                        pltpu.VMEM((2, 2, 8, FOX_TQ), F32),
                        stat, pltpu.VMEM((2, LANES, FOX_TQ + LANES), F32)],
        compiler_params=pltpu.CompilerParams(
            dimension_semantics=("arbitrary", "arbitrary"),
            vmem_limit_bytes=_vmem_limit(56 * 1024 * 1024)),
        name="fox_attn",
    )(cs_flat, ce_flat, qp, kp, vt)


def _odd_mix(x_tile, t, g_ref, w_ref, wp_ref, sp_ref, cw_ref, zbuf, xbuf, windows):
    tm = x_tile.shape[0]
    pw = sp_ref.shape[1]
    cwid = cw_ref.shape[1]

    @pl.when(t == 0)
    def _():
        zbuf[0:POOL_HALO, :] = jnp.zeros((POOL_HALO, pw), F32)
        xbuf[0:CONV_HALO, :] = jnp.zeros((CONV_HALO, cwid), F32)

    h = _rms(x_tile, g_ref[...]).astype(BF16)
    z = _dot(h, w_ref[...])
    zc = z[:, :pw]
    hdn = z[:, pw:pw + cwid]
    gb = z[:, pw + cwid:pw + 2 * cwid]
    gc = z[:, pw + 2 * cwid:pw + 3 * cwid]

    zbuf[POOL_HALO:POOL_HALO + tm, :] = zc
    pos = t * tm + lax.broadcasted_iota(jnp.int32, (tm, 1), 0)
    yc = []
    for g, w in enumerate(windows):
        sl = slice(g * LANES, (g + 1) * LANES)
        assert w & (w - 1) == 0
        acc = zbuf[:, sl]
        k = 1
        while k < w:
            acc = acc + pltpu.roll(acc, k, axis=0)
            k *= 2
        acc = acc[POOL_HALO:POOL_HALO + tm]
        inv_cnt = 1.0 / jnp.minimum(pos + 1, w).astype(F32)
        p = acc * inv_cnt - zc[:, sl]
        yc.append((_dot(p.astype(BF16), wp_ref[g]) * sp_ref[:, sl]).astype(BF16))
    zbuf[0:POOL_HALO, :] = zbuf[tm:tm + POOL_HALO, :]

    xg = gc * hdn
    xbuf[CONV_HALO:CONV_HALO + tm, :] = xg
    k = cw_ref.shape[0]
    conv = cw_ref[k - 1:k, :] * xg
    for j in range(1, k):
        conv = conv + cw_ref[k - 1 - j:k - j, :] * xbuf[CONV_HALO - j:CONV_HALO - j + tm, :]
    yd = (gb * conv).astype(BF16)
    xbuf[0:CONV_HALO, :] = xbuf[tm:tm + CONV_HALO, :]
    return jnp.concatenate(yc, axis=-1), yd


def _mem_kv_kernel(m_ref, g_ref, w_ref, gk_ref, k_ref, v_ref):
    xa = k_ref.shape[2]
    hm = _rms(m_ref[0], g_ref[...]).astype(BF16)
    kv = _dot(hm, w_ref[...])
    for h in range(xa // LANES):
        sl = slice(h * LANES, (h + 1) * LANES)
        k_ref[0, :, sl] = _rms(kv[:, sl], gk_ref[...]).astype(BF16)
    v_ref[0] = kv[:, xa:].astype(BF16)


def _mem_kv(mem, g_mem, w_kv, g_k):
    B, M, D = mem.shape
    xa = w_kv.shape[1] // 2
    blk = lambda w: pl.BlockSpec((1, M, w), lambda b: (b, 0, 0))
    return pl.pallas_call(
        _mem_kv_kernel,
        grid=(B,),
        in_specs=[blk(D), _const_spec(g_mem.shape), _const_spec(w_kv.shape), _const_spec(g_k.shape)],
        out_specs=[blk(xa), blk(xa)],
        out_shape=[jax.ShapeDtypeStruct((B, M, xa), BF16), jax.ShapeDtypeStruct((B, M, xa), BF16)],
        compiler_params=pltpu.CompilerParams(dimension_semantics=("arbitrary",)),
        name="mem_kv",
    )(mem, g_mem, w_kv, g_k)


def _post_tile(x_tile, ya, yb, woa_ref, wob_ref, gxa_ref, wq_ref, gq_ref, k_ref, v_ref, wo_ref,
               gff_ref, wg_ref, wu_ref, wd_ref, hs_scr):
    xa = wq_ref.shape[1]
    dff = wg_ref.shape[1]
    inv_sqrt = 1.0 / math.sqrt(LANES)
    x1 = x_tile + _dot(ya, woa_ref[...]) + _dot(yb, wob_ref[...])

    hx = _rms(x1, gxa_ref[...]).astype(BF16)
    q = _dot(hx, wq_ref[...])
    outs = []
    for h in range(xa // LANES):
        sl = slice(h * LANES, (h + 1) * LANES)
        qn = _rms(q[:, sl], gq_ref[...]).astype(BF16)
        s = _dot_nt(qn, k_ref[0, :, sl]) * inv_sqrt
        m = jnp.max(s, axis=-1, keepdims=True)
        p = jnp.exp(s - m)
        l = jnp.sum(p, axis=-1, keepdims=True)
        outs.append((_dot(p.astype(BF16), v_ref[0, :, sl]) * (1.0 / l)).astype(BF16))
    x2 = x1 + _dot(jnp.concatenate(outs, axis=-1), wo_ref[...])

    hf = _rms(x2, gff_ref[...]).astype(BF16)
    c0 = 0
    while c0 < dff:
        c1 = min(c0 + FF_CHUNK, dff)
        a = _dot(hf, wg_ref[:, c0:c1])
        u = _dot(hf, wu_ref[:, c0:c1])
        hs_scr[:, c0:c1] = (a * jax.nn.sigmoid(a) * u).astype(BF16)
        c0 = c1
    return x2 + _dot(hs_scr[...], wd_ref[...])


def _post_kernel(x_ref, ya_ref, yb_ref, *rest):
    *w_refs, o_ref, hs_scr = rest
    o_ref[0] = _post_tile(x_ref[0], ya_ref[0], yb_ref[0], *w_refs, hs_scr)


def _odd_layer_kernel(x_ref, gmix_ref, win_ref, wp_ref, sp_ref, cw_ref, *rest, windows):
    *w_refs, o_ref, hs_scr, zbuf, xbuf = rest
    x_tile = x_ref[0]
    yc, yd = _odd_mix(x_tile, pl.program_id(1), gmix_ref, win_ref, wp_ref, sp_ref, cw_ref, zbuf, xbuf, windows)
    o_ref[0] = _post_tile(x_tile, yc, yd, *w_refs, hs_scr)


def _layer_spec(stack, idx):
    return pl.BlockSpec((None,) + stack.shape[1:], lambda b, t: (idx, 0, 0), pipeline_mode=pl.Buffered(1))


def _post_specs(D, w_out, g_xa, w_q, g_q, k_mem, w_o, g_ffn, w_gate, w_up, w_down):
    M, xa = k_mem.shape[1], w_q[0].shape[2]
    wo_stack, wo_idx = w_out
    half = wo_stack.shape[1] // 2
    memb = pl.BlockSpec((1, M, xa), lambda b, t: (b, 0, 0))
    cs = lambda a: _const_spec(a.shape, single=True)
    wo_half = lambda k: pl.BlockSpec((None, half, D), lambda b, t: (wo_idx, k, 0), pipeline_mode=pl.Buffered(1))
    return [wo_half(0), wo_half(1), cs(g_xa), _layer_spec(*w_q), cs(g_q), memb, memb, _layer_spec(*w_o),
            cs(g_ffn), _layer_spec(*w_gate), _layer_spec(*w_up), _layer_spec(*w_down)]


def _post(x, ya, yb, w_out, g_xa, w_q, g_q, k_mem, v_mem, w_o, g_ffn, w_gate, w_up, w_down):
    B, S, D = x.shape
    tm = POST_TILE
    a_w, b_w = ya.shape[2], yb.shape[2]
    assert a_w == b_w and w_out[0].shape[1] == a_w + b_w
    row3 = lambda w: pl.BlockSpec((1, tm, w), lambda b, t: (b, t, 0))
    return pl.pallas_call(
        _post_kernel,
        grid=(B, S // tm),
        in_specs=[row3(D), row3(a_w), row3(b_w)] + _post_specs(D, w_out, g_xa, w_q, g_q, k_mem, w_o,
                                                              g_ffn, w_gate, w_up, w_down),
        out_specs=row3(D),
        out_shape=jax.ShapeDtypeStruct((B, S, D), F32),
        scratch_shapes=[pltpu.VMEM((tm, w_gate[0].shape[2]), BF16)],
        compiler_params=pltpu.CompilerParams(
            dimension_semantics=("arbitrary", "arbitrary"),
            vmem_limit_bytes=_vmem_limit(58 * 1024 * 1024)),
        name="post",
    )(x, ya, yb, w_out[0], w_out[0], g_xa, w_q[0], g_q, k_mem, v_mem, w_o[0], g_ffn,
      w_gate[0], w_up[0], w_down[0])


def _odd_layer(x, g_mix, w_in, w_pool, s_pool, conv_w, windows,
               w_out, g_xa, w_q, g_q, k_mem, v_mem, w_o, g_ffn, w_gate, w_up, w_down):
    B, S, D = x.shape
    tm = POST_TILE
    pw, cwid = s_pool.shape[1], conv_w.shape[1]
    assert max(windows) <= POOL_HALO and conv_w.shape[0] - 1 <= CONV_HALO
    assert pw == cwid and w_out[0].shape[1] == pw + cwid
    row3 = lambda w: pl.BlockSpec((1, tm, w), lambda b, t: (b, t, 0))
    cs = lambda a: _const_spec(a.shape, single=True)
    return pl.pallas_call(
        functools.partial(_odd_layer_kernel, windows=windows),
        grid=(B, S // tm),
        in_specs=[row3(D), cs(g_mix), cs(w_in), cs(w_pool), cs(s_pool), cs(conv_w)]
        + _post_specs(D, w_out, g_xa, w_q, g_q, k_mem, w_o, g_ffn, w_gate, w_up, w_down),
        out_specs=row3(D),
        out_shape=jax.ShapeDtypeStruct((B, S, D), F32),
        scratch_shapes=[pltpu.VMEM((tm, w_gate[0].shape[2]), BF16),
                        pltpu.VMEM((POOL_HALO + tm, pw), F32), pltpu.VMEM((CONV_HALO + tm, cwid), F32)],
        compiler_params=pltpu.CompilerParams(
            dimension_semantics=("arbitrary", "arbitrary"),
            vmem_limit_bytes=_vmem_limit(58 * 1024 * 1024)),
        name="odd_layer",
    )(x, g_mix, w_in, w_pool, s_pool, conv_w,
      w_out[0], w_out[0], g_xa, w_q[0], g_q, k_mem, v_mem, w_o[0], g_ffn, w_gate[0], w_up[0], w_down[0])


def kernel(x, mem, g_mix, g_xa, g_mem, xa_wq, xa_wkv, xa_wo, xa_gq, xa_gk, g_ffn, w_gate, w_up, w_down,
           e_w_in, e_b_f, e_g_v, e_w_s, e_b_s, e_g_qn, e_g_kn, e_w_out,
           o_w_in, o_w_pool, o_s_pool, o_conv_w, o_w_out):
    depth = g_mix.shape[0]
    S = x.shape[1]
    assert S % FOX_TQ == 0 and S % POST_TILE == 0 and S % ROW_TILE == 0
    row = lambda a: a.reshape(1, -1)
    tri = (lax.broadcasted_iota(jnp.int32, (FOX_TK, FOX_TK), 0)
           >= lax.broadcasted_iota(jnp.int32, (FOX_TK, FOX_TK), 1)).astype(BF16)
    pool_windows = (2, 4, 8, 16)[:o_w_pool.shape[1]]
    xa_wq_b, xa_wo_b, w_gate_b, w_up_b, w_down_b, e_w_out_b, o_w_out_b = (
        w.astype(BF16) for w in (xa_wq, xa_wo, w_gate, w_up, w_down, e_w_out, o_w_out))

    for layer in range(depth):
        i = layer // 2
        if layer % 2 == 0:
            n_heads = e_b_f.shape[1]
            a_w = e_g_v.shape[1]
            f_w = n_heads * e_g_qn.shape[1]
            n_uvqk = 2 * a_w + 2 * f_w
            w_main = e_w_in[i].astype(BF16)
            w_vt = w_main[:, n_uvqk:n_uvqk + f_w].T
            w_f = jnp.pad(w_main[:, n_uvqk + f_w:], ((0, 0), (0, LANES - n_heads)))
            b_f = jnp.pad(e_b_f[i], (0, LANES - n_heads)).reshape(1, LANES)
            g_q2 = jnp.tile(e_g_qn[i], 2).reshape(1, LANES)
            g_k2 = jnp.tile(e_g_kn[i], 2).reshape(1, LANES)
            ya, qp, kp, vt, cs, ce = _even_in(x, row(g_mix[layer]), w_main, w_vt, w_f, b_f, row(e_g_v[i]),
                                              e_w_s[i], e_b_s[i].T, g_q2, g_k2, tri)
            yb = _fox(cs[:, ::8, :n_heads].reshape(-1), ce[:, ::8, :n_heads].reshape(-1), qp, kp, vt)
        k_mem, v_mem = _mem_kv(mem, row(g_mem[layer]), xa_wkv[layer].astype(BF16), row(xa_gk[layer]))
        post_args = (row(g_xa[layer]), (xa_wq_b, layer), row(xa_gq[layer]), k_mem, v_mem,
                     (xa_wo_b, layer), row(g_ffn[layer]),
                     (w_gate_b, layer), (w_up_b, layer), (w_down_b, layer))
        if layer % 2 == 0:
            x = _post(x, ya, yb, (e_w_out_b, i), *post_args)
        else:
            x = _odd_layer(x, row(g_mix[layer]), o_w_in[i].astype(BF16), o_w_pool[i].astype(BF16),
                           row(o_s_pool[i]), o_conv_w[i], pool_windows, (o_w_out_b, i), *post_args)
    return x
```

```python
import functools
import math

import jax
import jax.numpy as jnp
import numpy as np
from jax import lax
from jax.experimental import pallas as pl
from jax.experimental.pallas import tpu as pltpu

F32 = jnp.float32
BF16 = jnp.bfloat16
EPS = 1e-6
LOG2E = 1.4426950408889634
NEG_BIG = -1e30

LANES = 128
MXU_TILE = 256
V7X_VMEM_BYTES = 64 * 1024 * 1024

ROW_TILE = 1024
POST_TILE = 1024
FOX_TQ = 2048
FOX_TK = 512
POOL_HALO = 16
CONV_HALO = 8
FF_CHUNK = 256


def _vmem_limit(nbytes):
    return int(min(nbytes, V7X_VMEM_BYTES - 4 * 1024 * 1024))


def _rms(x, g):
    ms = jnp.mean(x * x, axis=-1, keepdims=True)
    return (x * lax.rsqrt(ms + EPS)) * g


def _dot(a, b):
    return jnp.dot(a, b, preferred_element_type=F32)


def _dot_nt(a, b):
    return lax.dot_general(a, b, (((1,), (1,)), ((), ())), preferred_element_type=F32)


def _split3(x):
    hi = x.astype(BF16)
    r1 = x - hi.astype(F32)
    mid = r1.astype(BF16)
    lo = (r1 - mid.astype(F32)).astype(BF16)
    return hi, mid, lo


def _const_spec(shape, single=False):
    nd = len(shape)
    kw = {}
    if single:
        kw["pipeline_mode"] = pl.Buffered(1)
    return pl.BlockSpec(shape, lambda *_: (0,) * nd, **kw)


def _even_in_kernel(x_ref, g_ref, w_ref, wvt_ref, wf_ref, bf_ref, gv_ref, ws_ref, bst_ref, gq_ref, gk_ref,
                    tri_ref, route_ref, ya_ref, qp_ref, kp_ref, vt_ref, cs_ref, ce_ref, run_ref, aoff_ref, cq0_ref,
                    *, blocks_per_q):
    t = pl.program_id(1)
    tm = x_ref.shape[1]
    a_w = ya_ref.shape[2]
    f_w = qp_ref.shape[2] // 2
    n_grp = a_w // LANES

    @pl.when(t == 0)
    def _():
        run_ref[...] = jnp.zeros_like(run_ref)

    h = _rms(x_ref[0], g_ref[...]).astype(BF16)
    z = _dot(h, w_ref[:, 0:2 * a_w + 2 * f_w])
    zvt = _dot_nt(wvt_ref[...], h)
    fl = _dot(h, wf_ref[...]) + bf_ref[...]
    logf = -(jnp.maximum(-fl, 0.0) + jnp.log1p(jnp.exp(-jnp.abs(fl)))) * LOG2E

    uv = jax.nn.gelu(z[:, :2 * a_w])
    row = lax.broadcasted_iota(jnp.int32, (LANES, LANES), 0) // 64
    col = lax.broadcasted_iota(jnp.int32, (LANES, LANES), 1) // 64
    chunk_mask = row >= col
    for g in range(n_grp):
        sl = slice(g * LANES, (g + 1) * LANES)
        vg = uv[:, a_w + g * LANES:a_w + (g + 1) * LANES]
        vn = _rms(vg, gv_ref[:, sl]).astype(BF16)
        wm = jnp.where(chunk_mask, ws_ref[g], 0.0).astype(BF16)
        bias = bst_ref[:, g:g + 1]
        for n in range(tm // LANES):
            rs = slice(n * LANES, (n + 1) * LANES)
            s = _dot(wm, vn[rs]) + bias
            ya_ref[0, rs, sl] = (uv[rs, sl] * s).astype(BF16)

    tri = tri_ref[...]
    nsub = tm // FOX_TK
    run = run_ref[...]
    a_off = aoff_ref[...]
    c_q0 = cq0_ref[...]
    a_parts, b_parts = [], []
    for r in range(nsub):
        lf = logf[r * FOX_TK:(r + 1) * FOX_TK]
        hi, mid, lo = _split3(lf)
        lc = _dot(tri, hi) + _dot(tri, mid) + _dot(tri, lo)
        first = lf[0:1]
        tot = lc[FOX_TK - 1:FOX_TK]
        q_start = ((t * nsub + r) % blocks_per_q) == 0
        a_off = jnp.where(q_start, -first, a_off)
        c_q0 = jnp.where(q_start, run + first, c_q0)
        a_parts.append(a_off + lc)
        b_parts.append(tot - lc)
        ce_ref[0, r * 8:(r + 1) * 8] = jnp.broadcast_to(run + tot, (8, LANES))
        a_off = a_off + tot
        run = run + tot
    a_all = jnp.concatenate(a_parts, axis=0)
    b_all = jnp.concatenate(b_parts, axis=0)
    cs_ref[0] = jnp.broadcast_to(c_q0, (8, LANES))
    aoff_ref[...] = a_off
    cq0_ref[...] = c_q0
    run_ref[...] = run

    lane = lax.broadcasted_iota(jnp.int32, (tm, LANES), 1)
    low = lane < 64
    hd = 64
    q_off = 2 * a_w
    k_off = 2 * a_w + f_w

    def head_norm(blk, gain):
        sq = blk * blk
        s_lo = jnp.sum(jnp.where(low, sq, 0.0), axis=-1, keepdims=True)
        s_hi = jnp.sum(jnp.where(low, 0.0, sq), axis=-1, keepdims=True)
        r = jnp.where(low, lax.rsqrt(s_lo / hd + EPS), lax.rsqrt(s_hi / hd + EPS))
        return (blk * r) * gain

    n_heads = f_w // hd
    ab3 = jnp.concatenate(list(_split3(a_all)) + list(_split3(b_all)), axis=-1)
    routed = _dot(ab3, route_ref[...])
    l64 = lane & 63
    x_q = jnp.where((l64 >= 3 * n_heads) & (l64 < 6 * n_heads), 1.0, routed[:, :LANES])
    y_all = routed[:, LANES:]

    def k_extras(h):
        return jnp.where((l64 >= 3 * h) & (l64 < 3 * h + 3), 1.0,
                         jnp.where((l64 >= 3 * (n_heads + h)) & (l64 < 3 * (n_heads + h) + 3), y_all, 0.0))

    for j in range(f_w // LANES):
        sl = slice(j * LANES, (j + 1) * LANES)
        qn = head_norm(z[:, q_off + j * LANES:q_off + (j + 1) * LANES], gq_ref[...]) * (LOG2E / math.sqrt(hd))
        kn = head_norm(z[:, k_off + j * LANES:k_off + (j + 1) * LANES], gk_ref[...])
        for hh in range(2):
            hidx = 2 * j + hh
            qhalf = low if hh == 0 else jnp.logical_not(low)
            osl = slice(hidx * LANES, (hidx + 1) * LANES)
            qp_ref[0, :, osl] = jnp.where(qhalf, qn, x_q).astype(BF16)
            kp_ref[0, :, osl] = jnp.where(qhalf, kn, k_extras(hidx)).astype(BF16)

    for hidx in range(n_heads):
        vt_ref[0, hidx, 0:hd, :] = zvt[hidx * hd:(hidx + 1) * hd].astype(BF16)
        vt_ref[0, hidx, hd:2 * hd, :] = jnp.ones((hd, tm), BF16)


def _route_matrix(n_heads):
    assert 6 * n_heads <= 64
    r = np.zeros((6 * LANES, 2 * LANES), np.float32)
    for h in range(n_heads):
        for x in range(3):
            for half in (0, 64):
                r[x * LANES + h, half + 3 * h + x] = 1.0
                r[(3 + x) * LANES + h, LANES + half + 3 * (n_heads + h) + x] = 1.0
    return jnp.asarray(r, BF16)


def _even_in(x, g_mix, w_main, w_vt, w_f, b_f, g_v, w_s, b_s_t, g_q2, g_k2, tri):
    B, S, D = x.shape
    tm = ROW_TILE
    assert tm % FOX_TK == 0 and FOX_TQ % tm == 0
    nsub = tm // FOX_TK
    blocks_per_q = FOX_TQ // FOX_TK
    a_w = g_v.shape[1]
    f_w = w_vt.shape[0]
    n_heads = f_w // 64
    route = _route_matrix(n_heads)
    grid = (B, S // tm)
    row3 = lambda w: pl.BlockSpec((1, tm, w), lambda b, t: (b, t, 0))
    return pl.pallas_call(
        functools.partial(_even_in_kernel, blocks_per_q=blocks_per_q),
        grid=grid,
        in_specs=[row3(D), _const_spec(g_mix.shape), _const_spec(w_main.shape), _const_spec(w_vt.shape),
                  _const_spec(w_f.shape), _const_spec(b_f.shape), _const_spec(g_v.shape), _const_spec(w_s.shape),
                  _const_spec(b_s_t.shape), _const_spec(g_q2.shape), _const_spec(g_k2.shape),
                  _const_spec(tri.shape), _const_spec(route.shape)],
        out_specs=[row3(a_w), row3(n_heads * LANES), row3(n_heads * LANES),
                   pl.BlockSpec((1, n_heads, LANES, tm), lambda b, t: (b, 0, 0, t)),
                   pl.BlockSpec((1, 8, LANES), lambda b, t: (b, (t * nsub) // blocks_per_q, 0)),
                   pl.BlockSpec((1, 8 * nsub, LANES), lambda b, t: (b, t, 0))],
        out_shape=[jax.ShapeDtypeStruct((B, S, a_w), BF16),
                   jax.ShapeDtypeStruct((B, S, n_heads * LANES), BF16),
                   jax.ShapeDtypeStruct((B, S, n_heads * LANES), BF16),
                   jax.ShapeDtypeStruct((B, n_heads, LANES, S), BF16),
                   jax.ShapeDtypeStruct((B, (S // FOX_TQ) * 8, LANES), F32),
                   jax.ShapeDtypeStruct((B, (S // FOX_TK) * 8, LANES), F32)],
        scratch_shapes=[pltpu.VMEM((1, LANES), F32), pltpu.VMEM((1, LANES), F32), pltpu.VMEM((1, LANES), F32)],
        compiler_params=pltpu.CompilerParams(
            dimension_semantics=("arbitrary", "arbitrary"),
            vmem_limit_bytes=_vmem_limit(48 * 1024 * 1024)),
        name="even_in",
    )(x, g_mix, w_main, w_vt, w_f, b_f, g_v, w_s, b_s_t, g_q2, g_k2, tri, route)


def _fox_kernel(cs_ref, ce_ref, qp_ref, kp_ref, vt_ref, o_ref, s_buf, mx_buf, p_buf, al_buf, m_scr, acc_scr,
                *, nq, nk, n_heads):
    b = pl.program_id(0)
    hp = pl.program_id(1)
    tq, tk = FOX_TQ, FOX_TK
    sub = tq // tk
    T = MXU_TILE
    nct = tq // T
    nkt = tk // T
    zslab = jnp.zeros((T, LANES), BF16)

    def key_tiles(c, col_lo, diag):
        return [kt for kt in range(nkt) if not diag or kt <= c - col_lo // T]

    def qk(i, j, masked, col_lo=0):
        q0 = pl.multiple_of(i * tq, tq)
        k0 = pl.multiple_of(j * tk, tk)
        for c in range(col_lo // T, nct):
            mxu = c % 2
            pltpu.matmul_push_rhs(qp_ref[0, pl.ds(q0 + c * T, T), :], staging_register=0, mxu_index=mxu,
                                  transpose=True)
            prods = [(hh, kt) for hh in range(2) for kt in key_tiles(c, col_lo, masked)]
            cmax = {}

            def issue(n):
                hh, kt = prods[n]
                ks = kp_ref[0, pl.ds(k0 + kt * T, T), hh * LANES:(hh + 1) * LANES]
                lhs = jnp.concatenate([ks, zslab] if hh == 0 else [zslab, ks], axis=1)
                pltpu.matmul_acc_lhs((n % 2) * (T // 4), lhs, mxu, load_staged_rhs=0 if n == 0 else None)

            def drain(n):
                hh, kt = prods[n]
                st = pltpu.matmul_pop((n % 2) * (T // 4), (T, T), F32, mxu)
                if masked and kt == c - col_lo // T:
                    ri = lax.broadcasted_iota(jnp.int32, (T, T), 0) + kt * T
                    ci = lax.broadcasted_iota(jnp.int32, (T, T), 1) + (c * T - col_lo)
                    st = jnp.where(ci >= ri, st, NEG_BIG)
                s_buf[hh, kt * T:(kt + 1) * T, c * T:(c + 1) * T] = st
                cm = jnp.max(st, axis=0, keepdims=True)
                cmax[hh] = cm if hh not in cmax else jnp.maximum(cmax[hh], cm)

            issue(0)
            for n in range(len(prods)):
                if n + 1 < len(prods):
                    issue(n + 1)
                drain(n)
            for hh in range(2):
                mx_buf[hh, :, c * T:(c + 1) * T] = jnp.broadcast_to(cmax[hh], (8, T))

    def ex(i, j, par, col_lo=0, diag=False):
        for hh in range(2):
            head = hp * 2 + hh
            d = cs_ref[(b * nq + i) * n_heads + head] - ce_ref[(b * nk + j) * n_heads + head]
            for c in range(col_lo // T, nct):
                cs_ = slice(c * T, (c + 1) * T)
                m_old = m_scr[hh, :, cs_]
                m_new = jnp.maximum(m_old, mx_buf[hh, :, cs_] + d)
                al_buf[par, hh, :, cs_] = jnp.exp2(m_old - m_new)
                m_scr[hh, :, cs_] = m_new
                shift = jnp.broadcast_to((m_new - d)[0:1], (T, T))
                for kt in key_tiles(c, col_lo, diag):
                    rs = slice(kt * T, (kt + 1) * T)
                    p_buf[par, hh, rs, cs_] = jnp.exp2(s_buf[hh, rs, cs_] - shift).astype(BF16)

    def pv(i, j, par, col_lo=0, diag=False):
        k0 = pl.multiple_of(j * tk, tk)
        n = 0
        for hh in range(2):
            for c in range(col_lo // T, nct):
                mxu = c % 2
                cs_ = slice(c * T, (c + 1) * T)
                a_pv = 2 * (T // 4) + (LANES // 4) * ((n // 2) % 4)
                n += 1
                for kt in key_tiles(c, col_lo, diag):
                    pltpu.matmul_push_rhs(p_buf[par, hh, kt * T:(kt + 1) * T, cs_], staging_register=1,
                                          mxu_index=mxu)
                    pltpu.matmul_acc_lhs(a_pv, vt_ref[0, hh, :, pl.ds(k0 + kt * T, T)], mxu, load_staged_rhs=1)
                out = pltpu.matmul_pop(a_pv, (LANES, T), F32, mxu)
                acc_scr[hh, :, cs_] = (jnp.broadcast_to(al_buf[par, hh, 0:1, cs_], (LANES, T)) * acc_scr[hh, :, cs_]
                                       + out)

    def tail(i, n_full, has_full_blocks):
        for t in range(sub + 2):
            for stage, off in ((pv, t - 2), (ex, t - 1)):
                if off >= 0:
                    stage(i, n_full + off, off % 2, off * tk, True)
                elif has_full_blocks:
                    stage(i, n_full + off, off % 2)
            if t < sub:
                qk(i, n_full + t, True, t * tk)

    def q_body(i, carry):
        q0 = pl.multiple_of(i * tq, tq)
        n_full = i * sub
        for hh in range(2):
            m_scr[hh] = jnp.full((8, tq), NEG_BIG, F32)
            acc_scr[hh, :, 0:tq] = jnp.zeros((LANES, tq), F32)

        @pl.when(i == 0)
        def _():
            tail(i, 0, False)

        @pl.when(i > 0)
        def _():
            assert sub >= 2 and sub % 2 == 0
            qk(i, 0, False)
            ex(i, 0, 0)
            qk(i, 1, False)

            def body(u, c):
                s = 2 + 2 * u
                pv(i, s - 2, 0)
                ex(i, s - 1, 1)
                qk(i, s, False)
                pv(i, s - 1, 1)
                ex(i, s, 0)
                qk(i, s + 1, False)
                return c

            lax.fori_loop(0, (n_full - 2) // 2, body, 0)
            tail(i, n_full, True)

        hd = LANES // 2
        tops = []
        for hh in range(2):
            a = acc_scr[hh, :, 0:tq]
            tops.append(a[0:hd] * (1.0 / a[hd:2 * hd]))
        o_ref[0, pl.ds(q0, tq), :] = jnp.concatenate(tops, axis=0).T.astype(BF16)
        return carry

    lax.fori_loop(0, nq, q_body, 0)


def _fox(cs_flat, ce_flat, qp, kp, vt):
    B, S, HW = qp.shape
    n_heads = HW // LANES
    nq, nk = S // FOX_TQ, S // FOX_TK
    assert FOX_TQ % MXU_TILE == 0 and FOX_TK % MXU_TILE == 0 and 2 * LANES == MXU_TILE
    grid = (B, n_heads // 2)
    slab = pl.BlockSpec((1, S, 2 * LANES), lambda b, h: (b, 0, h))
    stat = pltpu.VMEM((2, 8, FOX_TQ), F32)
    return pl.pallas_call(
        functools.partial(_fox_kernel, nq=nq, nk=nk, n_heads=n_heads),
        grid=grid,
        in_specs=[pl.BlockSpec(memory_space=pltpu.SMEM), pl.BlockSpec(memory_space=pltpu.SMEM),
                  slab, slab, pl.BlockSpec((1, 2, LANES, S), lambda b, h: (b, h, 0, 0))],
        out_specs=pl.BlockSpec((1, S, LANES), lambda b, h: (b, 0, h)),
        out_shape=jax.ShapeDtypeStruct((B, S, n_heads * 64), BF16),
        scratch_shapes=[pltpu.VMEM((2, FOX_TK, FOX_TQ + LANES), F32), stat,
                        pltpu.VMEM((2, 2, FOX_TK, FOX_TQ + LANES), BF16),
                        pltpu.VMEM((2, 2, 8, FOX_TQ), F32),
                        stat, pltpu.VMEM((2, LANES, FOX_TQ + LANES), F32)],
        compiler_params=pltpu.CompilerParams(
            dimension_semantics=("arbitrary", "arbitrary"),
            vmem_limit_bytes=_vmem_limit(56 * 1024 * 1024)),
        name="fox_attn",
    )(cs_flat, ce_flat, qp, kp, vt)


def _odd_mix(x_tile, t, g_ref, w_ref, wp_ref, sp_ref, cw_ref, zbuf, xbuf, windows):
    tm = x_tile.shape[0]
    pw = sp_ref.shape[1]
    cwid = cw_ref.shape[1]

    @pl.when(t == 0)
    def _():
        zbuf[0:POOL_HALO, :] = jnp.zeros((POOL_HALO, pw), F32)
        xbuf[0:CONV_HALO, :] = jnp.zeros((CONV_HALO, cwid), F32)

    h = _rms(x_tile, g_ref[...]).astype(BF16)
    z = _dot(h, w_ref[...])
    zc = z[:, :pw]
    hdn = z[:, pw:pw + cwid]
    gb = z[:, pw + cwid:pw + 2 * cwid]
    gc = z[:, pw + 2 * cwid:pw + 3 * cwid]

    zbuf[POOL_HALO:POOL_HALO + tm, :] = zc
    pos = t * tm + lax.broadcasted_iota(jnp.int32, (tm, 1), 0)
    yc = []
    for g, w in enumerate(windows):
        sl = slice(g * LANES, (g + 1) * LANES)
        assert w & (w - 1) == 0
        acc = zbuf[:, sl]
        k = 1
        while k < w:
            acc = acc + pltpu.roll(acc, k, axis=0)
            k *= 2
        acc = acc[POOL_HALO:POOL_HALO + tm]
        inv_cnt = 1.0 / jnp.minimum(pos + 1, w).astype(F32)
        p = acc * inv_cnt - zc[:, sl]
        yc.append((_dot(p.astype(BF16), wp_ref[g]) * sp_ref[:, sl]).astype(BF16))
    zbuf[0:POOL_HALO, :] = zbuf[tm:tm + POOL_HALO, :]

    xg = gc * hdn
    xbuf[CONV_HALO:CONV_HALO + tm, :] = xg
    k = cw_ref.shape[0]
    conv = cw_ref[k - 1:k, :] * xg
    for j in range(1, k):
        conv = conv + cw_ref[k - 1 - j:k - j, :] * xbuf[CONV_HALO - j:CONV_HALO - j + tm, :]
    yd = (gb * conv).astype(BF16)
    xbuf[0:CONV_HALO, :] = xbuf[tm:tm + CONV_HALO, :]
    return jnp.concatenate(yc, axis=-1), yd


def _mem_kv_kernel(m_ref, g_ref, w_ref, gk_ref, k_ref, v_ref):
    xa = k_ref.shape[2]
    hm = _rms(m_ref[0], g_ref[...]).astype(BF16)
    kv = _dot(hm, w_ref[...].astype(BF16))
    for h in range(xa // LANES):
        sl = slice(h * LANES, (h + 1) * LANES)
        k_ref[0, :, sl] = _rms(kv[:, sl], gk_ref[...]).astype(BF16)
    v_ref[0] = kv[:, xa:].astype(BF16)


def _mem_kv(mem, g_mem, w_kv, g_k):
    B, M, D = mem.shape
    L = w_kv.shape[0]
    xa = w_kv.shape[2] // 2
    per_layer = lambda a: pl.BlockSpec((None,) + a.shape[1:], lambda l, b: (l,) + (0,) * (a.ndim - 1))
    out = pl.BlockSpec((None, 1, M, xa), lambda l, b: (l, b, 0, 0))
    g_mem3, g_k3 = g_mem.reshape(L, 1, D), g_k.reshape(L, 1, -1)
    return pl.pallas_call(
        _mem_kv_kernel,
        grid=(L, B),
        in_specs=[pl.BlockSpec((1, M, D), lambda l, b: (b, 0, 0)), per_layer(g_mem3), per_layer(w_kv),
                  per_layer(g_k3)],
        out_specs=[out, out],
        out_shape=[jax.ShapeDtypeStruct((L, B, M, xa), BF16), jax.ShapeDtypeStruct((L, B, M, xa), BF16)],
        compiler_params=pltpu.CompilerParams(dimension_semantics=("arbitrary", "arbitrary"),
                                             vmem_limit_bytes=_vmem_limit(40 * 1024 * 1024)),
        name="mem_kv",
    )(mem, g_mem3, w_kv, g_k3)


def _post_tile(x_tile, ya, yb, woa_ref, wob_ref, gxa_ref, wq_ref, gq_ref, k_ref, v_ref, wo_ref,
               gff_ref, wg_ref, wu_ref, wd_ref, hs_scr):
    xa = wq_ref.shape[1]
    dff = wg_ref.shape[1]
    inv_sqrt = 1.0 / math.sqrt(LANES)
    x1 = x_tile + _dot(ya, woa_ref[...]) + _dot(yb, wob_ref[...])

    hx = _rms(x1, gxa_ref[...]).astype(BF16)
    q = _dot(hx, wq_ref[...])
    outs = []
    for h in range(xa // LANES):
        sl = slice(h * LANES, (h + 1) * LANES)
        qn = _rms(q[:, sl], gq_ref[...]).astype(BF16)
        s = _dot_nt(qn, k_ref[0, :, sl]) * inv_sqrt
        m = jnp.max(s, axis=-1, keepdims=True)
        p = jnp.exp(s - m)
        l = jnp.sum(p, axis=-1, keepdims=True)
        outs.append((_dot(p.astype(BF16), v_ref[0, :, sl]) * (1.0 / l)).astype(BF16))
    x2 = x1 + _dot(jnp.concatenate(outs, axis=-1), wo_ref[...])

    hf = _rms(x2, gff_ref[...]).astype(BF16)
    c0 = 0
    while c0 < dff:
        c1 = min(c0 + FF_CHUNK, dff)
        a = _dot(hf, wg_ref[:, c0:c1])
        u = _dot(hf, wu_ref[:, c0:c1])
        hs_scr[:, c0:c1] = (a * jax.nn.sigmoid(a) * u).astype(BF16)
        c0 = c1
    return x2 + _dot(hs_scr[...], wd_ref[...])


def _post_kernel(x_ref, ya_ref, yb_ref, *rest):
    *w_refs, o_ref, hs_scr = rest
    o_ref[0] = _post_tile(x_ref[0], ya_ref[0], yb_ref[0], *w_refs, hs_scr)


def _odd_layer_kernel(x_ref, gmix_ref, win_ref, wp_ref, sp_ref, cw_ref, *rest, windows):
    *w_refs, o_ref, hs_scr, zbuf, xbuf = rest
    x_tile = x_ref[0]
    yc, yd = _odd_mix(x_tile, pl.program_id(1), gmix_ref, win_ref, wp_ref, sp_ref, cw_ref, zbuf, xbuf, windows)
    o_ref[0] = _post_tile(x_tile, yc, yd, *w_refs, hs_scr)


def _layer_spec(stack, idx):
    return pl.BlockSpec((None,) + stack.shape[1:], lambda b, t: (idx, 0, 0), pipeline_mode=pl.Buffered(1))


def _post_specs(D, w_out, g_xa, w_q, g_q, k_mem, w_o, g_ffn, w_gate, w_up, w_down):
    kv_stack, kv_idx = k_mem
    M, xa = kv_stack.shape[2], w_q[0].shape[2]
    wo_stack, wo_idx = w_out
    half = wo_stack.shape[1] // 2
    memb = pl.BlockSpec((None, 1, M, xa), lambda b, t: (kv_idx, b, 0, 0))
    cs = lambda a: _const_spec(a.shape, single=True)
    wo_half = lambda k: pl.BlockSpec((None, half, D), lambda b, t: (wo_idx, k, 0), pipeline_mode=pl.Buffered(1))
    return [wo_half(0), wo_half(1), cs(g_xa), _layer_spec(*w_q), cs(g_q), memb, memb, _layer_spec(*w_o),
            cs(g_ffn), _layer_spec(*w_gate), _layer_spec(*w_up), _layer_spec(*w_down)]


def _post(x, ya, yb, w_out, g_xa, w_q, g_q, k_mem, v_mem, w_o, g_ffn, w_gate, w_up, w_down):
    B, S, D = x.shape
    tm = POST_TILE
    a_w, b_w = ya.shape[2], yb.shape[2]
    assert a_w == b_w and w_out[0].shape[1] == a_w + b_w
    row3 = lambda w: pl.BlockSpec((1, tm, w), lambda b, t: (b, t, 0))
    return pl.pallas_call(
        _post_kernel,
        grid=(B, S // tm),
        in_specs=[row3(D), row3(a_w), row3(b_w)] + _post_specs(D, w_out, g_xa, w_q, g_q, k_mem, w_o,
                                                              g_ffn, w_gate, w_up, w_down),
        out_specs=row3(D),
        out_shape=jax.ShapeDtypeStruct((B, S, D), F32),
        scratch_shapes=[pltpu.VMEM((tm, w_gate[0].shape[2]), BF16)],
        compiler_params=pltpu.CompilerParams(
            dimension_semantics=("arbitrary", "arbitrary"),
            vmem_limit_bytes=_vmem_limit(58 * 1024 * 1024)),
        name="post",
    )(x, ya, yb, w_out[0], w_out[0], g_xa, w_q[0], g_q, k_mem[0], v_mem[0], w_o[0], g_ffn,
      w_gate[0], w_up[0], w_down[0])


def _odd_layer(x, g_mix, w_in, w_pool, s_pool, conv_w, windows,
               w_out, g_xa, w_q, g_q, k_mem, v_mem, w_o, g_ffn, w_gate, w_up, w_down):
    B, S, D = x.shape
    tm = POST_TILE
    pw, cwid = s_pool.shape[1], conv_w.shape[1]
    assert max(windows) <= POOL_HALO and conv_w.shape[0] - 1 <= CONV_HALO
    assert pw == cwid and w_out[0].shape[1] == pw + cwid
    row3 = lambda w: pl.BlockSpec((1, tm, w), lambda b, t: (b, t, 0))
    cs = lambda a: _const_spec(a.shape, single=True)
    return pl.pallas_call(
        functools.partial(_odd_layer_kernel, windows=windows),
        grid=(B, S // tm),
        in_specs=[row3(D), cs(g_mix), cs(w_in), cs(w_pool), cs(s_pool), cs(conv_w)]
        + _post_specs(D, w_out, g_xa, w_q, g_q, k_mem, w_o, g_ffn, w_gate, w_up, w_down),
        out_specs=row3(D),
        out_shape=jax.ShapeDtypeStruct((B, S, D), F32),
        scratch_shapes=[pltpu.VMEM((tm, w_gate[0].shape[2]), BF16),
                        pltpu.VMEM((POOL_HALO + tm, pw), F32), pltpu.VMEM((CONV_HALO + tm, cwid), F32)],
        compiler_params=pltpu.CompilerParams(
            dimension_semantics=("arbitrary", "arbitrary"),
            vmem_limit_bytes=_vmem_limit(58 * 1024 * 1024)),
        name="odd_layer",
    )(x, g_mix, w_in, w_pool, s_pool, conv_w,
      w_out[0], w_out[0], g_xa, w_q[0], g_q, k_mem[0], v_mem[0], w_o[0], g_ffn, w_gate[0], w_up[0], w_down[0])


def kernel(x, mem, g_mix, g_xa, g_mem, xa_wq, xa_wkv, xa_wo, xa_gq, xa_gk, g_ffn, w_gate, w_up, w_down,
           e_w_in, e_b_f, e_g_v, e_w_s, e_b_s, e_g_qn, e_g_kn, e_w_out,
           o_w_in, o_w_pool, o_s_pool, o_conv_w, o_w_out):
    depth = g_mix.shape[0]
    S = x.shape[1]
    assert S % FOX_TQ == 0 and S % POST_TILE == 0 and S % ROW_TILE == 0
    row = lambda a: a.reshape(1, -1)
    tri = (lax.broadcasted_iota(jnp.int32, (FOX_TK, FOX_TK), 0)
           >= lax.broadcasted_iota(jnp.int32, (FOX_TK, FOX_TK), 1)).astype(BF16)
    pool_windows = (2, 4, 8, 16)[:o_w_pool.shape[1]]
    xa_wq_b, xa_wo_b, w_gate_b, w_up_b, w_down_b, e_w_out_b, o_w_out_b = (
        w.astype(BF16) for w in (xa_wq, xa_wo, w_gate, w_up, w_down, e_w_out, o_w_out))
    k_all, v_all = _mem_kv(mem, g_mem, xa_wkv, xa_gk)

    for layer in range(depth):
        i = layer // 2
        if layer % 2 == 0:
            n_heads = e_b_f.shape[1]
            a_w = e_g_v.shape[1]
            f_w = n_heads * e_g_qn.shape[1]
            n_uvqk = 2 * a_w + 2 * f_w
            w_main = e_w_in[i].astype(BF16)
            w_vt = w_main[:, n_uvqk:n_uvqk + f_w].T
            w_f = jnp.pad(w_main[:, n_uvqk + f_w:], ((0, 0), (0, LANES - n_heads)))
            b_f = jnp.pad(e_b_f[i], (0, LANES - n_heads)).reshape(1, LANES)
            g_q2 = jnp.tile(e_g_qn[i], 2).reshape(1, LANES)
            g_k2 = jnp.tile(e_g_kn[i], 2).reshape(1, LANES)
            ya, qp, kp, vt, cs, ce = _even_in(x, row(g_mix[layer]), w_main, w_vt, w_f, b_f, row(e_g_v[i]),
                                              e_w_s[i], e_b_s[i].T, g_q2, g_k2, tri)
            yb = _fox(cs[:, ::8, :n_heads].reshape(-1), ce[:, ::8, :n_heads].reshape(-1), qp, kp, vt)
        post_args = (row(g_xa[layer]), (xa_wq_b, layer), row(xa_gq[layer]), (k_all, layer), (v_all, layer),
                     (xa_wo_b, layer), row(g_ffn[layer]),
                     (w_gate_b, layer), (w_up_b, layer), (w_down_b, layer))
        if layer % 2 == 0:
            x = _post(x, ya, yb, (e_w_out_b, i), *post_args)
        else:
            x = _odd_layer(x, row(g_mix[layer]), o_w_in[i].astype(BF16), o_w_pool[i].astype(BF16),
                           row(o_s_pool[i]), o_conv_w[i], pool_windows, (o_w_out_b, i), *post_args)
    return x
```

```python
import functools
import math

import jax
import jax.numpy as jnp
import numpy as np
from jax import lax
from jax.experimental import pallas as pl
from jax.experimental.pallas import tpu as pltpu

F32 = jnp.float32
BF16 = jnp.bfloat16
EPS = 1e-6
LOG2E = 1.4426950408889634
NEG_BIG = -1e30

LANES = 128
MXU_TILE = 256
V7X_VMEM_BYTES = 64 * 1024 * 1024

ROW_TILE = 1024
POST_TILE = 1024
FOX_TQ = 2048
FOX_TK = 512
POOL_HALO = 16
CONV_HALO = 8
FF_CHUNK = 256


def _vmem_limit(nbytes):
    return int(min(nbytes, V7X_VMEM_BYTES - 4 * 1024 * 1024))


def _rms(x, g):
    ms = jnp.mean(x * x, axis=-1, keepdims=True)
    return (x * lax.rsqrt(ms + EPS)) * g


def _dot(a, b):
    return jnp.dot(a, b, preferred_element_type=F32)


def _dot_nt(a, b):
    return lax.dot_general(a, b, (((1,), (1,)), ((), ())), preferred_element_type=F32)


def _split3(x):
    hi = x.astype(BF16)
    r1 = x - hi.astype(F32)
    mid = r1.astype(BF16)
    lo = (r1 - mid.astype(F32)).astype(BF16)
    return hi, mid, lo


def _const_spec(shape, single=False):
    nd = len(shape)
    kw = {}
    if single:
        kw["pipeline_mode"] = pl.Buffered(1)
    return pl.BlockSpec(shape, lambda *_: (0,) * nd, **kw)


def _even_in_kernel(x_ref, g_ref, w_ref, wvt_ref, wf_ref, bf_ref, gv_ref, ws_ref, bst_ref, gq_ref, gk_ref,
                    tri_ref, route_ref, ya_ref, qp_ref, kp_ref, vt_ref, cs_ref, ce_ref, run_ref, aoff_ref, cq0_ref,
                    *, blocks_per_q):
    t = pl.program_id(1)
    tm = x_ref.shape[1]
    a_w = ya_ref.shape[2]
    f_w = qp_ref.shape[2] // 2
    n_grp = a_w // LANES

    @pl.when(t == 0)
    def _():
        run_ref[...] = jnp.zeros_like(run_ref)

    h = _rms(x_ref[0], g_ref[...]).astype(BF16)
    z = _dot(h, w_ref[:, 0:2 * a_w + 2 * f_w])
    zvt = _dot_nt(wvt_ref[...], h)
    fl = _dot(h, wf_ref[...]) + bf_ref[...]
    logf = -(jnp.maximum(-fl, 0.0) + jnp.log1p(jnp.exp(-jnp.abs(fl)))) * LOG2E

    uv = jax.nn.gelu(z[:, :2 * a_w])
    row = lax.broadcasted_iota(jnp.int32, (LANES, LANES), 0) // 64
    col = lax.broadcasted_iota(jnp.int32, (LANES, LANES), 1) // 64
    chunk_mask = row >= col
    for g in range(n_grp):
        sl = slice(g * LANES, (g + 1) * LANES)
        vg = uv[:, a_w + g * LANES:a_w + (g + 1) * LANES]
        vn = _rms(vg, gv_ref[:, sl]).astype(BF16)
        wm = jnp.where(chunk_mask, ws_ref[g], 0.0).astype(BF16)
        bias = bst_ref[:, g:g + 1]
        nb = tm // LANES
        v_cat = jnp.concatenate([vn[n * LANES:(n + 1) * LANES] for n in range(nb)], axis=1)
        s_cat = _dot(wm, v_cat)
        for n in range(nb):
            rs = slice(n * LANES, (n + 1) * LANES)
            s = s_cat[:, n * LANES:(n + 1) * LANES] + bias
            ya_ref[0, rs, sl] = (uv[rs, sl] * s).astype(BF16)

    tri = tri_ref[...]
    nsub = tm // FOX_TK
    run = run_ref[...]
    a_off = aoff_ref[...]
    c_q0 = cq0_ref[...]
    a_parts, b_parts = [], []
    for r in range(nsub):
        lf = logf[r * FOX_TK:(r + 1) * FOX_TK]
        hi, mid, lo = _split3(lf)
        lc = _dot(tri, hi) + _dot(tri, mid) + _dot(tri, lo)
        first = lf[0:1]
        tot = lc[FOX_TK - 1:FOX_TK]
        q_start = ((t * nsub + r) % blocks_per_q) == 0
        a_off = jnp.where(q_start, -first, a_off)
        c_q0 = jnp.where(q_start, run + first, c_q0)
        a_parts.append(a_off + lc)
        b_parts.append(tot - lc)
        ce_ref[0, r * 8:(r + 1) * 8] = jnp.broadcast_to(run + tot, (8, LANES))
        a_off = a_off + tot
        run = run + tot
    a_all = jnp.concatenate(a_parts, axis=0)
    b_all = jnp.concatenate(b_parts, axis=0)
    cs_ref[0] = jnp.broadcast_to(c_q0, (8, LANES))
    aoff_ref[...] = a_off
    cq0_ref[...] = c_q0
    run_ref[...] = run

    lane = lax.broadcasted_iota(jnp.int32, (tm, LANES), 1)
    low = lane < 64
    hd = 64
    q_off = 2 * a_w
    k_off = 2 * a_w + f_w

    def head_norm(blk, gain):
        sq = blk * blk
        s_lo = jnp.sum(jnp.where(low, sq, 0.0), axis=-1, keepdims=True)
        s_hi = jnp.sum(jnp.where(low, 0.0, sq), axis=-1, keepdims=True)
        r = jnp.where(low, lax.rsqrt(s_lo / hd + EPS), lax.rsqrt(s_hi / hd + EPS))
        return (blk * r) * gain

    n_heads = f_w // hd
    ab3 = jnp.concatenate(list(_split3(a_all)) + list(_split3(b_all)), axis=-1)
    routed = _dot(ab3, route_ref[...])
    l64 = lane & 63
    x_q = jnp.where((l64 >= 3 * n_heads) & (l64 < 6 * n_heads), 1.0, routed[:, :LANES])
    y_all = routed[:, LANES:]

    def k_extras(h):
        return jnp.where((l64 >= 3 * h) & (l64 < 3 * h + 3), 1.0,
                         jnp.where((l64 >= 3 * (n_heads + h)) & (l64 < 3 * (n_heads + h) + 3), y_all, 0.0))

    for j in range(f_w // LANES):
        sl = slice(j * LANES, (j + 1) * LANES)
        qn = head_norm(z[:, q_off + j * LANES:q_off + (j + 1) * LANES], gq_ref[...]) * (LOG2E / math.sqrt(hd))
        kn = head_norm(z[:, k_off + j * LANES:k_off + (j + 1) * LANES], gk_ref[...])
        for hh in range(2):
            hidx = 2 * j + hh
            qhalf = low if hh == 0 else jnp.logical_not(low)
            osl = slice(hidx * LANES, (hidx + 1) * LANES)
            qp_ref[0, :, osl] = jnp.where(qhalf, qn, x_q).astype(BF16)
            kp_ref[0, :, osl] = jnp.where(qhalf, kn, k_extras(hidx)).astype(BF16)

    for hidx in range(n_heads):
        vt_ref[0, hidx, 0:hd, :] = zvt[hidx * hd:(hidx + 1) * hd].astype(BF16)
        vt_ref[0, hidx, hd:2 * hd, :] = jnp.ones((hd, tm), BF16)


def _route_matrix(n_heads):
    assert 6 * n_heads <= 64
    r = np.zeros((6 * LANES, 2 * LANES), np.float32)
    for h in range(n_heads):
        for x in range(3):
            for half in (0, 64):
                r[x * LANES + h, half + 3 * h + x] = 1.0
                r[(3 + x) * LANES + h, LANES + half + 3 * (n_heads + h) + x] = 1.0
    return jnp.asarray(r, BF16)


def _even_in(x, g_mix, w_main, w_vt, w_f, b_f, g_v, w_s, b_s_t, g_q2, g_k2, tri):
    B, S, D = x.shape
    tm = ROW_TILE
    assert tm % FOX_TK == 0 and FOX_TQ % tm == 0
    nsub = tm // FOX_TK
    blocks_per_q = FOX_TQ // FOX_TK
    a_w = g_v.shape[1]
    f_w = w_vt.shape[0]
    n_heads = f_w // 64
    route = _route_matrix(n_heads)
    grid = (B, S // tm)
    row3 = lambda w: pl.BlockSpec((1, tm, w), lambda b, t: (b, t, 0))
    return pl.pallas_call(
        functools.partial(_even_in_kernel, blocks_per_q=blocks_per_q),
        grid=grid,
        in_specs=[row3(D), _const_spec(g_mix.shape), _const_spec(w_main.shape), _const_spec(w_vt.shape),
                  _const_spec(w_f.shape), _const_spec(b_f.shape), _const_spec(g_v.shape), _const_spec(w_s.shape),
                  _const_spec(b_s_t.shape), _const_spec(g_q2.shape), _const_spec(g_k2.shape),
                  _const_spec(tri.shape), _const_spec(route.shape)],
        out_specs=[row3(a_w), row3(n_heads * LANES), row3(n_heads * LANES),
                   pl.BlockSpec((1, n_heads, LANES, tm), lambda b, t: (b, 0, 0, t)),
                   pl.BlockSpec((1, 8, LANES), lambda b, t: (b, (t * nsub) // blocks_per_q, 0)),
                   pl.BlockSpec((1, 8 * nsub, LANES), lambda b, t: (b, t, 0))],
        out_shape=[jax.ShapeDtypeStruct((B, S, a_w), BF16),
                   jax.ShapeDtypeStruct((B, S, n_heads * LANES), BF16),
                   jax.ShapeDtypeStruct((B, S, n_heads * LANES), BF16),
                   jax.ShapeDtypeStruct((B, n_heads, LANES, S), BF16),
                   jax.ShapeDtypeStruct((B, (S // FOX_TQ) * 8, LANES), F32),
                   jax.ShapeDtypeStruct((B, (S // FOX_TK) * 8, LANES), F32)],
        scratch_shapes=[pltpu.VMEM((1, LANES), F32), pltpu.VMEM((1, LANES), F32), pltpu.VMEM((1, LANES), F32)],
        compiler_params=pltpu.CompilerParams(
            dimension_semantics=("arbitrary", "arbitrary"),
            vmem_limit_bytes=_vmem_limit(48 * 1024 * 1024)),
        name="even_in",
    )(x, g_mix, w_main, w_vt, w_f, b_f, g_v, w_s, b_s_t, g_q2, g_k2, tri, route)


def _fox_kernel(cs_ref, ce_ref, qp_ref, kp_ref, vt_ref, o_ref, s_buf, mx_buf, p_buf, al_buf, m_scr, acc_scr,
                *, nq, nk, n_heads):
    b = pl.program_id(0)
    hp = pl.program_id(1)
    tq, tk = FOX_TQ, FOX_TK
    sub = tq // tk
    T = MXU_TILE
    nct = tq // T
    nkt = tk // T
    zslab = jnp.zeros((T, LANES), BF16)

    def key_tiles(c, col_lo, diag):
        return [kt for kt in range(nkt) if not diag or kt <= c - col_lo // T]

    def qk(i, j, masked, col_lo=0):
        q0 = pl.multiple_of(i * tq, tq)
        k0 = pl.multiple_of(j * tk, tk)
        for c in range(col_lo // T, nct):
            mxu = c % 2
            pltpu.matmul_push_rhs(qp_ref[0, pl.ds(q0 + c * T, T), :], staging_register=0, mxu_index=mxu,
                                  transpose=True)
            prods = [(hh, kt) for hh in range(2) for kt in key_tiles(c, col_lo, masked)]
            cmax = {}

            def issue(n):
                hh, kt = prods[n]
                ks = kp_ref[0, pl.ds(k0 + kt * T, T), hh * LANES:(hh + 1) * LANES]
                lhs = jnp.concatenate([ks, zslab] if hh == 0 else [zslab, ks], axis=1)
                pltpu.matmul_acc_lhs((n % 2) * (T // 4), lhs, mxu, load_staged_rhs=0 if n == 0 else None)

            def drain(n):
                hh, kt = prods[n]
                st = pltpu.matmul_pop((n % 2) * (T // 4), (T, T), F32, mxu)
                if masked and kt == c - col_lo // T:
                    ri = lax.broadcasted_iota(jnp.int32, (T, T), 0) + kt * T
                    ci = lax.broadcasted_iota(jnp.int32, (T, T), 1) + (c * T - col_lo)
                    st = jnp.where(ci >= ri, st, NEG_BIG)
                s_buf[hh, kt * T:(kt + 1) * T, c * T:(c + 1) * T] = st
                cm = jnp.max(st, axis=0, keepdims=True)
                cmax[hh] = cm if hh not in cmax else jnp.maximum(cmax[hh], cm)

            issue(0)
            for n in range(len(prods)):
                if n + 1 < len(prods):
                    issue(n + 1)
                drain(n)
            for hh in range(2):
                mx_buf[hh, :, c * T:(c + 1) * T] = jnp.broadcast_to(cmax[hh], (8, T))

    def ex(i, j, par, col_lo=0, diag=False):
        for hh in range(2):
            head = hp * 2 + hh
            d = cs_ref[(b * nq + i) * n_heads + head] - ce_ref[(b * nk + j) * n_heads + head]
            for c in range(col_lo // T, nct):
                cs_ = slice(c * T, (c + 1) * T)
                m_old = m_scr[hh, :, cs_]
                m_new = jnp.maximum(m_old, mx_buf[hh, :, cs_] + d)
                al_buf[par, hh, :, cs_] = jnp.exp2(m_old - m_new)
                m_scr[hh, :, cs_] = m_new
                shift = jnp.broadcast_to((m_new - d)[0:1], (T, T))
                for kt in key_tiles(c, col_lo, diag):
                    rs = slice(kt * T, (kt + 1) * T)
                    p_buf[par, hh, rs, cs_] = jnp.exp2(s_buf[hh, rs, cs_] - shift).astype(BF16)

    def pv(i, j, par, col_lo=0, diag=False):
        k0 = pl.multiple_of(j * tk, tk)
        n = 0
        for hh in range(2):
            for c in range(col_lo // T, nct):
                mxu = c % 2
                cs_ = slice(c * T, (c + 1) * T)
                a_pv = 2 * (T // 4) + (LANES // 4) * ((n // 2) % 4)
                n += 1
                for kt in key_tiles(c, col_lo, diag):
                    pltpu.matmul_push_rhs(p_buf[par, hh, kt * T:(kt + 1) * T, cs_], staging_register=1,
                                          mxu_index=mxu)
                    pltpu.matmul_acc_lhs(a_pv, vt_ref[0, hh, :, pl.ds(k0 + kt * T, T)], mxu, load_staged_rhs=1)
                out = pltpu.matmul_pop(a_pv, (LANES, T), F32, mxu)
                acc_scr[hh, :, cs_] = (jnp.broadcast_to(al_buf[par, hh, 0:1, cs_], (LANES, T)) * acc_scr[hh, :, cs_]
                                       + out)

    def tail(i, n_full, has_full_blocks):
        for t in range(sub + 2):
            for stage, off in ((pv, t - 2), (ex, t - 1)):
                if off >= 0:
                    stage(i, n_full + off, off % 2, off * tk, True)
                elif has_full_blocks:
                    stage(i, n_full + off, off % 2)
            if t < sub:
                qk(i, n_full + t, True, t * tk)

    def q_body(i, carry):
        q0 = pl.multiple_of(i * tq, tq)
        n_full = i * sub
        for hh in range(2):
            m_scr[hh] = jnp.full((8, tq), NEG_BIG, F32)
            acc_scr[hh, :, 0:tq] = jnp.zeros((LANES, tq), F32)

        @pl.when(i == 0)
        def _():
            tail(i, 0, False)

        @pl.when(i > 0)
        def _():
            assert sub >= 2 and sub % 2 == 0
            qk(i, 0, False)
            ex(i, 0, 0)
            qk(i, 1, False)

            def body(u, c):
                s = 2 + 2 * u
                pv(i, s - 2, 0)
                ex(i, s - 1, 1)
                qk(i, s, False)
                pv(i, s - 1, 1)
                ex(i, s, 0)
                qk(i, s + 1, False)
                return c

            lax.fori_loop(0, (n_full - 2) // 2, body, 0)
            tail(i, n_full, True)

        hd = LANES // 2
        tops = []
        for hh in range(2):
            a = acc_scr[hh, :, 0:tq]
            tops.append(a[0:hd] * (1.0 / a[hd:2 * hd]))
        o_ref[0, pl.ds(q0, tq), :] = jnp.concatenate(tops, axis=0).T.astype(BF16)
        return carry

    lax.fori_loop(0, nq, q_body, 0)


def _fox(cs_flat, ce_flat, qp, kp, vt):
    B, S, HW = qp.shape
    n_heads = HW // LANES
    nq, nk = S // FOX_TQ, S // FOX_TK
    assert FOX_TQ % MXU_TILE == 0 and FOX_TK % MXU_TILE == 0 and 2 * LANES == MXU_TILE
    grid = (B, n_heads // 2)
    slab = pl.BlockSpec((1, S, 2 * LANES), lambda b, h: (b, 0, h))
    stat = pltpu.VMEM((2, 8, FOX_TQ), F32)
    return pl.pallas_call(
        functools.partial(_fox_kernel, nq=nq, nk=nk, n_heads=n_heads),
        grid=grid,
        in_specs=[pl.BlockSpec(memory_space=pltpu.SMEM), pl.BlockSpec(memory_space=pltpu.SMEM),
                  slab, slab, pl.BlockSpec((1, 2, LANES, S), lambda b, h: (b, h, 0, 0))],
        out_specs=pl.BlockSpec((1, S, LANES), lambda b, h: (b, 0, h)),
        out_shape=jax.ShapeDtypeStruct((B, S, n_heads * 64), BF16),
        scratch_shapes=[pltpu.VMEM((2, FOX_TK, FOX_TQ + LANES), F32), stat,
                        pltpu.VMEM((2, 2, FOX_TK, FOX_TQ + LANES), BF16),
                        pltpu.VMEM((2, 2, 8, FOX_TQ), F32),
                        stat, pltpu.VMEM((2, LANES, FOX_TQ + LANES), F32)],
        compiler_params=pltpu.CompilerParams(
            dimension_semantics=("arbitrary", "arbitrary"),
            vmem_limit_bytes=_vmem_limit(56 * 1024 * 1024)),
        name="fox_attn",
    )(cs_flat, ce_flat, qp, kp, vt)


def _odd_mix(x_tile, t, g_ref, w_ref, wp_ref, sp_ref, cw_ref, zbuf, xbuf, windows):
    tm = x_tile.shape[0]
    pw = sp_ref.shape[1]
    cwid = cw_ref.shape[1]

    @pl.when(t == 0)
    def _():
        zbuf[0:POOL_HALO, :] = jnp.zeros((POOL_HALO, pw), F32)
        xbuf[0:CONV_HALO, :] = jnp.zeros((CONV_HALO, cwid), F32)

    h = _rms(x_tile, g_ref[...]).astype(BF16)
    z = _dot(h, w_ref[...])
    zc = z[:, :pw]
    hdn = z[:, pw:pw + cwid]
    gb = z[:, pw + cwid:pw + 2 * cwid]
    gc = z[:, pw + 2 * cwid:pw + 3 * cwid]

    zbuf[POOL_HALO:POOL_HALO + tm, :] = zc
    pos = t * tm + lax.broadcasted_iota(jnp.int32, (tm, 1), 0)
    yc = []
    for g, w in enumerate(windows):
        sl = slice(g * LANES, (g + 1) * LANES)
        assert w & (w - 1) == 0
        acc = zbuf[:, sl]
        k = 1
        while k < w:
            acc = acc + pltpu.roll(acc, k, axis=0)
            k *= 2
        acc = acc[POOL_HALO:POOL_HALO + tm]
        inv_cnt = 1.0 / jnp.minimum(pos + 1, w).astype(F32)
        p = acc * inv_cnt - zc[:, sl]
        yc.append((_dot(p.astype(BF16), wp_ref[g]) * sp_ref[:, sl]).astype(BF16))
    zbuf[0:POOL_HALO, :] = zbuf[tm:tm + POOL_HALO, :]

    xg = gc * hdn
    xbuf[CONV_HALO:CONV_HALO + tm, :] = xg
    k = cw_ref.shape[0]
    conv = cw_ref[k - 1:k, :] * xg
    for j in range(1, k):
        conv = conv + cw_ref[k - 1 - j:k - j, :] * xbuf[CONV_HALO - j:CONV_HALO - j + tm, :]
    yd = (gb * conv).astype(BF16)
    xbuf[0:CONV_HALO, :] = xbuf[tm:tm + CONV_HALO, :]
    return jnp.concatenate(yc, axis=-1), yd


def _mem_kv_kernel(m_ref, g_ref, w_ref, gk_ref, k_ref, v_ref):
    xa = k_ref.shape[2]
    hm = _rms(m_ref[0], g_ref[...]).astype(BF16)
    kv = _dot(hm, w_ref[...].astype(BF16))
    for h in range(xa // LANES):
        sl = slice(h * LANES, (h + 1) * LANES)
        k_ref[0, :, sl] = _rms(kv[:, sl], gk_ref[...]).astype(BF16)
    v_ref[0] = kv[:, xa:].astype(BF16)


def _mem_kv(mem, g_mem, w_kv, g_k):
    B, M, D = mem.shape
    L = w_kv.shape[0]
    xa = w_kv.shape[2] // 2
    per_layer = lambda a: pl.BlockSpec((None,) + a.shape[1:], lambda l, b: (l,) + (0,) * (a.ndim - 1))
    out = pl.BlockSpec((None, 1, M, xa), lambda l, b: (l, b, 0, 0))
    g_mem3, g_k3 = g_mem.reshape(L, 1, D), g_k.reshape(L, 1, -1)
    return pl.pallas_call(
        _mem_kv_kernel,
        grid=(L, B),
        in_specs=[pl.BlockSpec((1, M, D), lambda l, b: (b, 0, 0)), per_layer(g_mem3), per_layer(w_kv),
                  per_layer(g_k3)],
        out_specs=[out, out],
        out_shape=[jax.ShapeDtypeStruct((L, B, M, xa), BF16), jax.ShapeDtypeStruct((L, B, M, xa), BF16)],
        compiler_params=pltpu.CompilerParams(dimension_semantics=("arbitrary", "arbitrary"),
                                             vmem_limit_bytes=_vmem_limit(40 * 1024 * 1024)),
        name="mem_kv",
    )(mem, g_mem3, w_kv, g_k3)


def _post_tile(x_tile, ya, yb, woa_ref, wob_ref, gxa_ref, wq_ref, gq_ref, k_ref, v_ref, wo_ref,
               gff_ref, wg_ref, wu_ref, wd_ref, hs_scr):
    xa = wq_ref.shape[1]
    dff = wg_ref.shape[1]
    inv_sqrt = 1.0 / math.sqrt(LANES)
    x1 = x_tile + _dot(ya, woa_ref[...]) + _dot(yb, wob_ref[...])

    hx = _rms(x1, gxa_ref[...]).astype(BF16)
    q = _dot(hx, wq_ref[...])
    outs = []
    for h in range(xa // LANES):
        sl = slice(h * LANES, (h + 1) * LANES)
        qn = _rms(q[:, sl], gq_ref[...]).astype(BF16)
        s = _dot_nt(qn, k_ref[0, :, sl]) * inv_sqrt
        m = jnp.max(s, axis=-1, keepdims=True)
        p = jnp.exp(s - m)
        l = jnp.sum(p, axis=-1, keepdims=True)
        outs.append((_dot(p.astype(BF16), v_ref[0, :, sl]) * (1.0 / l)).astype(BF16))
    x2 = x1 + _dot(jnp.concatenate(outs, axis=-1), wo_ref[...])

    hf = _rms(x2, gff_ref[...]).astype(BF16)
    c0 = 0
    while c0 < dff:
        c1 = min(c0 + FF_CHUNK, dff)
        a = _dot(hf, wg_ref[:, c0:c1])
        u = _dot(hf, wu_ref[:, c0:c1])
        hs_scr[:, c0:c1] = (a * jax.nn.sigmoid(a) * u).astype(BF16)
        c0 = c1
    return x2 + _dot(hs_scr[...], wd_ref[...])


def _post_kernel(x_ref, ya_ref, yb_ref, *rest):
    *w_refs, o_ref, hs_scr = rest
    o_ref[0] = _post_tile(x_ref[0], ya_ref[0], yb_ref[0], *w_refs, hs_scr)


def _odd_layer_kernel(x_ref, gmix_ref, win_ref, wp_ref, sp_ref, cw_ref, *rest, windows):
    *w_refs, o_ref, hs_scr, zbuf, xbuf = rest
    x_tile = x_ref[0]
    yc, yd = _odd_mix(x_tile, pl.program_id(1), gmix_ref, win_ref, wp_ref, sp_ref, cw_ref, zbuf, xbuf, windows)
    o_ref[0] = _post_tile(x_tile, yc, yd, *w_refs, hs_scr)


def _layer_spec(stack, idx):
    return pl.BlockSpec((None,) + stack.shape[1:], lambda b, t: (idx, 0, 0), pipeline_mode=pl.Buffered(1))


def _post_specs(D, w_out, g_xa, w_q, g_q, k_mem, w_o, g_ffn, w_gate, w_up, w_down):
    kv_stack, kv_idx = k_mem
    M, xa = kv_stack.shape[2], w_q[0].shape[2]
    wo_stack, wo_idx = w_out
    half = wo_stack.shape[1] // 2
    memb = pl.BlockSpec((None, 1, M, xa), lambda b, t: (kv_idx, b, 0, 0))
    cs = lambda a: _const_spec(a.shape, single=True)
    wo_half = lambda k: pl.BlockSpec((None, half, D), lambda b, t: (wo_idx, k, 0), pipeline_mode=pl.Buffered(1))
    return [wo_half(0), wo_half(1), cs(g_xa), _layer_spec(*w_q), cs(g_q), memb, memb, _layer_spec(*w_o),
            cs(g_ffn), _layer_spec(*w_gate), _layer_spec(*w_up), _layer_spec(*w_down)]


def _post(x, ya, yb, w_out, g_xa, w_q, g_q, k_mem, v_mem, w_o, g_ffn, w_gate, w_up, w_down):
    B, S, D = x.shape
    tm = POST_TILE
    a_w, b_w = ya.shape[2], yb.shape[2]
    assert a_w == b_w and w_out[0].shape[1] == a_w + b_w
    row3 = lambda w: pl.BlockSpec((1, tm, w), lambda b, t: (b, t, 0))
    return pl.pallas_call(
        _post_kernel,
        grid=(B, S // tm),
        in_specs=[row3(D), row3(a_w), row3(b_w)] + _post_specs(D, w_out, g_xa, w_q, g_q, k_mem, w_o,
                                                              g_ffn, w_gate, w_up, w_down),
        out_specs=row3(D),
        out_shape=jax.ShapeDtypeStruct((B, S, D), F32),
        scratch_shapes=[pltpu.VMEM((tm, w_gate[0].shape[2]), BF16)],
        compiler_params=pltpu.CompilerParams(
            dimension_semantics=("arbitrary", "arbitrary"),
            vmem_limit_bytes=_vmem_limit(58 * 1024 * 1024)),
        name="post",
    )(x, ya, yb, w_out[0], w_out[0], g_xa, w_q[0], g_q, k_mem[0], v_mem[0], w_o[0], g_ffn,
      w_gate[0], w_up[0], w_down[0])


def _odd_layer(x, g_mix, w_in, w_pool, s_pool, conv_w, windows,
               w_out, g_xa, w_q, g_q, k_mem, v_mem, w_o, g_ffn, w_gate, w_up, w_down):
    B, S, D = x.shape
    tm = POST_TILE
    pw, cwid = s_pool.shape[1], conv_w.shape[1]
    assert max(windows) <= POOL_HALO and conv_w.shape[0] - 1 <= CONV_HALO
    assert pw == cwid and w_out[0].shape[1] == pw + cwid
    row3 = lambda w: pl.BlockSpec((1, tm, w), lambda b, t: (b, t, 0))
    cs = lambda a: _const_spec(a.shape, single=True)
    return pl.pallas_call(
        functools.partial(_odd_layer_kernel, windows=windows),
        grid=(B, S // tm),
        in_specs=[row3(D), cs(g_mix), cs(w_in), cs(w_pool), cs(s_pool), cs(conv_w)]
        + _post_specs(D, w_out, g_xa, w_q, g_q, k_mem, w_o, g_ffn, w_gate, w_up, w_down),
        out_specs=row3(D),
        out_shape=jax.ShapeDtypeStruct((B, S, D), F32),
        scratch_shapes=[pltpu.VMEM((tm, w_gate[0].shape[2]), BF16),
                        pltpu.VMEM((POOL_HALO + tm, pw), F32), pltpu.VMEM((CONV_HALO + tm, cwid), F32)],
        compiler_params=pltpu.CompilerParams(
            dimension_semantics=("arbitrary", "arbitrary"),
            vmem_limit_bytes=_vmem_limit(58 * 1024 * 1024)),
        name="odd_layer",
    )(x, g_mix, w_in, w_pool, s_pool, conv_w,
      w_out[0], w_out[0], g_xa, w_q[0], g_q, k_mem[0], v_mem[0], w_o[0], g_ffn, w_gate[0], w_up[0], w_down[0])


def kernel(x, mem, g_mix, g_xa, g_mem, xa_wq, xa_wkv, xa_wo, xa_gq, xa_gk, g_ffn, w_gate, w_up, w_down,
           e_w_in, e_b_f, e_g_v, e_w_s, e_b_s, e_g_qn, e_g_kn, e_w_out,
           o_w_in, o_w_pool, o_s_pool, o_conv_w, o_w_out):
    depth = g_mix.shape[0]
    S = x.shape[1]
    assert S % FOX_TQ == 0 and S % POST_TILE == 0 and S % ROW_TILE == 0
    row = lambda a: a.reshape(1, -1)
    tri = (lax.broadcasted_iota(jnp.int32, (FOX_TK, FOX_TK), 0)
           >= lax.broadcasted_iota(jnp.int32, (FOX_TK, FOX_TK), 1)).astype(BF16)
    pool_windows = (2, 4, 8, 16)[:o_w_pool.shape[1]]
    xa_wq_b, xa_wo_b, w_gate_b, w_up_b, w_down_b, e_w_out_b, o_w_out_b = (
        w.astype(BF16) for w in (xa_wq, xa_wo, w_gate, w_up, w_down, e_w_out, o_w_out))
    k_all, v_all = _mem_kv(mem, g_mem, xa_wkv, xa_gk)

    for layer in range(depth):
        i = layer // 2
        if layer % 2 == 0:
            n_heads = e_b_f.shape[1]
            a_w = e_g_v.shape[1]
            f_w = n_heads * e_g_qn.shape[1]
            n_uvqk = 2 * a_w + 2 * f_w
            w_main = e_w_in[i].astype(BF16)
            w_vt = w_main[:, n_uvqk:n_uvqk + f_w].T
            w_f = jnp.pad(w_main[:, n_uvqk + f_w:], ((0, 0), (0, LANES - n_heads)))
            b_f = jnp.pad(e_b_f[i], (0, LANES - n_heads)).reshape(1, LANES)
            g_q2 = jnp.tile(e_g_qn[i], 2).reshape(1, LANES)
            g_k2 = jnp.tile(e_g_kn[i], 2).reshape(1, LANES)
            ya, qp, kp, vt, cs, ce = _even_in(x, row(g_mix[layer]), w_main, w_vt, w_f, b_f, row(e_g_v[i]),
                                              e_w_s[i], e_b_s[i].T, g_q2, g_k2, tri)
            yb = _fox(cs[:, ::8, :n_heads].reshape(-1), ce[:, ::8, :n_heads].reshape(-1), qp, kp, vt)
        post_args = (row(g_xa[layer]), (xa_wq_b, layer), row(xa_gq[layer]), (k_all, layer), (v_all, layer),
                     (xa_wo_b, layer), row(g_ffn[layer]),
                     (w_gate_b, layer), (w_up_b, layer), (w_down_b, layer))
        if layer % 2 == 0:
            x = _post(x, ya, yb, (e_w_out_b, i), *post_args)
        else:
            x = _odd_layer(x, row(g_mix[layer]), o_w_in[i].astype(BF16), o_w_pool[i].astype(BF16),
                           row(o_s_pool[i]), o_conv_w[i], pool_windows, (o_w_out_b, i), *post_args)
    return x
```

```python
import functools
import math

import jax
import jax.numpy as jnp
import numpy as np
from jax import lax
from jax.experimental import pallas as pl
from jax.experimental.pallas import tpu as pltpu

F32 = jnp.float32
BF16 = jnp.bfloat16
EPS = 1e-6
LOG2E = 1.4426950408889634
NEG_BIG = -1e30

LANES = 128
MXU_TILE = 256
V7X_VMEM_BYTES = 64 * 1024 * 1024

ROW_TILE = 1024
POST_TILE = 1024
FOX_TQ = 2048
FOX_TK = 512
POOL_HALO = 16
CONV_HALO = 8
FF_CHUNK = 256


def _vmem_limit(nbytes):
    return int(min(nbytes, V7X_VMEM_BYTES - 4 * 1024 * 1024))


def _rms(x, g):
    ms = jnp.mean(x * x, axis=-1, keepdims=True)
    return (x * lax.rsqrt(ms + EPS)) * g


def _dot(a, b):
    return jnp.dot(a, b, preferred_element_type=F32)


def _dot_nt(a, b):
    return lax.dot_general(a, b, (((1,), (1,)), ((), ())), preferred_element_type=F32)


def _split3(x):
    hi = x.astype(BF16)
    r1 = x - hi.astype(F32)
    mid = r1.astype(BF16)
    lo = (r1 - mid.astype(F32)).astype(BF16)
    return hi, mid, lo


def _const_spec(shape, single=False):
    nd = len(shape)
    kw = {}
    if single:
        kw["pipeline_mode"] = pl.Buffered(1)
    return pl.BlockSpec(shape, lambda *_: (0,) * nd, **kw)


def _even_in_kernel(x_ref, g_ref, w_ref, wvt_ref, wf_ref, bf_ref, gv_ref, ws_ref, bst_ref, gq_ref, gk_ref,
                    tri_ref, route_ref, ya_ref, qp_ref, kp_ref, vt_ref, cs_ref, ce_ref, run_ref, aoff_ref, cq0_ref,
                    *, blocks_per_q):
    t = pl.program_id(1)
    tm = x_ref.shape[1]
    a_w = ya_ref.shape[2]
    f_w = qp_ref.shape[2] // 2
    n_grp = a_w // LANES

    @pl.when(t == 0)
    def _():
        run_ref[...] = jnp.zeros_like(run_ref)

    h = _rms(x_ref[0], g_ref[...]).astype(BF16)
    z = _dot(h, w_ref[:, 0:2 * a_w + 2 * f_w])
    zvt = _dot_nt(wvt_ref[...], h)
    fl = _dot(h, wf_ref[...]) + bf_ref[...]
    logf = -(jnp.maximum(-fl, 0.0) + jnp.log1p(jnp.exp(-jnp.abs(fl)))) * LOG2E

    uv = jax.nn.gelu(z[:, :2 * a_w])
    row = lax.broadcasted_iota(jnp.int32, (LANES, LANES), 0) // 64
    col = lax.broadcasted_iota(jnp.int32, (LANES, LANES), 1) // 64
    chunk_mask = row >= col
    for g in range(n_grp):
        sl = slice(g * LANES, (g + 1) * LANES)
        vg = uv[:, a_w + g * LANES:a_w + (g + 1) * LANES]
        vn = _rms(vg, gv_ref[:, sl]).astype(BF16)
        wm = jnp.where(chunk_mask, ws_ref[g], 0.0).astype(BF16)
        bias = bst_ref[:, g:g + 1]
        nb = tm // LANES
        v_cat = jnp.concatenate([vn[n * LANES:(n + 1) * LANES] for n in range(nb)], axis=1)
        s_cat = _dot(wm, v_cat)
        for n in range(nb):
            rs = slice(n * LANES, (n + 1) * LANES)
            s = s_cat[:, n * LANES:(n + 1) * LANES] + bias
            ya_ref[0, rs, sl] = (uv[rs, sl] * s).astype(BF16)

    tri = tri_ref[...]
    nsub = tm // FOX_TK
    run = run_ref[...]
    a_off = aoff_ref[...]
    c_q0 = cq0_ref[...]
    a_parts, b_parts = [], []
    for r in range(nsub):
        lf = logf[r * FOX_TK:(r + 1) * FOX_TK]
        hi, mid, lo = _split3(lf)
        lc = _dot(tri, hi) + _dot(tri, mid) + _dot(tri, lo)
        first = lf[0:1]
        tot = lc[FOX_TK - 1:FOX_TK]
        q_start = ((t * nsub + r) % blocks_per_q) == 0
        a_off = jnp.where(q_start, -first, a_off)
        c_q0 = jnp.where(q_start, run + first, c_q0)
        a_parts.append(a_off + lc)
        b_parts.append(tot - lc)
        ce_ref[0, r * 8:(r + 1) * 8] = jnp.broadcast_to(run + tot, (8, LANES))
        a_off = a_off + tot
        run = run + tot
    a_all = jnp.concatenate(a_parts, axis=0)
    b_all = jnp.concatenate(b_parts, axis=0)
    cs_ref[0] = jnp.broadcast_to(c_q0, (8, LANES))
    aoff_ref[...] = a_off
    cq0_ref[...] = c_q0
    run_ref[...] = run

    lane = lax.broadcasted_iota(jnp.int32, (tm, LANES), 1)
    low = lane < 64
    hd = 64
    q_off = 2 * a_w
    k_off = 2 * a_w + f_w

    def head_norm(blk, gain):
        sq = blk * blk
        s_lo = jnp.sum(jnp.where(low, sq, 0.0), axis=-1, keepdims=True)
        s_hi = jnp.sum(jnp.where(low, 0.0, sq), axis=-1, keepdims=True)
        r = jnp.where(low, lax.rsqrt(s_lo / hd + EPS), lax.rsqrt(s_hi / hd + EPS))
        return (blk * r) * gain

    n_heads = f_w // hd
    ab3 = jnp.concatenate(list(_split3(a_all)) + list(_split3(b_all)), axis=-1)
    routed = _dot(ab3, route_ref[...])
    l64 = lane & 63
    x_q = jnp.where((l64 >= 3 * n_heads) & (l64 < 6 * n_heads), 1.0, routed[:, :LANES])
    y_all = routed[:, LANES:]

    def k_extras(h):
        return jnp.where((l64 >= 3 * h) & (l64 < 3 * h + 3), 1.0,
                         jnp.where((l64 >= 3 * (n_heads + h)) & (l64 < 3 * (n_heads + h) + 3), y_all, 0.0))

    for j in range(f_w // LANES):
        sl = slice(j * LANES, (j + 1) * LANES)
        qn = head_norm(z[:, q_off + j * LANES:q_off + (j + 1) * LANES], gq_ref[...]) * (LOG2E / math.sqrt(hd))
        kn = head_norm(z[:, k_off + j * LANES:k_off + (j + 1) * LANES], gk_ref[...])
        for hh in range(2):
            hidx = 2 * j + hh
            qhalf = low if hh == 0 else jnp.logical_not(low)
            osl = slice(hidx * LANES, (hidx + 1) * LANES)
            qp_ref[0, :, osl] = jnp.where(qhalf, qn, x_q).astype(BF16)
            kp_ref[0, :, osl] = jnp.where(qhalf, kn, k_extras(hidx)).astype(BF16)

    for hidx in range(n_heads):
        vt_ref[0, hidx, 0:hd, :] = zvt[hidx * hd:(hidx + 1) * hd].astype(BF16)
        vt_ref[0, hidx, hd:2 * hd, :] = jnp.ones((hd, tm), BF16)


def _route_matrix(n_heads):
    assert 6 * n_heads <= 64
    r = np.zeros((6 * LANES, 2 * LANES), np.float32)
    for h in range(n_heads):
        for x in range(3):
            for half in (0, 64):
                r[x * LANES + h, half + 3 * h + x] = 1.0
                r[(3 + x) * LANES + h, LANES + half + 3 * (n_heads + h) + x] = 1.0
    return jnp.asarray(r, BF16)


def _even_in(x, g_mix, w_main, w_vt, w_f, b_f, g_v, w_s, b_s_t, g_q2, g_k2, tri):
    B, S, D = x.shape
    tm = ROW_TILE
    assert tm % FOX_TK == 0 and FOX_TQ % tm == 0
    nsub = tm // FOX_TK
    blocks_per_q = FOX_TQ // FOX_TK
    a_w = g_v.shape[1]
    f_w = w_vt.shape[0]
    n_heads = f_w // 64
    route = _route_matrix(n_heads)
    grid = (B, S // tm)
    row3 = lambda w: pl.BlockSpec((1, tm, w), lambda b, t: (b, t, 0))
    return pl.pallas_call(
        functools.partial(_even_in_kernel, blocks_per_q=blocks_per_q),
        grid=grid,
        in_specs=[row3(D), _const_spec(g_mix.shape), _const_spec(w_main.shape), _const_spec(w_vt.shape),
                  _const_spec(w_f.shape), _const_spec(b_f.shape), _const_spec(g_v.shape), _const_spec(w_s.shape),
                  _const_spec(b_s_t.shape), _const_spec(g_q2.shape), _const_spec(g_k2.shape),
                  _const_spec(tri.shape), _const_spec(route.shape)],
        out_specs=[row3(a_w), row3(n_heads * LANES), row3(n_heads * LANES),
                   pl.BlockSpec((1, n_heads, LANES, tm), lambda b, t: (b, 0, 0, t)),
                   pl.BlockSpec((1, 8, LANES), lambda b, t: (b, (t * nsub) // blocks_per_q, 0)),
                   pl.BlockSpec((1, 8 * nsub, LANES), lambda b, t: (b, t, 0))],
        out_shape=[jax.ShapeDtypeStruct((B, S, a_w), BF16),
                   jax.ShapeDtypeStruct((B, S, n_heads * LANES), BF16),
                   jax.ShapeDtypeStruct((B, S, n_heads * LANES), BF16),
                   jax.ShapeDtypeStruct((B, n_heads, LANES, S), BF16),
                   jax.ShapeDtypeStruct((B, (S // FOX_TQ) * 8, LANES), F32),
                   jax.ShapeDtypeStruct((B, (S // FOX_TK) * 8, LANES), F32)],
        scratch_shapes=[pltpu.VMEM((1, LANES), F32), pltpu.VMEM((1, LANES), F32), pltpu.VMEM((1, LANES), F32)],
        compiler_params=pltpu.CompilerParams(
            dimension_semantics=("arbitrary", "arbitrary"),
            vmem_limit_bytes=_vmem_limit(48 * 1024 * 1024)),
        name="even_in",
    )(x, g_mix, w_main, w_vt, w_f, b_f, g_v, w_s, b_s_t, g_q2, g_k2, tri, route)


def _fox_kernel(cs_ref, ce_ref, qp_ref, kp_ref, vt_ref, o_ref, s_buf, mx_buf, p_buf, al_buf, m_scr, acc_scr,
                *, nq, nk, n_heads):
    b = pl.program_id(0)
    hp = pl.program_id(1)
    tq, tk = FOX_TQ, FOX_TK
    sub = tq // tk
    T = MXU_TILE
    nct = tq // T
    nkt = tk // T
    zslab = jnp.zeros((T, LANES), BF16)

    def key_tiles(c, col_lo, diag):
        return [kt for kt in range(nkt) if not diag or kt <= c - col_lo // T]

    def qk(i, j, masked, col_lo=0):
        q0 = pl.multiple_of(i * tq, tq)
        k0 = pl.multiple_of(j * tk, tk)
        for c in range(col_lo // T, nct):
            mxu = c % 2
            pltpu.matmul_push_rhs(qp_ref[0, pl.ds(q0 + c * T, T), :], staging_register=0, mxu_index=mxu,
                                  transpose=True)
            prods = [(hh, kt) for hh in range(2) for kt in key_tiles(c, col_lo, masked)]
            cmax = {}

            def issue(n):
                hh, kt = prods[n]
                ks = kp_ref[0, pl.ds(k0 + kt * T, T), hh * LANES:(hh + 1) * LANES]
                lhs = jnp.concatenate([ks, zslab] if hh == 0 else [zslab, ks], axis=1)
                pltpu.matmul_acc_lhs((n % 2) * (T // 4), lhs, mxu, load_staged_rhs=0 if n == 0 else None)

            def drain(n):
                hh, kt = prods[n]
                st = pltpu.matmul_pop((n % 2) * (T // 4), (T, T), F32, mxu)
                if masked and kt == c - col_lo // T:
                    ri = lax.broadcasted_iota(jnp.int32, (T, T), 0) + kt * T
                    ci = lax.broadcasted_iota(jnp.int32, (T, T), 1) + (c * T - col_lo)
                    st = jnp.where(ci >= ri, st, NEG_BIG)
                s_buf[hh, kt * T:(kt + 1) * T, c * T:(c + 1) * T] = st
                cm = jnp.max(st, axis=0, keepdims=True)
                cmax[hh] = cm if hh not in cmax else jnp.maximum(cmax[hh], cm)

            issue(0)
            for n in range(len(prods)):
                if n + 1 < len(prods):
                    issue(n + 1)
                drain(n)
            for hh in range(2):
                mx_buf[hh, :, c * T:(c + 1) * T] = jnp.broadcast_to(cmax[hh], (8, T))

    def ex(i, j, par, col_lo=0, diag=False):
        for hh in range(2):
            head = hp * 2 + hh
            d = cs_ref[(b * nq + i) * n_heads + head] - ce_ref[(b * nk + j) * n_heads + head]
            for c in range(col_lo // T, nct):
                cs_ = slice(c * T, (c + 1) * T)
                m_old = m_scr[hh, :, cs_]
                m_new = jnp.maximum(m_old, mx_buf[hh, :, cs_] + d)
                al_buf[par, hh, :, cs_] = jnp.exp2(m_old - m_new)
                m_scr[hh, :, cs_] = m_new
                shift = jnp.broadcast_to((m_new - d)[0:1], (T, T))
                for kt in key_tiles(c, col_lo, diag):
                    rs = slice(kt * T, (kt + 1) * T)
                    p_buf[par, hh, rs, cs_] = jnp.exp2(s_buf[hh, rs, cs_] - shift).astype(BF16)

    def pv(i, j, par, col_lo=0, diag=False):
        k0 = pl.multiple_of(j * tk, tk)
        n = 0
        for hh in range(2):
            for c in range(col_lo // T, nct):
                mxu = c % 2
                cs_ = slice(c * T, (c + 1) * T)
                a_pv = 2 * (T // 4) + (LANES // 4) * ((n // 2) % 4)
                n += 1
                for kt in key_tiles(c, col_lo, diag):
                    pltpu.matmul_push_rhs(p_buf[par, hh, kt * T:(kt + 1) * T, cs_], staging_register=1,
                                          mxu_index=mxu)
                    pltpu.matmul_acc_lhs(a_pv, vt_ref[0, hh, :, pl.ds(k0 + kt * T, T)], mxu, load_staged_rhs=1)
                out = pltpu.matmul_pop(a_pv, (LANES, T), F32, mxu)
                acc_scr[hh, :, cs_] = (jnp.broadcast_to(al_buf[par, hh, 0:1, cs_], (LANES, T)) * acc_scr[hh, :, cs_]
                                       + out)

    def tail(i, n_full, has_full_blocks):
        for t in range(sub + 2):
            for stage, off in ((pv, t - 2), (ex, t - 1)):
                if off >= 0:
                    stage(i, n_full + off, off % 2, off * tk, True)
                elif has_full_blocks:
                    stage(i, n_full + off, off % 2)
            if t < sub:
                qk(i, n_full + t, True, t * tk)

    def q_body(i, carry):
        q0 = pl.multiple_of(i * tq, tq)
        n_full = i * sub
        for hh in range(2):
            m_scr[hh] = jnp.full((8, tq), NEG_BIG, F32)
            acc_scr[hh, :, 0:tq] = jnp.zeros((LANES, tq), F32)

        @pl.when(i == 0)
        def _():
            tail(i, 0, False)

        @pl.when(i > 0)
        def _():
            assert sub >= 2 and sub % 2 == 0
            qk(i, 0, False)
            ex(i, 0, 0)
            qk(i, 1, False)

            def body(u, c):
                s = 2 + 2 * u
                pv(i, s - 2, 0)
                ex(i, s - 1, 1)
                qk(i, s, False)
                pv(i, s - 1, 1)
                ex(i, s, 0)
                qk(i, s + 1, False)
                return c

            lax.fori_loop(0, (n_full - 2) // 2, body, 0)
            tail(i, n_full, True)

        hd = LANES // 2
        tops = []
        for hh in range(2):
            a = acc_scr[hh, :, 0:tq]
            tops.append(a[0:hd] * (1.0 / a[hd:2 * hd]))
        o_ref[0, pl.ds(q0, tq), :] = jnp.concatenate(tops, axis=0).T.astype(BF16)
        return carry

    lax.fori_loop(0, nq, q_body, 0)


def _fox(cs_flat, ce_flat, qp, kp, vt):
    B, S, HW = qp.shape
    n_heads = HW // LANES
    nq, nk = S // FOX_TQ, S // FOX_TK
    assert FOX_TQ % MXU_TILE == 0 and FOX_TK % MXU_TILE == 0 and 2 * LANES == MXU_TILE
    grid = (B, n_heads // 2)
    slab = pl.BlockSpec((1, S, 2 * LANES), lambda b, h: (b, 0, h))
    stat = pltpu.VMEM((2, 8, FOX_TQ), F32)
    return pl.pallas_call(
        functools.partial(_fox_kernel, nq=nq, nk=nk, n_heads=n_heads),
        grid=grid,
        in_specs=[pl.BlockSpec(memory_space=pltpu.SMEM), pl.BlockSpec(memory_space=pltpu.SMEM),
                  slab, slab, pl.BlockSpec((1, 2, LANES, S), lambda b, h: (b, h, 0, 0))],
        out_specs=pl.BlockSpec((1, S, LANES), lambda b, h: (b, 0, h)),
        out_shape=jax.ShapeDtypeStruct((B, S, n_heads * 64), BF16),
        scratch_shapes=[pltpu.VMEM((2, FOX_TK, FOX_TQ + LANES), F32), stat,
                        pltpu.VMEM((2, 2, FOX_TK, FOX_TQ + LANES), BF16),
                        pltpu.VMEM((2, 2, 8, FOX_TQ), F32),
                        stat, pltpu.VMEM((2, LANES, FOX_TQ + LANES), F32)],
        compiler_params=pltpu.CompilerParams(
            dimension_semantics=("arbitrary", "arbitrary"),
            vmem_limit_bytes=_vmem_limit(56 * 1024 * 1024)),
        name="fox_attn",
    )(cs_flat, ce_flat, qp, kp, vt)


def _odd_mix(x_tile, t, g_ref, w_ref, wp_ref, sp_ref, cw_ref, zbuf, xbuf, windows):
    tm = x_tile.shape[0]
    pw = sp_ref.shape[1]
    cwid = cw_ref.shape[1]

    @pl.when(t == 0)
    def _():
        zbuf[0:POOL_HALO, :] = jnp.zeros((POOL_HALO, pw), F32)
        xbuf[0:CONV_HALO, :] = jnp.zeros((CONV_HALO, cwid), F32)

    h = _rms(x_tile, g_ref[...]).astype(BF16)
    z = _dot(h, w_ref[...])
    zc = z[:, :pw]
    hdn = z[:, pw:pw + cwid]
    gb = z[:, pw + cwid:pw + 2 * cwid]
    gc = z[:, pw + 2 * cwid:pw + 3 * cwid]

    zbuf[POOL_HALO:POOL_HALO + tm, :] = zc
    pos = t * tm + lax.broadcasted_iota(jnp.int32, (tm, 1), 0)
    yc = []
    for g, w in enumerate(windows):
        sl = slice(g * LANES, (g + 1) * LANES)
        assert w & (w - 1) == 0
        acc = zbuf[:, sl]
        k = 1
        while k < w:
            acc = acc + pltpu.roll(acc, k, axis=0)
            k *= 2
        acc = acc[POOL_HALO:POOL_HALO + tm]
        inv_cnt = 1.0 / jnp.minimum(pos + 1, w).astype(F32)
        p = acc * inv_cnt - zc[:, sl]
        yc.append((_dot(p.astype(BF16), wp_ref[g]) * sp_ref[:, sl]).astype(BF16))
    zbuf[0:POOL_HALO, :] = zbuf[tm:tm + POOL_HALO, :]

    xg = gc * hdn
    xbuf[CONV_HALO:CONV_HALO + tm, :] = xg
    k = cw_ref.shape[0]
    conv = cw_ref[k - 1:k, :] * xg
    for j in range(1, k):
        conv = conv + cw_ref[k - 1 - j:k - j, :] * xbuf[CONV_HALO - j:CONV_HALO - j + tm, :]
    yd = (gb * conv).astype(BF16)
    xbuf[0:CONV_HALO, :] = xbuf[tm:tm + CONV_HALO, :]
    return jnp.concatenate(yc, axis=-1), yd


def _mem_kv_kernel(m_ref, g_ref, w_ref, gk_ref, k_ref, v_ref):
    xa = k_ref.shape[2]
    hm = _rms(m_ref[0], g_ref[...]).astype(BF16)
    kv = _dot(hm, w_ref[...].astype(BF16))
    for h in range(xa // LANES):
        sl = slice(h * LANES, (h + 1) * LANES)
        k_ref[0, :, sl] = _rms(kv[:, sl], gk_ref[...]).astype(BF16)
    v_ref[0] = kv[:, xa:].astype(BF16)


def _mem_kv(mem, g_mem, w_kv, g_k):
    B, M, D = mem.shape
    L = w_kv.shape[0]
    xa = w_kv.shape[2] // 2
    per_layer = lambda a: pl.BlockSpec((None,) + a.shape[1:], lambda l, b: (l,) + (0,) * (a.ndim - 1))
    out = pl.BlockSpec((None, 1, M, xa), lambda l, b: (l, b, 0, 0))
    g_mem3, g_k3 = g_mem.reshape(L, 1, D), g_k.reshape(L, 1, -1)
    return pl.pallas_call(
        _mem_kv_kernel,
        grid=(L, B),
        in_specs=[pl.BlockSpec((1, M, D), lambda l, b: (b, 0, 0)), per_layer(g_mem3), per_layer(w_kv),
                  per_layer(g_k3)],
        out_specs=[out, out],
        out_shape=[jax.ShapeDtypeStruct((L, B, M, xa), BF16), jax.ShapeDtypeStruct((L, B, M, xa), BF16)],
        compiler_params=pltpu.CompilerParams(dimension_semantics=("arbitrary", "arbitrary"),
                                             vmem_limit_bytes=_vmem_limit(40 * 1024 * 1024)),
        name="mem_kv",
    )(mem, g_mem3, w_kv, g_k3)


def _post_tile(x_tile, ya, yb, woa_ref, wob_ref, gxa_ref, wq_ref, gq_ref, k_ref, v_ref, wo_ref,
               gff_ref, wg_ref, wu_ref, wd_ref, hs_scr):
    xa = wq_ref.shape[1]
    dff = wg_ref.shape[1]
    inv_sqrt = 1.0 / math.sqrt(LANES)
    x1 = x_tile + _dot(ya, woa_ref[...]) + _dot(yb, wob_ref[...])

    hx = _rms(x1, gxa_ref[...]).astype(BF16)
    q = _dot(hx, wq_ref[...])
    outs = []
    for h in range(xa // LANES):
        sl = slice(h * LANES, (h + 1) * LANES)
        qn = _rms(q[:, sl], gq_ref[...]).astype(BF16)
        s = _dot_nt(qn, k_ref[0, :, sl]) * (inv_sqrt * LOG2E)
        m = jnp.max(s, axis=-1, keepdims=True)
        p = jnp.exp2(s - m).astype(BF16)
        v_ones = jnp.concatenate([v_ref[0, :, sl], jnp.ones((v_ref.shape[1], LANES), BF16)], axis=1)
        pv = _dot(p, v_ones)
        outs.append((pv[:, :LANES] * (1.0 / pv[:, LANES:])).astype(BF16))
    x2 = x1 + _dot(jnp.concatenate(outs, axis=-1), wo_ref[...])

    hf = _rms(x2, gff_ref[...]).astype(BF16)
    c0 = 0
    while c0 < dff:
        c1 = min(c0 + FF_CHUNK, dff)
        a = _dot(hf, wg_ref[:, c0:c1])
        u = _dot(hf, wu_ref[:, c0:c1])
        hs_scr[:, c0:c1] = (a * jax.nn.sigmoid(a) * u).astype(BF16)
        c0 = c1
    return x2 + _dot(hs_scr[...], wd_ref[...])


def _post_kernel(x_ref, ya_ref, yb_ref, *rest):
    *w_refs, o_ref, hs_scr = rest
    o_ref[0] = _post_tile(x_ref[0], ya_ref[0], yb_ref[0], *w_refs, hs_scr)


def _odd_layer_kernel(x_ref, gmix_ref, win_ref, wp_ref, sp_ref, cw_ref, *rest, windows):
    *w_refs, o_ref, hs_scr, zbuf, xbuf = rest
    x_tile = x_ref[0]
    yc, yd = _odd_mix(x_tile, pl.program_id(1), gmix_ref, win_ref, wp_ref, sp_ref, cw_ref, zbuf, xbuf, windows)
    o_ref[0] = _post_tile(x_tile, yc, yd, *w_refs, hs_scr)


def _layer_spec(stack, idx):
    return pl.BlockSpec((None,) + stack.shape[1:], lambda b, t: (idx, 0, 0), pipeline_mode=pl.Buffered(1))


def _post_specs(D, w_out, g_xa, w_q, g_q, k_mem, w_o, g_ffn, w_gate, w_up, w_down):
    kv_stack, kv_idx = k_mem
    M, xa = kv_stack.shape[2], w_q[0].shape[2]
    wo_stack, wo_idx = w_out
    half = wo_stack.shape[1] // 2
    memb = pl.BlockSpec((None, 1, M, xa), lambda b, t: (kv_idx, b, 0, 0))
    cs = lambda a: _const_spec(a.shape, single=True)
    wo_half = lambda k: pl.BlockSpec((None, half, D), lambda b, t: (wo_idx, k, 0), pipeline_mode=pl.Buffered(1))
    return [wo_half(0), wo_half(1), cs(g_xa), _layer_spec(*w_q), cs(g_q), memb, memb, _layer_spec(*w_o),
            cs(g_ffn), _layer_spec(*w_gate), _layer_spec(*w_up), _layer_spec(*w_down)]


def _post(x, ya, yb, w_out, g_xa, w_q, g_q, k_mem, v_mem, w_o, g_ffn, w_gate, w_up, w_down):
    B, S, D = x.shape
    tm = POST_TILE
    a_w, b_w = ya.shape[2], yb.shape[2]
    assert a_w == b_w and w_out[0].shape[1] == a_w + b_w
    row3 = lambda w: pl.BlockSpec((1, tm, w), lambda b, t: (b, t, 0))
    return pl.pallas_call(
        _post_kernel,
        grid=(B, S // tm),
        in_specs=[row3(D), row3(a_w), row3(b_w)] + _post_specs(D, w_out, g_xa, w_q, g_q, k_mem, w_o,
                                                              g_ffn, w_gate, w_up, w_down),
        out_specs=row3(D),
        out_shape=jax.ShapeDtypeStruct((B, S, D), F32),
        scratch_shapes=[pltpu.VMEM((tm, w_gate[0].shape[2]), BF16)],
        compiler_params=pltpu.CompilerParams(
            dimension_semantics=("arbitrary", "arbitrary"),
            vmem_limit_bytes=_vmem_limit(58 * 1024 * 1024)),
        name="post",
    )(x, ya, yb, w_out[0], w_out[0], g_xa, w_q[0], g_q, k_mem[0], v_mem[0], w_o[0], g_ffn,
      w_gate[0], w_up[0], w_down[0])


def _odd_layer(x, g_mix, w_in, w_pool, s_pool, conv_w, windows,
               w_out, g_xa, w_q, g_q, k_mem, v_mem, w_o, g_ffn, w_gate, w_up, w_down):
    B, S, D = x.shape
    tm = POST_TILE
    pw, cwid = s_pool.shape[1], conv_w.shape[1]
    assert max(windows) <= POOL_HALO and conv_w.shape[0] - 1 <= CONV_HALO
    assert pw == cwid and w_out[0].shape[1] == pw + cwid
    row3 = lambda w: pl.BlockSpec((1, tm, w), lambda b, t: (b, t, 0))
    cs = lambda a: _const_spec(a.shape, single=True)
    return pl.pallas_call(
        functools.partial(_odd_layer_kernel, windows=windows),
        grid=(B, S // tm),
        in_specs=[row3(D), cs(g_mix), cs(w_in), cs(w_pool), cs(s_pool), cs(conv_w)]
        + _post_specs(D, w_out, g_xa, w_q, g_q, k_mem, w_o, g_ffn, w_gate, w_up, w_down),
        out_specs=row3(D),
        out_shape=jax.ShapeDtypeStruct((B, S, D), F32),
        scratch_shapes=[pltpu.VMEM((tm, w_gate[0].shape[2]), BF16),
                        pltpu.VMEM((POOL_HALO + tm, pw), F32), pltpu.VMEM((CONV_HALO + tm, cwid), F32)],
        compiler_params=pltpu.CompilerParams(
            dimension_semantics=("arbitrary", "arbitrary"),
            vmem_limit_bytes=_vmem_limit(58 * 1024 * 1024)),
        name="odd_layer",
    )(x, g_mix, w_in, w_pool, s_pool, conv_w,
      w_out[0], w_out[0], g_xa, w_q[0], g_q, k_mem[0], v_mem[0], w_o[0], g_ffn, w_gate[0], w_up[0], w_down[0])


def kernel(x, mem, g_mix, g_xa, g_mem, xa_wq, xa_wkv, xa_wo, xa_gq, xa_gk, g_ffn, w_gate, w_up, w_down,
           e_w_in, e_b_f, e_g_v, e_w_s, e_b_s, e_g_qn, e_g_kn, e_w_out,
           o_w_in, o_w_pool, o_s_pool, o_conv_w, o_w_out):
    depth = g_mix.shape[0]
    S = x.shape[1]
    assert S % FOX_TQ == 0 and S % POST_TILE == 0 and S % ROW_TILE == 0
    row = lambda a: a.reshape(1, -1)
    tri = (lax.broadcasted_iota(jnp.int32, (FOX_TK, FOX_TK), 0)
           >= lax.broadcasted_iota(jnp.int32, (FOX_TK, FOX_TK), 1)).astype(BF16)
    pool_windows = (2, 4, 8, 16)[:o_w_pool.shape[1]]
    xa_wq_b, xa_wo_b, w_gate_b, w_up_b, w_down_b, e_w_out_b, o_w_out_b = (
        w.astype(BF16) for w in (xa_wq, xa_wo, w_gate, w_up, w_down, e_w_out, o_w_out))
    k_all, v_all = _mem_kv(mem, g_mem, xa_wkv, xa_gk)

    for layer in range(depth):
        i = layer // 2
        if layer % 2 == 0:
            n_heads = e_b_f.shape[1]
            a_w = e_g_v.shape[1]
            f_w = n_heads * e_g_qn.shape[1]
            n_uvqk = 2 * a_w + 2 * f_w
            w_main = e_w_in[i].astype(BF16)
            w_vt = w_main[:, n_uvqk:n_uvqk + f_w].T
            w_f = jnp.pad(w_main[:, n_uvqk + f_w:], ((0, 0), (0, LANES - n_heads)))
            b_f = jnp.pad(e_b_f[i], (0, LANES - n_heads)).reshape(1, LANES)
            g_q2 = jnp.tile(e_g_qn[i], 2).reshape(1, LANES)
            g_k2 = jnp.tile(e_g_kn[i], 2).reshape(1, LANES)
            ya, qp, kp, vt, cs, ce = _even_in(x, row(g_mix[layer]), w_main, w_vt, w_f, b_f, row(e_g_v[i]),
                                              e_w_s[i], e_b_s[i].T, g_q2, g_k2, tri)
            yb = _fox(cs[:, ::8, :n_heads].reshape(-1), ce[:, ::8, :n_heads].reshape(-1), qp, kp, vt)
        post_args = (row(g_xa[layer]), (xa_wq_b, layer), row(xa_gq[layer]), (k_all, layer), (v_all, layer),
                     (xa_wo_b, layer), row(g_ffn[layer]),
                     (w_gate_b, layer), (w_up_b, layer), (w_down_b, layer))
        if layer % 2 == 0:
            x = _post(x, ya, yb, (e_w_out_b, i), *post_args)
        else:
            x = _odd_layer(x, row(g_mix[layer]), o_w_in[i].astype(BF16), o_w_pool[i].astype(BF16),
                           row(o_s_pool[i]), o_conv_w[i], pool_windows, (o_w_out_b, i), *post_args)
    return x
```
